```python
import math
import jax, jax.numpy as jnp
from jax import lax
import numpy as np

D_MODEL = 2048
BATCH = 2
SEQ = 8192
DEPTH = 1

EPS = 1e-6
HALF_STEP = 0.5
MIX_WIDTH = D_MODEL
A_WIDTH = MIX_WIDTH // 2
A_HEAD_DIM = 128
A_HEADS = A_WIDTH // A_HEAD_DIM
A_CHUNK = 128
B_WIDTH = MIX_WIDTH - A_WIDTH
B_HEAD_DIM = 64
B_HEADS = B_WIDTH // B_HEAD_DIM
B_GROUPS = 2
B_STATE = 128
B_CONV = 4
B_CHUNK = 256
B_CONV_DIM = B_WIDTH + 2 * B_GROUPS * B_STATE
DT_MIN = 0.001
DT_MAX = 0.1
IN_PROJ_DIM = 2 * A_WIDTH + B_WIDTH + B_CONV_DIM + B_HEADS
D_FF = 5504

kernel_name = 'hybrid_gmlp_ssd_macaron_block'


def rmsnorm(x, g):
    xf = x.astype(jnp.float32)
    y = xf * lax.rsqrt(jnp.mean(xf * xf, axis=-1, keepdims=True) + EPS)
    return (y * g.astype(jnp.float32)).astype(x.dtype)


def swiglu(h, w_gate, w_up, w_down):
    return (jax.nn.silu(h @ w_gate) * (h @ w_up)) @ w_down


def causal_depthwise_conv(x, w, b):
    c = x.shape[-1]
    y = lax.conv_general_dilated(x, w.astype(x.dtype)[:, None, :], window_strides=(1,), padding=[(B_CONV - 1, 0)],
                                 dimension_numbers=('NWC', 'WIO', 'NWC'), feature_group_count=c)
    return y + b


def chunked_spatial_gating(u, v, w_s, b_s):
    bsz, seq, _ = u.shape
    nc = seq // A_CHUNK
    mask = jnp.tril(jnp.ones((A_CHUNK, A_CHUNK), dtype=bool))
    w = jnp.where(mask, w_s, 0)
    vc = v.reshape(bsz, nc, A_CHUNK, A_HEADS, A_HEAD_DIM)
    mixed = jnp.einsum('hts,bcshd->bcthd', w, vc) + b_s.T[None, None, :, :, None]
    return u * mixed.reshape(bsz, seq, A_WIDTH)


def segsum_exp(a_cs):
    n = a_cs.shape[-1]
    mask = jnp.tril(jnp.ones((n, n), dtype=bool))
    diff = a_cs[..., :, None] - a_cs[..., None, :]
    return jnp.exp(jnp.where(mask, diff, -jnp.inf))


def pad_seq(t, pad):
    return jnp.pad(t, [(0, 0), (0, pad)] + [(0, 0)] * (t.ndim - 2))


def ssd_chunked(xh, dt, a, bm, cm):
    bsz, seq = xh.shape[:2]
    pad = (-seq) % B_CHUNK
    xh, dt, bm, cm = pad_seq(xh, pad), pad_seq(dt, pad), pad_seq(bm, pad), pad_seq(cm, pad)
    nc = (seq + pad) // B_CHUNK
    r = B_HEADS // B_GROUPS
    x = (xh * dt[..., None]).reshape(bsz, nc, B_CHUNK, B_GROUPS, r, B_HEAD_DIM)
    da = jnp.moveaxis((dt * a).reshape(bsz, nc, B_CHUNK, B_GROUPS, r), 2, -1)
    a_cs = jnp.cumsum(da, axis=-1)
    bc = bm.reshape(bsz, nc, B_CHUNK, B_GROUPS, B_STATE)
    cc = cm.reshape(bsz, nc, B_CHUNK, B_GROUPS, B_STATE)
    decay_in = segsum_exp(a_cs)
    cb = jnp.einsum('bclgn,bcsgn->bcgls', cc, bc)
    y_diag = jnp.einsum('bcgls,bcgrls,bcsgrp->bclgrp', cb, decay_in, x)
    decay_to_end = jnp.exp(a_cs[..., -1:] - a_cs)
    states = jnp.einsum('bclgn,bcgrl,bclgrp->bcgrpn', bc, decay_to_end, x)
    chunk_decay = jnp.exp(a_cs[..., -1])

    def step(carry, inp):
        st, dec = inp
        return carry * dec[..., None, None] + st, carry

    init = jnp.zeros_like(states[:, 0])
    _, prev = lax.scan(step, init, (jnp.moveaxis(states, 1, 0), jnp.moveaxis(chunk_decay, 1, 0)))
    prev = jnp.moveaxis(prev, 0, 1)
    y_off = jnp.einsum('bclgn,bcgrpn,bcgrl->bclgrp', cc, prev, jnp.exp(a_cs))
    y = (y_diag + y_off).reshape(bsz, nc * B_CHUNK, B_HEADS, B_HEAD_DIM)
    return y[:, :seq]


def mamba2_mixer(z, xbc, dt_raw, conv_w, conv_b, dt_bias, a_log, d_skip, ssm_norm):
    bsz, seq, _ = z.shape
    f32 = jnp.float32
    xbc = jax.nn.silu(causal_depthwise_conv(xbc, conv_w, conv_b))
    xs, bm, cm = jnp.split(xbc, [B_WIDTH, B_WIDTH + B_GROUPS * B_STATE], axis=-1)
    dt = jax.nn.softplus(dt_raw.astype(f32) + dt_bias.astype(f32))
    a = -jnp.exp(a_log.astype(f32))
    xh = xs.astype(f32).reshape(bsz, seq, B_HEADS, B_HEAD_DIM)
    y = ssd_chunked(xh, dt, a,
                    bm.astype(f32).reshape(bsz, seq, B_GROUPS, B_STATE),
                    cm.astype(f32).reshape(bsz, seq, B_GROUPS, B_STATE))
    y = y + d_skip.astype(f32)[:, None] * xh
    y = y.reshape(bsz, seq, B_WIDTH) * jax.nn.silu(z.astype(f32))
    yg = y.reshape(bsz, seq, B_GROUPS, B_WIDTH // B_GROUPS)
    yg = yg * lax.rsqrt(jnp.mean(yg * yg, axis=-1, keepdims=True) + EPS)
    return (yg.reshape(bsz, seq, B_WIDTH) * ssm_norm.astype(f32)).astype(z.dtype)


def hybrid_mixer(h, w_in, sgu_norm, w_spatial, b_spatial, conv_w, conv_b, dt_bias, a_log, d_skip, ssm_norm, w_out):
    proj = h @ w_in
    u, v, z, xbc, dt_raw = jnp.split(
        proj, [A_WIDTH, 2 * A_WIDTH, 2 * A_WIDTH + B_WIDTH, 2 * A_WIDTH + B_WIDTH + B_CONV_DIM], axis=-1)
    u = jax.nn.gelu(u, approximate=False)
    v = rmsnorm(jax.nn.gelu(v, approximate=False), sgu_norm)
    y_a = chunked_spatial_gating(u, v, w_spatial, b_spatial)
    y_b = mamba2_mixer(z, xbc, dt_raw, conv_w, conv_b, dt_bias, a_log, d_skip, ssm_norm)
    return jnp.concatenate([y_a, y_b.astype(y_a.dtype)], axis=-1) @ w_out


def setup_inputs(seed: int = 0) -> dict:
    key = jax.random.key(seed)
    ks = jax.random.split(key, 24)
    f32 = jnp.float32
    L = DEPTH

    def nrm(k, shape, scale):
        return jax.random.normal(k, shape, f32) * scale

    def gain(k, shape):
        return 1.0 + 0.05 * jax.random.normal(k, shape, f32)

    x = nrm(ks[0], (BATCH, SEQ, D_MODEL), 1.0)
    ffn1_norm_pre = gain(ks[1], (L, D_MODEL))
    ffn1_norm_post = gain(ks[2], (L, D_MODEL))
    ffn1_w_gate = nrm(ks[3], (L, D_MODEL, D_FF), D_MODEL ** -0.5)
    ffn1_w_up = nrm(ks[4], (L, D_MODEL, D_FF), D_MODEL ** -0.5)
    ffn1_w_down = nrm(ks[5], (L, D_FF, D_MODEL), D_FF ** -0.5)
    mix_norm_pre = gain(ks[6], (L, D_MODEL))
    mix_norm_post = gain(ks[7], (L, D_MODEL))
    w_in = nrm(ks[8], (L, D_MODEL, IN_PROJ_DIM), D_MODEL ** -0.5)
    sgu_norm = gain(ks[9], (L, A_WIDTH))
    w_spatial = nrm(ks[10], (L, A_HEADS, A_CHUNK, A_CHUNK), A_CHUNK ** -0.5)
    b_spatial = 1.0 + 0.1 * jax.random.normal(ks[11], (L, A_HEADS, A_CHUNK), f32)
    conv_w = nrm(ks[12], (L, B_CONV, B_CONV_DIM), B_CONV ** -0.5)
    conv_b = nrm(ks[13], (L, B_CONV_DIM), 0.02)
    dt0 = jnp.exp(jax.random.uniform(ks[14], (L, B_HEADS), f32, math.log(DT_MIN), math.log(DT_MAX)))
    dt_bias = dt0 + jnp.log(-jnp.expm1(-dt0))
    a_log = jnp.log(jax.random.uniform(ks[15], (L, B_HEADS), f32, 1.0, 16.0))
    d_skip = gain(ks[16], (L, B_HEADS))
    ssm_norm = gain(ks[17], (L, B_WIDTH))
    w_out = nrm(ks[18], (L, MIX_WIDTH, D_MODEL), MIX_WIDTH ** -0.5)
    ffn2_norm_pre = gain(ks[19], (L, D_MODEL))
    ffn2_norm_post = gain(ks[20], (L, D_MODEL))
    ffn2_w_gate = nrm(ks[21], (L, D_MODEL, D_FF), D_MODEL ** -0.5)
    ffn2_w_up = nrm(ks[22], (L, D_MODEL, D_FF), D_MODEL ** -0.5)
    ffn2_w_down = nrm(ks[23], (L, D_FF, D_MODEL), D_FF ** -0.5)
    return {'x': x,
            'ffn1_norm_pre': ffn1_norm_pre, 'ffn1_norm_post': ffn1_norm_post,
            'ffn1_w_gate': ffn1_w_gate, 'ffn1_w_up': ffn1_w_up, 'ffn1_w_down': ffn1_w_down,
            'mix_norm_pre': mix_norm_pre, 'mix_norm_post': mix_norm_post, 'w_in': w_in,
            'sgu_norm': sgu_norm, 'w_spatial': w_spatial, 'b_spatial': b_spatial,
            'conv_w': conv_w, 'conv_b': conv_b, 'dt_bias': dt_bias, 'a_log': a_log,
            'd_skip': d_skip, 'ssm_norm': ssm_norm, 'w_out': w_out,
            'ffn2_norm_pre': ffn2_norm_pre, 'ffn2_norm_post': ffn2_norm_post,
            'ffn2_w_gate': ffn2_w_gate, 'ffn2_w_up': ffn2_w_up, 'ffn2_w_down': ffn2_w_down}


def reference(x, ffn1_norm_pre, ffn1_norm_post, ffn1_w_gate, ffn1_w_up, ffn1_w_down,
              mix_norm_pre, mix_norm_post, w_in, sgu_norm, w_spatial, b_spatial,
              conv_w, conv_b, dt_bias, a_log, d_skip, ssm_norm, w_out,
              ffn2_norm_pre, ffn2_norm_post, ffn2_w_gate, ffn2_w_up, ffn2_w_down):
    for l in range(DEPTH):
        h = swiglu(rmsnorm(x, ffn1_norm_pre[l]), ffn1_w_gate[l], ffn1_w_up[l], ffn1_w_down[l])
        x = x + HALF_STEP * rmsnorm(h, ffn1_norm_post[l])
        h = hybrid_mixer(rmsnorm(x, mix_norm_pre[l]), w_in[l], sgu_norm[l], w_spatial[l], b_spatial[l],
                         conv_w[l], conv_b[l], dt_bias[l], a_log[l], d_skip[l], ssm_norm[l], w_out[l])
        x = x + rmsnorm(h, mix_norm_post[l])
        h = swiglu(rmsnorm(x, ffn2_norm_pre[l]), ffn2_w_gate[l], ffn2_w_up[l], ffn2_w_down[l])
        x = x + HALF_STEP * rmsnorm(h, ffn2_norm_post[l])
    return x
```

```python
import functools

import jax
import jax.numpy as jnp
from jax import lax
from jax.experimental import pallas as pl
from jax.experimental.pallas import tpu as pltpu

F32 = jnp.float32
BF16 = jnp.bfloat16

EPS = 1e-6
HALF_STEP = 0.5
SQRT_HALF = 0.7071067811865476

A_HEAD_DIM = 128
A_CHUNK = 128
B_HEAD_DIM = 64
B_HEADS = 16
B_GROUPS = 2
B_STATE = 128
B_CONV = 4
B_CHUNK = 256
B_WIDTH = B_HEADS * B_HEAD_DIM
B_GROUP_WIDTH = B_WIDTH // B_GROUPS
B_CONV_DIM = B_WIDTH + 2 * B_GROUPS * B_STATE

LANES = 128
SUBLANES = 8
VMEM_LIMIT_BYTES = 56 * 1024 * 1024

FFN_ROW_TILE = 512
FFN_FF_TILE = 512
PROJ_ROW_TILE = 512


def _rmsnorm(x, g):
    return x * lax.rsqrt(jnp.mean(x * x, axis=-1, keepdims=True) + EPS) * g


def _gelu(x):
    return 0.5 * x * (1.0 + lax.erf(x * SQRT_HALF))


def _silu(x):
    return x * jax.nn.sigmoid(x)


def _dot(a, b):
    return jnp.dot(a, b, preferred_element_type=F32)


def _split3(v):
    hi = v.astype(BF16)
    r1 = v - hi.astype(F32)
    mid = r1.astype(BF16)
    lo = (r1 - mid.astype(F32)).astype(BF16)
    return hi, mid, lo


def _params(semantics):
    return pltpu.CompilerParams(dimension_semantics=semantics, vmem_limit_bytes=VMEM_LIMIT_BYTES)


def _ffn_body(*refs, emit_next):
    if emit_next:
        x_ref, gpre_ref, gpost_ref, gnext_ref, wg_ref, wu_ref, wd_ref, o_ref, on_ref, xn_ref = refs
    else:
        x_ref, gpre_ref, gpost_ref, wg_ref, wu_ref, wd_ref, o_ref, xn_ref = refs
    j = pl.program_id(1)
    nj = pl.num_programs(1)

    @pl.when(j == 0)
    def _():
        xn_ref[...] = _rmsnorm(x_ref[...], gpre_ref[...]).astype(BF16)

    xn = xn_ref[...]
    h = _dot(xn, wg_ref[...])
    u = _dot(xn, wu_ref[...])
    a = (_silu(h) * u).astype(BF16)
    p = _dot(a, wd_ref[...])

    @pl.when(j == 0)
    def _():
        o_ref[...] = p

    @pl.when(j > 0)
    def _():
        o_ref[...] += p

    @pl.when(j == nj - 1)
    def _():
        x_new = x_ref[...] + HALF_STEP * _rmsnorm(o_ref[...], gpost_ref[...])
        o_ref[...] = x_new
        if emit_next:
            on_ref[...] = _rmsnorm(x_new, gnext_ref[...]).astype(BF16)


def _ffn(x, g_pre, g_post, g_next, wg, wu, wd):
    t, d = x.shape
    fp = wg.shape[1]
    tm, tf = FFN_ROW_TILE, FFN_FF_TILE
    assert t % tm == 0 and fp % tf == 0
    emit_next = g_next is not None
    row = pl.BlockSpec((tm, d), lambda i, j: (i, 0))
    vec = pl.BlockSpec((1, d), lambda i, j: (0, 0))
    in_specs = [row, vec, vec] + ([vec] if emit_next else []) + [
        pl.BlockSpec((d, tf), lambda i, j: (0, j)),
        pl.BlockSpec((d, tf), lambda i, j: (0, j)),
        pl.BlockSpec((tf, d), lambda i, j: (j, 0)),
    ]
    out_shape = [jax.ShapeDtypeStruct((t, d), F32)]
    out_specs = [row]
    if emit_next:
        out_shape.append(jax.ShapeDtypeStruct((t, d), BF16))
        out_specs.append(row)
    args = [x, g_pre, g_post] + ([g_next] if emit_next else []) + [wg, wu, wd]
    outs = pl.pallas_call(
        functools.partial(_ffn_body, emit_next=emit_next),
        grid=(t // tm, fp // tf),
        in_specs=in_specs,
        out_specs=out_specs,
        out_shape=out_shape,
        scratch_shapes=[pltpu.VMEM((tm, d), BF16)],
        compiler_params=_params(("parallel", "arbitrary")),
        name="ffn_next" if emit_next else "ffn",
    )(*args)
    return outs if emit_next else outs[0]


def _sgu_body(xn_ref, wu_ref, wv_ref, gs_ref, ws_ref, bs_ref, o_ref, u_ref, v_ref):
    tm = xn_ref.shape[0]
    heads = ws_ref.shape[0]
    xn = xn_ref[...]
    u_ref[...] = _gelu(_dot(xn, wu_ref[...]))
    v_ref[...] = _rmsnorm(_gelu(_dot(xn, wv_ref[...])), gs_ref[...]).astype(BF16)
    rows = lax.broadcasted_iota(jnp.int32, (A_CHUNK, A_CHUNK), 0)
    cols = lax.broadcasted_iota(jnp.int32, (A_CHUNK, A_CHUNK), 1)
    causal = rows >= cols
    for h in range(heads):
        w = jnp.where(causal, ws_ref[h], 0.0).astype(BF16)
        bias = bs_ref[h]
        lanes = pl.ds(h * A_HEAD_DIM, A_HEAD_DIM)
        for c in range(tm // A_CHUNK):
            rws = pl.ds(c * A_CHUNK, A_CHUNK)
            mixed = _dot(w, v_ref[rws, lanes]) + bias
            o_ref[rws, lanes] = (u_ref[rws, lanes] * mixed).astype(BF16)


def _inproj_sgu(xn, w_u, w_v, g_sgu, w_spatial, b_full):
    t, d = xn.shape
    aw = w_u.shape[1]
    heads = w_spatial.shape[0]
    tm = PROJ_ROW_TILE
    assert t % tm == 0 and tm % A_CHUNK == 0 and aw == heads * A_HEAD_DIM
    const2 = lambda i: (0, 0)
    const3 = lambda i: (0, 0, 0)
    return pl.pallas_call(
        _sgu_body,
        grid=(t // tm,),
        in_specs=[
            pl.BlockSpec((tm, d), lambda i: (i, 0)),
            pl.BlockSpec((d, aw), const2),
            pl.BlockSpec((d, aw), const2),
            pl.BlockSpec((1, aw), const2),
            pl.BlockSpec((heads, A_CHUNK, A_CHUNK), const3),
            pl.BlockSpec((heads, A_CHUNK, A_HEAD_DIM), const3),
        ],
        out_specs=pl.BlockSpec((tm, aw), lambda i: (i, 0)),
        out_shape=jax.ShapeDtypeStruct((t, aw), BF16),
        scratch_shapes=[pltpu.VMEM((tm, aw), F32), pltpu.VMEM((tm, aw), BF16)],
        compiler_params=_params(("parallel",)),
        name="inproj_sgu",
    )(xn, w_u, w_v, g_sgu, w_spatial, b_full)


def _matmul_body(x_ref, w_ref, o_ref):
    o_ref[...] = _dot(x_ref[...], w_ref[...])


def _inproj_zxd(xn, w):
    t, d = xn.shape
    n = w.shape[1]
    tm = PROJ_ROW_TILE
    assert t % tm == 0 and n % LANES == 0
    return pl.pallas_call(
        _matmul_body,
        grid=(t // tm,),
        in_specs=[pl.BlockSpec((tm, d), lambda i: (i, 0)), pl.BlockSpec((d, n), lambda i: (0, 0))],
        out_specs=pl.BlockSpec((tm, n), lambda i: (i, 0)),
        out_shape=jax.ShapeDtypeStruct((t, n), F32),
        compiler_params=_params(("parallel",)),
        name="inproj_zxd",
    )(xn, w)


Z_OFF = 0
XBC_OFF = B_WIDTH
DT_OFF = B_WIDTH + B_CONV_DIM
ZXD_WIDTH = DT_OFF + LANES
CONV_HALO = SUBLANES


def _ssd_body(zxd_ref, convw_ref, convb_ref, dtb_ref, alog_ref, dskip_ref, norm_ref, o_ref,
              ext_ref, state_ref, y_ref):
    cl = B_CHUNK
    c = pl.program_id(1)

    @pl.when(c == 0)
    def _():
        ext_ref[0:CONV_HALO, :] = jnp.zeros((CONV_HALO, B_CONV_DIM), F32)
        state_ref[...] = jnp.zeros_like(state_ref)

    ext_ref[CONV_HALO:CONV_HALO + cl, :] = zxd_ref[:, XBC_OFF:XBC_OFF + B_CONV_DIM]
    conv = convb_ref[...]
    for k in range(B_CONV):
        start = CONV_HALO - (B_CONV - 1) + k
        conv = conv + convw_ref[k:k + 1, :] * ext_ref[start:start + cl, :]
    ext_ref[0:CONV_HALO, :] = ext_ref[cl:cl + CONV_HALO, :]
    xbc = _silu(conv)
    xs = xbc[:, :B_WIDTH]

    dt = jax.nn.softplus(zxd_ref[:, DT_OFF:DT_OFF + LANES] + dtb_ref[...])
    da = dt * (-jnp.exp(alog_ref[...]))
    rows = lax.broadcasted_iota(jnp.int32, (cl, cl), 0)
    cols = lax.broadcasted_iota(jnp.int32, (cl, cl), 1)
    causal = rows >= cols
    tril = jnp.where(causal, 1.0, 0.0).astype(BF16)
    acs = sum(_dot(tril, part) for part in _split3(da))
    acs_t = acs.T

    e_rows = lax.broadcasted_iota(jnp.int32, (LANES, B_WIDTH), 0)
    e_cols = lax.broadcasted_iota(jnp.int32, (LANES, B_WIDTH), 1)
    expand = jnp.where(e_cols // B_HEAD_DIM == e_rows, 1.0, 0.0).astype(BF16)
    acs_e = sum(_dot(part, expand) for part in _split3(acs))
    dt_e = sum(_dot(part, expand) for part in _split3(dt))

    x = xs * dt_e
    x_bf = x.astype(BF16)
    acs_last = acs_e[cl - 1:cl, :]
    decay_from_start = jnp.exp(acs_e)
    x_to_end = (x * jnp.exp(acs_last - acs_e)).astype(BF16)
    chunk_decay = jnp.exp(acs_last)

    lane = lax.broadcasted_iota(jnp.int32, (cl, LANES), 1)
    first_half = lane < B_HEAD_DIM
    heads_per_group = B_HEADS // B_GROUPS
    for g in range(B_GROUPS):
        gl = slice(g * B_GROUP_WIDTH, (g + 1) * B_GROUP_WIDTH)
        b_off = B_WIDTH + g * B_STATE
        c_off = B_WIDTH + B_GROUPS * B_STATE + g * B_STATE
        bc_t = xbc[:, b_off:b_off + B_STATE].T.astype(BF16)
        cc = xbc[:, c_off:c_off + B_STATE].astype(BF16)
        cb = _dot(cc, bc_t)
        state = state_ref[g]
        y_off = _dot(cc, state.astype(BF16)) * decay_from_start[:, gl]
        state_ref[g] = state * chunk_decay[:, gl] + _dot(bc_t, x_to_end[:, gl])
        for pair in range(heads_per_group // 2):
            head_a = g * heads_per_group + 2 * pair
            pl_off = head_a * B_HEAD_DIM
            x_pair = x_bf[:, pl_off:pl_off + LANES]
            y_pair = y_off[:, pair * LANES:(pair + 1) * LANES]
            for head, keep in ((head_a, first_half), (head_a + 1, ~first_half)):
                diff = acs[:, head:head + 1] - acs_t[head:head + 1, :]
                m = (cb * jnp.exp(jnp.where(causal, diff, -jnp.inf))).astype(BF16)
                y_pair = y_pair + _dot(m, jnp.where(keep, x_pair, jnp.zeros_like(x_pair)))
            y_ref[:, pl_off:pl_off + LANES] = y_pair

    y = (y_ref[...] + dskip_ref[...] * xs) * _silu(zxd_ref[:, Z_OFF:Z_OFF + B_WIDTH])
    for g in range(B_GROUPS):
        gl = slice(g * B_GROUP_WIDTH, (g + 1) * B_GROUP_WIDTH)
        yg = y[:, gl]
        yg = yg * lax.rsqrt(jnp.mean(yg * yg, axis=-1, keepdims=True) + EPS)
        o_ref[:, gl] = (yg * norm_ref[:, gl]).astype(BF16)


def _ssd(zxd, batch, conv_w, conv_b, dt_bias, a_log, d_skip, ssm_norm):
    t = zxd.shape[0]
    seq = t // batch
    assert seq % B_CHUNK == 0 and zxd.shape[1] == ZXD_WIDTH
    nc = seq // B_CHUNK
    const = lambda b, c: (0, 0)
    return pl.pallas_call(
        _ssd_body,
        grid=(batch, nc),
        in_specs=[
            pl.BlockSpec((B_CHUNK, ZXD_WIDTH), lambda b, c: (b * nc + c, 0)),
            pl.BlockSpec((B_CONV, B_CONV_DIM), const),
            pl.BlockSpec((1, B_CONV_DIM), const),
            pl.BlockSpec((1, LANES), const),
            pl.BlockSpec((1, LANES), const),
            pl.BlockSpec((1, B_WIDTH), const),
            pl.BlockSpec((1, B_WIDTH), const),
        ],
        out_specs=pl.BlockSpec((B_CHUNK, B_WIDTH), lambda b, c: (b * nc + c, 0)),
        out_shape=jax.ShapeDtypeStruct((t, B_WIDTH), BF16),
        scratch_shapes=[
            pltpu.VMEM((CONV_HALO + B_CHUNK, B_CONV_DIM), F32),
            pltpu.VMEM((B_GROUPS, B_STATE, B_GROUP_WIDTH), F32),
            pltpu.VMEM((B_CHUNK, B_WIDTH), F32),
        ],
        compiler_params=_params(("parallel", "arbitrary")),
        name="ssd",
    )(zxd, conv_w, conv_b, dt_bias, a_log, d_skip, ssm_norm)


def _outproj_body(x_ref, ya_ref, yb_ref, wa_ref, wb_ref, g_ref, o_ref):
    h = _dot(ya_ref[...], wa_ref[...]) + _dot(yb_ref[...], wb_ref[...])
    o_ref[...] = x_ref[...] + _rmsnorm(h, g_ref[...])


def _outproj(x, y_a, y_b, w_a, w_b, g_post):
    t, d = x.shape
    ka, kb = y_a.shape[1], y_b.shape[1]
    tm = PROJ_ROW_TILE
    assert t % tm == 0
    const = lambda i: (0, 0)
    return pl.pallas_call(
        _outproj_body,
        grid=(t // tm,),
        in_specs=[
            pl.BlockSpec((tm, d), lambda i: (i, 0)),
            pl.BlockSpec((tm, ka), lambda i: (i, 0)),
            pl.BlockSpec((tm, kb), lambda i: (i, 0)),
            pl.BlockSpec((ka, d), const),
            pl.BlockSpec((kb, d), const),
            pl.BlockSpec((1, d), const),
        ],
        out_specs=pl.BlockSpec((tm, d), lambda i: (i, 0)),
        out_shape=jax.ShapeDtypeStruct((t, d), F32),
        compiler_params=_params(("parallel",)),
        name="outproj",
    )(x, y_a, y_b, w_a, w_b, g_post)


def _pad_cols(w, n):
    return jnp.pad(w, ((0, 0), (0, n - w.shape[1])))


def _ffn_weights(w_gate, w_up, w_down):
    f = w_gate.shape[1]
    fp = pl.cdiv(f, FFN_FF_TILE) * FFN_FF_TILE
    wg = _pad_cols(w_gate, fp).astype(BF16)
    wu = _pad_cols(w_up, fp).astype(BF16)
    wd = jnp.pad(w_down, ((0, fp - f), (0, 0))).astype(BF16)
    return wg, wu, wd


def _row(v):
    return v.reshape(1, -1).astype(F32)


def kernel(x, ffn1_norm_pre, ffn1_norm_post, ffn1_w_gate, ffn1_w_up, ffn1_w_down, mix_norm_pre, mix_norm_post, w_in, sgu_norm, w_spatial, b_spatial, conv_w, conv_b, dt_bias, a_log, d_skip, ssm_norm, w_out, ffn2_norm_pre, ffn2_norm_post, ffn2_w_gate, ffn2_w_up, ffn2_w_down):
    batch, seq, d = x.shape
    depth = ffn1_norm_pre.shape[0]
    a_width = sgu_norm.shape[1]
    xf = x.reshape(batch * seq, d)
    for l in range(depth):
        xf, xn = _ffn(xf, _row(ffn1_norm_pre[l]), _row(ffn1_norm_post[l]), _row(mix_norm_pre[l]),
                      *_ffn_weights(ffn1_w_gate[l], ffn1_w_up[l], ffn1_w_down[l]))

        w = w_in[l]
        w_u = w[:, :a_width].astype(BF16)
        w_v = w[:, a_width:2 * a_width].astype(BF16)
        w_zxd = _pad_cols(w[:, 2 * a_width:], ZXD_WIDTH).astype(BF16)
        b_full = jnp.broadcast_to(b_spatial[l][:, :, None], b_spatial[l].shape + (A_HEAD_DIM,))
        y_a = _inproj_sgu(xn, w_u, w_v, _row(sgu_norm[l]), w_spatial[l], b_full)

        zxd = _inproj_zxd(xn, w_zxd)
        pad_heads = lambda v: jnp.pad(v.astype(F32), (0, LANES - B_HEADS)).reshape(1, LANES)
        y_b = _ssd(zxd, batch, conv_w[l], _row(conv_b[l]), pad_heads(dt_bias[l]), pad_heads(a_log[l]),
                   _row(jnp.repeat(d_skip[l], B_HEAD_DIM)), _row(ssm_norm[l]))

        wo = w_out[l].astype(BF16)
        xf = _outproj(xf, y_a, y_b, wo[:a_width], wo[a_width:], _row(mix_norm_post[l]))

        xf = _ffn(xf, _row(ffn2_norm_pre[l]), _row(ffn2_norm_post[l]), None,
                  *_ffn_weights(ffn2_w_gate[l], ffn2_w_up[l], ffn2_w_down[l]))
    return xf.reshape(batch, seq, d)
```

```python
import functools

import jax
import jax.numpy as jnp
from jax import lax
from jax.experimental import pallas as pl
from jax.experimental.pallas import tpu as pltpu

F32 = jnp.float32
BF16 = jnp.bfloat16

EPS = 1e-6
HALF_STEP = 0.5
SQRT_HALF = 0.7071067811865476

A_HEAD_DIM = 128
A_CHUNK = 128
B_HEAD_DIM = 64
B_HEADS = 16
B_GROUPS = 2
B_STATE = 128
B_CONV = 4
B_CHUNK = 256
B_WIDTH = B_HEADS * B_HEAD_DIM
B_GROUP_WIDTH = B_WIDTH // B_GROUPS
B_CONV_DIM = B_WIDTH + 2 * B_GROUPS * B_STATE

LANES = 128
SUBLANES = 8
VMEM_LIMIT_BYTES = 56 * 1024 * 1024

FFN_ROW_TILE = 512
FFN_FF_TILE = 512
PROJ_ROW_TILE = 512


def _rmsnorm(x, g):
    return x * lax.rsqrt(jnp.mean(x * x, axis=-1, keepdims=True) + EPS) * g


def _gelu(x):
    return 0.5 * x * (1.0 + lax.erf(x * SQRT_HALF))


def _silu(x):
    return x * jax.nn.sigmoid(x)


def _dot(a, b):
    return jnp.dot(a, b, preferred_element_type=F32)


def _split3(v):
    hi = v.astype(BF16)
    r1 = v - hi.astype(F32)
    mid = r1.astype(BF16)
    lo = (r1 - mid.astype(F32)).astype(BF16)
    return hi, mid, lo


def _params(semantics):
    return pltpu.CompilerParams(dimension_semantics=semantics, vmem_limit_bytes=VMEM_LIMIT_BYTES)


def _ffn_body(*refs, emit_next, n_main):
    if emit_next:
        (x_ref, gpre_ref, gpost_ref, gnext_ref, wg_ref, wu_ref, wd_ref, wgt_ref, wut_ref, wdt_ref,
         o_ref, on_ref, xn_ref) = refs
    else:
        (x_ref, gpre_ref, gpost_ref, wg_ref, wu_ref, wd_ref, wgt_ref, wut_ref, wdt_ref,
         o_ref, xn_ref) = refs
    j = pl.program_id(1)

    def partial_out(g_ref, u_ref, d_ref):
        xn = xn_ref[...]
        a = (_silu(_dot(xn, g_ref[...])) * _dot(xn, u_ref[...])).astype(BF16)
        return _dot(a, d_ref[...])

    @pl.when(j == 0)
    def _():
        xn_ref[...] = _rmsnorm(x_ref[...], gpre_ref[...]).astype(BF16)
        o_ref[...] = partial_out(wg_ref, wu_ref, wd_ref)

    @pl.when(jnp.logical_and(j > 0, j < n_main))
    def _():
        o_ref[...] += partial_out(wg_ref, wu_ref, wd_ref)

    @pl.when(j == n_main)
    def _():
        h = o_ref[...] + partial_out(wgt_ref, wut_ref, wdt_ref)
        x_new = x_ref[...] + HALF_STEP * _rmsnorm(h, gpost_ref[...])
        o_ref[...] = x_new
        if emit_next:
            on_ref[...] = _rmsnorm(x_new, gnext_ref[...]).astype(BF16)


def _ffn(x, g_pre, g_post, g_next, weights):
    wg, wu, wd, wgt, wut, wdt = weights
    t, d = x.shape
    tm, tf = FFN_ROW_TILE, FFN_FF_TILE
    n_main = wg.shape[1] // tf
    ft = wgt.shape[1]
    assert t % tm == 0 and wg.shape[1] == n_main * tf and n_main >= 1 and ft % LANES == 0
    emit_next = g_next is not None
    row = pl.BlockSpec((tm, d), lambda i, j: (i, 0))
    const = lambda i, j: (0, 0)
    vec = pl.BlockSpec((1, d), const)
    main_col = lambda i, j: (0, jnp.minimum(j, n_main - 1))
    main_row = lambda i, j: (jnp.minimum(j, n_main - 1), 0)
    in_specs = [row, vec, vec] + ([vec] if emit_next else []) + [
        pl.BlockSpec((d, tf), main_col),
        pl.BlockSpec((d, tf), main_col),
        pl.BlockSpec((tf, d), main_row),
        pl.BlockSpec((d, ft), const),
        pl.BlockSpec((d, ft), const),
        pl.BlockSpec((ft, d), const),
    ]
    out_shape = [jax.ShapeDtypeStruct((t, d), F32)]
    out_specs = [row]
    if emit_next:
        out_shape.append(jax.ShapeDtypeStruct((t, d), BF16))
        out_specs.append(row)
    args = [x, g_pre, g_post] + ([g_next] if emit_next else []) + list(weights)
    outs = pl.pallas_call(
        functools.partial(_ffn_body, emit_next=emit_next, n_main=n_main),
        grid=(t // tm, n_main + 1),
        in_specs=in_specs,
        out_specs=out_specs,
        out_shape=out_shape,
        scratch_shapes=[pltpu.VMEM((tm, d), BF16)],
        compiler_params=_params(("parallel", "arbitrary")),
        name="ffn_next" if emit_next else "ffn",
    )(*args)
    return outs if emit_next else outs[0]


def _sgu_body(xn_ref, wu_ref, wv_ref, gs_ref, ws_ref, bs_ref, o_ref, u_ref, v_ref):
    tm = xn_ref.shape[0]
    heads = ws_ref.shape[0]
    xn = xn_ref[...]
    u_ref[...] = _gelu(_dot(xn, wu_ref[...]))
    v_ref[...] = _rmsnorm(_gelu(_dot(xn, wv_ref[...])), gs_ref[...]).astype(BF16)
    rows = lax.broadcasted_iota(jnp.int32, (A_CHUNK, A_CHUNK), 0)
    cols = lax.broadcasted_iota(jnp.int32, (A_CHUNK, A_CHUNK), 1)
    causal = rows >= cols
    for h in range(heads):
        w = jnp.where(causal, ws_ref[h], 0.0).astype(BF16)
        bias = bs_ref[h]
        lanes = pl.ds(h * A_HEAD_DIM, A_HEAD_DIM)
        for c in range(tm // A_CHUNK):
            rws = pl.ds(c * A_CHUNK, A_CHUNK)
            mixed = _dot(w, v_ref[rws, lanes]) + bias
            o_ref[rws, lanes] = (u_ref[rws, lanes] * mixed).astype(BF16)


def _inproj_sgu(xn, w_u, w_v, g_sgu, w_spatial, b_full):
    t, d = xn.shape
    aw = w_u.shape[1]
    heads = w_spatial.shape[0]
    tm = PROJ_ROW_TILE
    assert t % tm == 0 and tm % A_CHUNK == 0 and aw == heads * A_HEAD_DIM
    const2 = lambda i: (0, 0)
    const3 = lambda i: (0, 0, 0)
    return pl.pallas_call(
        _sgu_body,
        grid=(t // tm,),
        in_specs=[
            pl.BlockSpec((tm, d), lambda i: (i, 0)),
            pl.BlockSpec((d, aw), const2),
            pl.BlockSpec((d, aw), const2),
            pl.BlockSpec((1, aw), const2),
            pl.BlockSpec((heads, A_CHUNK, A_CHUNK), const3),
            pl.BlockSpec((heads, A_CHUNK, A_HEAD_DIM), const3),
        ],
        out_specs=pl.BlockSpec((tm, aw), lambda i: (i, 0)),
        out_shape=jax.ShapeDtypeStruct((t, aw), BF16),
        scratch_shapes=[pltpu.VMEM((tm, aw), F32), pltpu.VMEM((tm, aw), BF16)],
        compiler_params=_params(("parallel",)),
        name="inproj_sgu",
    )(xn, w_u, w_v, g_sgu, w_spatial, b_full)


def _matmul_body(x_ref, w_ref, o_ref):
    o_ref[...] = _dot(x_ref[...], w_ref[...])


def _inproj_zxd(xn, w):
    t, d = xn.shape
    n = w.shape[1]
    tm = PROJ_ROW_TILE
    assert t % tm == 0 and n % LANES == 0
    return pl.pallas_call(
        _matmul_body,
        grid=(t // tm,),
        in_specs=[pl.BlockSpec((tm, d), lambda i: (i, 0)), pl.BlockSpec((d, n), lambda i: (0, 0))],
        out_specs=pl.BlockSpec((tm, n), lambda i: (i, 0)),
        out_shape=jax.ShapeDtypeStruct((t, n), F32),
        compiler_params=_params(("parallel",)),
        name="inproj_zxd",
    )(xn, w)


Z_OFF = 0
XBC_OFF = B_WIDTH
DT_OFF = B_WIDTH + B_CONV_DIM
ZXD_WIDTH = DT_OFF + LANES
CONV_HALO = SUBLANES


def _ssd_body(zxd_ref, convw_ref, convb_ref, dtb_ref, alog_ref, dskip_ref, norm_ref, o_ref,
              ext_ref, state_ref, y_ref):
    cl = B_CHUNK
    c = pl.program_id(1)

    @pl.when(c == 0)
    def _():
        ext_ref[0:CONV_HALO, :] = jnp.zeros((CONV_HALO, B_CONV_DIM), F32)
        state_ref[...] = jnp.zeros_like(state_ref)

    ext_ref[CONV_HALO:CONV_HALO + cl, :] = zxd_ref[:, XBC_OFF:XBC_OFF + B_CONV_DIM]
    conv = convb_ref[...]
    for k in range(B_CONV):
        start = CONV_HALO - (B_CONV - 1) + k
        conv = conv + convw_ref[k:k + 1, :] * ext_ref[start:start + cl, :]
    ext_ref[0:CONV_HALO, :] = ext_ref[cl:cl + CONV_HALO, :]
    xbc = _silu(conv)
    xs = xbc[:, :B_WIDTH]

    dt = jax.nn.softplus(zxd_ref[:, DT_OFF:DT_OFF + LANES] + dtb_ref[...])
    da = dt * (-jnp.exp(alog_ref[...]))
    rows = lax.broadcasted_iota(jnp.int32, (cl, cl), 0)
    cols = lax.broadcasted_iota(jnp.int32, (cl, cl), 1)
    causal = rows >= cols
    tril = jnp.where(causal, 1.0, 0.0).astype(BF16)
    acs = sum(_dot(tril, part) for part in _split3(da))
    acs_t = acs.T

    e_rows = lax.broadcasted_iota(jnp.int32, (LANES, B_WIDTH), 0)
    e_cols = lax.broadcasted_iota(jnp.int32, (LANES, B_WIDTH), 1)
    expand = jnp.where(e_cols // B_HEAD_DIM == e_rows, 1.0, 0.0).astype(BF16)
    acs_e = sum(_dot(part, expand) for part in _split3(acs))
    dt_e = sum(_dot(part, expand) for part in _split3(dt))

    x = xs * dt_e
    x_bf = x.astype(BF16)
    acs_last = acs_e[cl - 1:cl, :]
    decay_from_start = jnp.exp(acs_e)
    x_to_end = (x * jnp.exp(acs_last - acs_e)).astype(BF16)
    chunk_decay = jnp.exp(acs_last)

    lane = lax.broadcasted_iota(jnp.int32, (cl, LANES), 1)
    first_half = lane < B_HEAD_DIM
    heads_per_group = B_HEADS // B_GROUPS
    for g in range(B_GROUPS):
        gl = slice(g * B_GROUP_WIDTH, (g + 1) * B_GROUP_WIDTH)
        b_off = B_WIDTH + g * B_STATE
        c_off = B_WIDTH + B_GROUPS * B_STATE + g * B_STATE
        bc_t = xbc[:, b_off:b_off + B_STATE].T.astype(BF16)
        cc = xbc[:, c_off:c_off + B_STATE].astype(BF16)
        cb = _dot(cc, bc_t)
        state = state_ref[g]
        y_off = _dot(cc, state.astype(BF16)) * decay_from_start[:, gl]
        state_ref[g] = state * chunk_decay[:, gl] + _dot(bc_t, x_to_end[:, gl])
        for pair in range(heads_per_group // 2):
            head_a = g * heads_per_group + 2 * pair
            pl_off = head_a * B_HEAD_DIM
            x_pair = x_bf[:, pl_off:pl_off + LANES]
            y_pair = y_off[:, pair * LANES:(pair + 1) * LANES]
            for head, keep in ((head_a, first_half), (head_a + 1, ~first_half)):
                diff = acs[:, head:head + 1] - acs_t[head:head + 1, :]
                m = (cb * jnp.exp(jnp.where(causal, diff, -jnp.inf))).astype(BF16)
                y_pair = y_pair + _dot(m, jnp.where(keep, x_pair, jnp.zeros_like(x_pair)))
            y_ref[:, pl_off:pl_off + LANES] = y_pair

    y = (y_ref[...] + dskip_ref[...] * xs) * _silu(zxd_ref[:, Z_OFF:Z_OFF + B_WIDTH])
    for g in range(B_GROUPS):
        gl = slice(g * B_GROUP_WIDTH, (g + 1) * B_GROUP_WIDTH)
        yg = y[:, gl]
        yg = yg * lax.rsqrt(jnp.mean(yg * yg, axis=-1, keepdims=True) + EPS)
        o_ref[:, gl] = (yg * norm_ref[:, gl]).astype(BF16)


def _ssd(zxd, batch, conv_w, conv_b, dt_bias, a_log, d_skip, ssm_norm):
    t = zxd.shape[0]
    seq = t // batch
    assert seq % B_CHUNK == 0 and zxd.shape[1] == ZXD_WIDTH
    nc = seq // B_CHUNK
    const = lambda b, c: (0, 0)
    return pl.pallas_call(
        _ssd_body,
        grid=(batch, nc),
        in_specs=[
            pl.BlockSpec((B_CHUNK, ZXD_WIDTH), lambda b, c: (b * nc + c, 0)),
            pl.BlockSpec((B_CONV, B_CONV_DIM), const),
            pl.BlockSpec((1, B_CONV_DIM), const),
            pl.BlockSpec((1, LANES), const),
            pl.BlockSpec((1, LANES), const),
            pl.BlockSpec((1, B_WIDTH), const),
            pl.BlockSpec((1, B_WIDTH), const),
        ],
        out_specs=pl.BlockSpec((B_CHUNK, B_WIDTH), lambda b, c: (b * nc + c, 0)),
        out_shape=jax.ShapeDtypeStruct((t, B_WIDTH), BF16),
        scratch_shapes=[
            pltpu.VMEM((CONV_HALO + B_CHUNK, B_CONV_DIM), F32),
            pltpu.VMEM((B_GROUPS, B_STATE, B_GROUP_WIDTH), F32),
            pltpu.VMEM((B_CHUNK, B_WIDTH), F32),
        ],
        compiler_params=_params(("parallel", "arbitrary")),
        name="ssd",
    )(zxd, conv_w, conv_b, dt_bias, a_log, d_skip, ssm_norm)


def _outproj_body(x_ref, ya_ref, yb_ref, wa_ref, wb_ref, g_ref, o_ref):
    h = _dot(ya_ref[...], wa_ref[...]) + _dot(yb_ref[...], wb_ref[...])
    o_ref[...] = x_ref[...] + _rmsnorm(h, g_ref[...])


def _outproj(x, y_a, y_b, w_a, w_b, g_post):
    t, d = x.shape
    ka, kb = y_a.shape[1], y_b.shape[1]
    tm = PROJ_ROW_TILE
    assert t % tm == 0
    const = lambda i: (0, 0)
    return pl.pallas_call(
        _outproj_body,
        grid=(t // tm,),
        in_specs=[
            pl.BlockSpec((tm, d), lambda i: (i, 0)),
            pl.BlockSpec((tm, ka), lambda i: (i, 0)),
            pl.BlockSpec((tm, kb), lambda i: (i, 0)),
            pl.BlockSpec((ka, d), const),
            pl.BlockSpec((kb, d), const),
            pl.BlockSpec((1, d), const),
        ],
        out_specs=pl.BlockSpec((tm, d), lambda i: (i, 0)),
        out_shape=jax.ShapeDtypeStruct((t, d), F32),
        compiler_params=_params(("parallel",)),
        name="outproj",
    )(x, y_a, y_b, w_a, w_b, g_post)


def _pad_cols(w, n):
    return jnp.pad(w, ((0, 0), (0, n - w.shape[1])))


def _ffn_weights(w_gate, w_up, w_down):
    f = w_gate.shape[1]
    fm = ((f - 1) // FFN_FF_TILE) * FFN_FF_TILE
    cast = lambda w: w.astype(BF16)
    return (cast(w_gate[:, :fm]), cast(w_up[:, :fm]), cast(w_down[:fm]),
            cast(w_gate[:, fm:]), cast(w_up[:, fm:]), cast(w_down[fm:]))


def _row(v):
    return v.reshape(1, -1).astype(F32)


def kernel(x, ffn1_norm_pre, ffn1_norm_post, ffn1_w_gate, ffn1_w_up, ffn1_w_down, mix_norm_pre, mix_norm_post, w_in, sgu_norm, w_spatial, b_spatial, conv_w, conv_b, dt_bias, a_log, d_skip, ssm_norm, w_out, ffn2_norm_pre, ffn2_norm_post, ffn2_w_gate, ffn2_w_up, ffn2_w_down):
    batch, seq, d = x.shape
    depth = ffn1_norm_pre.shape[0]
    a_width = sgu_norm.shape[1]
    xf = x.reshape(batch * seq, d)
    for l in range(depth):
        xf, xn = _ffn(xf, _row(ffn1_norm_pre[l]), _row(ffn1_norm_post[l]), _row(mix_norm_pre[l]),
                      _ffn_weights(ffn1_w_gate[l], ffn1_w_up[l], ffn1_w_down[l]))

        w = w_in[l]
        w_u = w[:, :a_width].astype(BF16)
        w_v = w[:, a_width:2 * a_width].astype(BF16)
        w_zxd = _pad_cols(w[:, 2 * a_width:], ZXD_WIDTH).astype(BF16)
        b_full = jnp.broadcast_to(b_spatial[l][:, :, None], b_spatial[l].shape + (A_HEAD_DIM,))
        y_a = _inproj_sgu(xn, w_u, w_v, _row(sgu_norm[l]), w_spatial[l], b_full)

        zxd = _inproj_zxd(xn, w_zxd)
        pad_heads = lambda v: jnp.pad(v.astype(F32), (0, LANES - B_HEADS)).reshape(1, LANES)
        y_b = _ssd(zxd, batch, conv_w[l], _row(conv_b[l]), pad_heads(dt_bias[l]), pad_heads(a_log[l]),
                   _row(jnp.repeat(d_skip[l], B_HEAD_DIM)), _row(ssm_norm[l]))

        wo = w_out[l].astype(BF16)
        xf = _outproj(xf, y_a, y_b, wo[:a_width], wo[a_width:], _row(mix_norm_post[l]))

        xf = _ffn(xf, _row(ffn2_norm_pre[l]), _row(ffn2_norm_post[l]), None,
                  _ffn_weights(ffn2_w_gate[l], ffn2_w_up[l], ffn2_w_down[l]))
    return xf.reshape(batch, seq, d)
```

```python
import functools

import jax
import jax.numpy as jnp
from jax import lax
from jax.experimental import pallas as pl
from jax.experimental.pallas import tpu as pltpu

F32 = jnp.float32
BF16 = jnp.bfloat16

EPS = 1e-6
HALF_STEP = 0.5
SQRT_HALF = 0.7071067811865476

A_HEAD_DIM = 128
A_CHUNK = 128
B_HEAD_DIM = 64
B_HEADS = 16
B_GROUPS = 2
B_STATE = 128
B_CONV = 4
B_CHUNK = 256
B_WIDTH = B_HEADS * B_HEAD_DIM
B_GROUP_WIDTH = B_WIDTH // B_GROUPS
B_CONV_DIM = B_WIDTH + 2 * B_GROUPS * B_STATE

LANES = 128
SUBLANES = 8
VMEM_LIMIT_BYTES = 56 * 1024 * 1024

FFN_ROW_TILE = 512
FFN_FF_TILE = 1024
FFN_FF_SUB = 512
PROJ_ROW_TILE = 512


def _rmsnorm(x, g):
    return x * lax.rsqrt(jnp.mean(x * x, axis=-1, keepdims=True) + EPS) * g


def _gelu(x):
    return 0.5 * x * (1.0 + lax.erf(x * SQRT_HALF))


def _silu(x):
    return x * jax.nn.sigmoid(x)


def _dot(a, b):
    return jnp.dot(a, b, preferred_element_type=F32)


def _split3(v):
    hi = v.astype(BF16)
    r1 = v - hi.astype(F32)
    mid = r1.astype(BF16)
    lo = (r1 - mid.astype(F32)).astype(BF16)
    return hi, mid, lo


def _params(semantics):
    return pltpu.CompilerParams(dimension_semantics=semantics, vmem_limit_bytes=VMEM_LIMIT_BYTES)


def _ffn_body(*refs, emit_next, n_steps, tail_width):
    if emit_next:
        x_ref, gpre_ref, gpost_ref, gnext_ref, wg_ref, wu_ref, wd_ref, o_ref, on_ref, xn_ref = refs
    else:
        x_ref, gpre_ref, gpost_ref, wg_ref, wu_ref, wd_ref, o_ref, xn_ref = refs
    j = pl.program_id(1)
    tf = wg_ref.shape[1]

    def partial_out(width):
        xn = xn_ref[...]
        acts = []
        for start in range(0, width, FFN_FF_SUB):
            cols = slice(start, min(start + FFN_FF_SUB, width))
            acts.append((_silu(_dot(xn, wg_ref[:, cols])) * _dot(xn, wu_ref[:, cols])).astype(BF16))
        return _dot(jnp.concatenate(acts, axis=1), wd_ref[:width, :])

    @pl.when(j == 0)
    def _():
        xn_ref[...] = _rmsnorm(x_ref[...], gpre_ref[...]).astype(BF16)
        o_ref[...] = partial_out(tf)

    @pl.when(jnp.logical_and(j > 0, j < n_steps - 1))
    def _():
        o_ref[...] += partial_out(tf)

    @pl.when(j == n_steps - 1)
    def _():
        h = o_ref[...] + partial_out(tail_width)
        x_new = x_ref[...] + HALF_STEP * _rmsnorm(h, gpost_ref[...])
        o_ref[...] = x_new
        if emit_next:
            on_ref[...] = _rmsnorm(x_new, gnext_ref[...]).astype(BF16)


def _ffn(x, g_pre, g_post, g_next, wg, wu, wd):
    t, d = x.shape
    f = wg.shape[1]
    tm, tf = FFN_ROW_TILE, FFN_FF_TILE
    n_steps = pl.cdiv(f, tf)
    tail_width = f - (n_steps - 1) * tf
    assert t % tm == 0 and n_steps >= 2 and tail_width % LANES == 0
    emit_next = g_next is not None
    row = pl.BlockSpec((tm, d), lambda i, j: (i, 0))
    vec = pl.BlockSpec((1, d), lambda i, j: (0, 0))
    in_specs = [row, vec, vec] + ([vec] if emit_next else []) + [
        pl.BlockSpec((d, tf), lambda i, j: (0, j)),
        pl.BlockSpec((d, tf), lambda i, j: (0, j)),
        pl.BlockSpec((tf, d), lambda i, j: (j, 0)),
    ]
    out_shape = [jax.ShapeDtypeStruct((t, d), F32)]
    out_specs = [row]
    if emit_next:
        out_shape.append(jax.ShapeDtypeStruct((t, d), BF16))
        out_specs.append(row)
    args = [x, g_pre, g_post] + ([g_next] if emit_next else []) + [wg, wu, wd]
    outs = pl.pallas_call(
        functools.partial(_ffn_body, emit_next=emit_next, n_steps=n_steps, tail_width=tail_width),
        grid=(t // tm, n_steps),
        in_specs=in_specs,
        out_specs=out_specs,
        out_shape=out_shape,
        scratch_shapes=[pltpu.VMEM((tm, d), BF16)],
        compiler_params=_params(("parallel", "arbitrary")),
        name="ffn_next" if emit_next else "ffn",
    )(*args)
    return outs if emit_next else outs[0]


def _sgu_body(xn_ref, wu_ref, wv_ref, gs_ref, ws_ref, bs_ref, o_ref, u_ref, v_ref):
    tm = xn_ref.shape[0]
    heads = ws_ref.shape[0]
    xn = xn_ref[...]
    u_ref[...] = _gelu(_dot(xn, wu_ref[...]))
    v_ref[...] = _rmsnorm(_gelu(_dot(xn, wv_ref[...])), gs_ref[...]).astype(BF16)
    rows = lax.broadcasted_iota(jnp.int32, (A_CHUNK, A_CHUNK), 0)
    cols = lax.broadcasted_iota(jnp.int32, (A_CHUNK, A_CHUNK), 1)
    causal = rows >= cols
    for h in range(heads):
        w = jnp.where(causal, ws_ref[h], 0.0).astype(BF16)
        bias = bs_ref[h]
        lanes = pl.ds(h * A_HEAD_DIM, A_HEAD_DIM)
        for c in range(tm // A_CHUNK):
            rws = pl.ds(c * A_CHUNK, A_CHUNK)
            mixed = _dot(w, v_ref[rws, lanes]) + bias
            o_ref[rws, lanes] = (u_ref[rws, lanes] * mixed).astype(BF16)


def _inproj_sgu(xn, w_u, w_v, g_sgu, w_spatial, b_full):
    t, d = xn.shape
    aw = w_u.shape[1]
    heads = w_spatial.shape[0]
    tm = PROJ_ROW_TILE
    assert t % tm == 0 and tm % A_CHUNK == 0 and aw == heads * A_HEAD_DIM
    const2 = lambda i: (0, 0)
    const3 = lambda i: (0, 0, 0)
    return pl.pallas_call(
        _sgu_body,
        grid=(t // tm,),
        in_specs=[
            pl.BlockSpec((tm, d), lambda i: (i, 0)),
            pl.BlockSpec((d, aw), const2),
            pl.BlockSpec((d, aw), const2),
            pl.BlockSpec((1, aw), const2),
            pl.BlockSpec((heads, A_CHUNK, A_CHUNK), const3),
            pl.BlockSpec((heads, A_CHUNK, A_HEAD_DIM), const3),
        ],
        out_specs=pl.BlockSpec((tm, aw), lambda i: (i, 0)),
        out_shape=jax.ShapeDtypeStruct((t, aw), BF16),
        scratch_shapes=[pltpu.VMEM((tm, aw), F32), pltpu.VMEM((tm, aw), BF16)],
        compiler_params=_params(("parallel",)),
        name="inproj_sgu",
    )(xn, w_u, w_v, g_sgu, w_spatial, b_full)


def _matmul_body(x_ref, w_ref, o_ref):
    o_ref[...] = _dot(x_ref[...], w_ref[...])


def _inproj_zxd(xn, w):
    t, d = xn.shape
    n = w.shape[1]
    tm = PROJ_ROW_TILE
    assert t % tm == 0 and n % LANES == 0
    return pl.pallas_call(
        _matmul_body,
        grid=(t // tm,),
        in_specs=[pl.BlockSpec((tm, d), lambda i: (i, 0)), pl.BlockSpec((d, n), lambda i: (0, 0))],
        out_specs=pl.BlockSpec((tm, n), lambda i: (i, 0)),
        out_shape=jax.ShapeDtypeStruct((t, n), F32),
        compiler_params=_params(("parallel",)),
        name="inproj_zxd",
    )(xn, w)


Z_OFF = 0
XBC_OFF = B_WIDTH
DT_OFF = B_WIDTH + B_CONV_DIM
ZXD_WIDTH = DT_OFF + LANES
CONV_HALO = SUBLANES


def _ssd_body(zxd_ref, convw_ref, convb_ref, dtb_ref, alog_ref, dskip_ref, norm_ref, o_ref,
              ext_ref, state_ref, y_ref):
    cl = B_CHUNK
    c = pl.program_id(1)

    @pl.when(c == 0)
    def _():
        ext_ref[0:CONV_HALO, :] = jnp.zeros((CONV_HALO, B_CONV_DIM), F32)
        state_ref[...] = jnp.zeros_like(state_ref)

    ext_ref[CONV_HALO:CONV_HALO + cl, :] = zxd_ref[:, XBC_OFF:XBC_OFF + B_CONV_DIM]
    conv = convb_ref[...]
    for k in range(B_CONV):
        start = CONV_HALO - (B_CONV - 1) + k
        conv = conv + convw_ref[k:k + 1, :] * ext_ref[start:start + cl, :]
    ext_ref[0:CONV_HALO, :] = ext_ref[cl:cl + CONV_HALO, :]
    xbc = _silu(conv)
    xs = xbc[:, :B_WIDTH]

    dt = jax.nn.softplus(zxd_ref[:, DT_OFF:DT_OFF + LANES] + dtb_ref[...])
    da = dt * (-jnp.exp(alog_ref[...]))
    rows = lax.broadcasted_iota(jnp.int32, (cl, cl), 0)
    cols = lax.broadcasted_iota(jnp.int32, (cl, cl), 1)
    causal = rows >= cols
    tril = jnp.where(causal, 1.0, 0.0).astype(BF16)
    acs = sum(_dot(tril, part) for part in _split3(da))
    acs_t = acs.T

    e_rows = lax.broadcasted_iota(jnp.int32, (LANES, B_WIDTH), 0)
    e_cols = lax.broadcasted_iota(jnp.int32, (LANES, B_WIDTH), 1)
    expand = jnp.where(e_cols // B_HEAD_DIM == e_rows, 1.0, 0.0).astype(BF16)
    acs_e = sum(_dot(part, expand) for part in _split3(acs))
    dt_e = sum(_dot(part, expand) for part in _split3(dt))

    x = xs * dt_e
    x_bf = x.astype(BF16)
    acs_last = acs_e[cl - 1:cl, :]
    decay_from_start = jnp.exp(acs_e)
    x_to_end = (x * jnp.exp(acs_last - acs_e)).astype(BF16)
    chunk_decay = jnp.exp(acs_last)

    lane = lax.broadcasted_iota(jnp.int32, (cl, LANES), 1)
    first_half = lane < B_HEAD_DIM
    heads_per_group = B_HEADS // B_GROUPS
    for g in range(B_GROUPS):
        gl = slice(g * B_GROUP_WIDTH, (g + 1) * B_GROUP_WIDTH)
        b_off = B_WIDTH + g * B_STATE
        c_off = B_WIDTH + B_GROUPS * B_STATE + g * B_STATE
        bc_t = xbc[:, b_off:b_off + B_STATE].T.astype(BF16)
        cc = xbc[:, c_off:c_off + B_STATE].astype(BF16)
        cb = _dot(cc, bc_t)
        state = state_ref[g]
        y_off = _dot(cc, state.astype(BF16)) * decay_from_start[:, gl]
        state_ref[g] = state * chunk_decay[:, gl] + _dot(bc_t, x_to_end[:, gl])
        for pair in range(heads_per_group // 2):
            head_a = g * heads_per_group + 2 * pair
            pl_off = head_a * B_HEAD_DIM
            x_pair = x_bf[:, pl_off:pl_off + LANES]
            y_pair = y_off[:, pair * LANES:(pair + 1) * LANES]
            for head, keep in ((head_a, first_half), (head_a + 1, ~first_half)):
                diff = acs[:, head:head + 1] - acs_t[head:head + 1, :]
                m = (cb * jnp.exp(jnp.where(causal, diff, -jnp.inf))).astype(BF16)
                y_pair = y_pair + _dot(m, jnp.where(keep, x_pair, jnp.zeros_like(x_pair)))
            y_ref[:, pl_off:pl_off + LANES] = y_pair

    y = (y_ref[...] + dskip_ref[...] * xs) * _silu(zxd_ref[:, Z_OFF:Z_OFF + B_WIDTH])
    for g in range(B_GROUPS):
        gl = slice(g * B_GROUP_WIDTH, (g + 1) * B_GROUP_WIDTH)
        yg = y[:, gl]
        yg = yg * lax.rsqrt(jnp.mean(yg * yg, axis=-1, keepdims=True) + EPS)
        o_ref[:, gl] = (yg * norm_ref[:, gl]).astype(BF16)


def _ssd(zxd, batch, conv_w, conv_b, dt_bias, a_log, d_skip, ssm_norm):
    t = zxd.shape[0]
    seq = t // batch
    assert seq % B_CHUNK == 0 and zxd.shape[1] == ZXD_WIDTH
    nc = seq // B_CHUNK
    const = lambda b, c: (0, 0)
    return pl.pallas_call(
        _ssd_body,
        grid=(batch, nc),
        in_specs=[
            pl.BlockSpec((B_CHUNK, ZXD_WIDTH), lambda b, c: (b * nc + c, 0)),
            pl.BlockSpec((B_CONV, B_CONV_DIM), const),
            pl.BlockSpec((1, B_CONV_DIM), const),
            pl.BlockSpec((1, LANES), const),
            pl.BlockSpec((1, LANES), const),
            pl.BlockSpec((1, B_WIDTH), const),
            pl.BlockSpec((1, B_WIDTH), const),
        ],
        out_specs=pl.BlockSpec((B_CHUNK, B_WIDTH), lambda b, c: (b * nc + c, 0)),
        out_shape=jax.ShapeDtypeStruct((t, B_WIDTH), BF16),
        scratch_shapes=[
            pltpu.VMEM((CONV_HALO + B_CHUNK, B_CONV_DIM), F32),
            pltpu.VMEM((B_GROUPS, B_STATE, B_GROUP_WIDTH), F32),
            pltpu.VMEM((B_CHUNK, B_WIDTH), F32),
        ],
        compiler_params=_params(("parallel", "arbitrary")),
        name="ssd",
    )(zxd, conv_w, conv_b, dt_bias, a_log, d_skip, ssm_norm)


def _outproj_body(x_ref, ya_ref, yb_ref, wa_ref, wb_ref, g_ref, o_ref):
    h = _dot(ya_ref[...], wa_ref[...]) + _dot(yb_ref[...], wb_ref[...])
    o_ref[...] = x_ref[...] + _rmsnorm(h, g_ref[...])


def _outproj(x, y_a, y_b, w_a, w_b, g_post):
    t, d = x.shape
    ka, kb = y_a.shape[1], y_b.shape[1]
    tm = PROJ_ROW_TILE
    assert t % tm == 0
    const = lambda i: (0, 0)
    return pl.pallas_call(
        _outproj_body,
        grid=(t // tm,),
        in_specs=[
            pl.BlockSpec((tm, d), lambda i: (i, 0)),
            pl.BlockSpec((tm, ka), lambda i: (i, 0)),
            pl.BlockSpec((tm, kb), lambda i: (i, 0)),
            pl.BlockSpec((ka, d), const),
            pl.BlockSpec((kb, d), const),
            pl.BlockSpec((1, d), const),
        ],
        out_specs=pl.BlockSpec((tm, d), lambda i: (i, 0)),
        out_shape=jax.ShapeDtypeStruct((t, d), F32),
        compiler_params=_params(("parallel",)),
        name="outproj",
    )(x, y_a, y_b, w_a, w_b, g_post)


def _pad_cols(w, n):
    return jnp.pad(w, ((0, 0), (0, n - w.shape[1])))


def _row(v):
    return v.reshape(1, -1).astype(F32)


def kernel(x, ffn1_norm_pre, ffn1_norm_post, ffn1_w_gate, ffn1_w_up, ffn1_w_down, mix_norm_pre, mix_norm_post, w_in, sgu_norm, w_spatial, b_spatial, conv_w, conv_b, dt_bias, a_log, d_skip, ssm_norm, w_out, ffn2_norm_pre, ffn2_norm_post, ffn2_w_gate, ffn2_w_up, ffn2_w_down):
    batch, seq, d = x.shape
    depth = ffn1_norm_pre.shape[0]
    a_width = sgu_norm.shape[1]
    xf = x.reshape(batch * seq, d)
    for l in range(depth):
        xf, xn = _ffn(xf, _row(ffn1_norm_pre[l]), _row(ffn1_norm_post[l]), _row(mix_norm_pre[l]),
                      ffn1_w_gate[l].astype(BF16), ffn1_w_up[l].astype(BF16), ffn1_w_down[l].astype(BF16))

        w = w_in[l]
        w_u = w[:, :a_width].astype(BF16)
        w_v = w[:, a_width:2 * a_width].astype(BF16)
        w_zxd = _pad_cols(w[:, 2 * a_width:], ZXD_WIDTH).astype(BF16)
        b_full = jnp.broadcast_to(b_spatial[l][:, :, None], b_spatial[l].shape + (A_HEAD_DIM,))
        y_a = _inproj_sgu(xn, w_u, w_v, _row(sgu_norm[l]), w_spatial[l], b_full)

        zxd = _inproj_zxd(xn, w_zxd)
        pad_heads = lambda v: jnp.pad(v.astype(F32), (0, LANES - B_HEADS)).reshape(1, LANES)
        y_b = _ssd(zxd, batch, conv_w[l], _row(conv_b[l]), pad_heads(dt_bias[l]), pad_heads(a_log[l]),
                   _row(jnp.repeat(d_skip[l], B_HEAD_DIM)), _row(ssm_norm[l]))

        wo = w_out[l].astype(BF16)
        xf = _outproj(xf, y_a, y_b, wo[:a_width], wo[a_width:], _row(mix_norm_post[l]))

        xf = _ffn(xf, _row(ffn2_norm_pre[l]), _row(ffn2_norm_post[l]), None,
                  ffn2_w_gate[l].astype(BF16), ffn2_w_up[l].astype(BF16), ffn2_w_down[l].astype(BF16))
    return xf.reshape(batch, seq, d)
```

```python
import functools

import jax
import jax.numpy as jnp
from jax import lax
from jax.experimental import pallas as pl
from jax.experimental.pallas import tpu as pltpu

F32 = jnp.float32
BF16 = jnp.bfloat16

EPS = 1e-6
HALF_STEP = 0.5
SQRT_HALF = 0.7071067811865476

A_HEAD_DIM = 128
A_CHUNK = 128
B_HEAD_DIM = 64
B_HEADS = 16
B_GROUPS = 2
B_STATE = 128
B_CONV = 4
B_CHUNK = 256
B_WIDTH = B_HEADS * B_HEAD_DIM
B_GROUP_WIDTH = B_WIDTH // B_GROUPS
B_CONV_DIM = B_WIDTH + 2 * B_GROUPS * B_STATE

LANES = 128
SUBLANES = 8
VMEM_LIMIT_BYTES = 56 * 1024 * 1024

FFN_ROW_TILE = 512
FFN_FF_TILE = 1024
FFN_FF_SUB = 512
PROJ_ROW_TILE = 512
SPLIT_PIECES = 2
PROJ_PIECE = 256


def _rmsnorm(x, g):
    return x * lax.rsqrt(jnp.mean(x * x, axis=-1, keepdims=True) + EPS) * g


def _gelu(x):
    return 0.5 * x * (1.0 + lax.erf(x * SQRT_HALF))


def _silu(x):
    return x * jax.nn.sigmoid(x)


def _dot(a, b):
    return jnp.dot(a, b, preferred_element_type=F32)


def _split3(v):
    hi = v.astype(BF16)
    r1 = v - hi.astype(F32)
    mid = r1.astype(BF16)
    lo = (r1 - mid.astype(F32)).astype(BF16)
    return hi, mid, lo


def _params(semantics):
    return pltpu.CompilerParams(dimension_semantics=semantics, vmem_limit_bytes=VMEM_LIMIT_BYTES)


def _ffn_body(*refs, emit_next, n_steps, tail_width):
    if emit_next:
        x_ref, gpre_ref, gpost_ref, gnext_ref, wg_ref, wu_ref, wd_ref, o_ref, on_ref, xn_ref = refs
    else:
        x_ref, gpre_ref, gpost_ref, wg_ref, wu_ref, wd_ref, o_ref, xn_ref = refs
    j = pl.program_id(1)
    tf = wg_ref.shape[1]

    def partial_out(width):
        xn = xn_ref[...]
        acts = []
        for start in range(0, width, FFN_FF_SUB):
            cols = slice(start, min(start + FFN_FF_SUB, width))
            acts.append((_silu(_dot(xn, wg_ref[:, cols])) * _dot(xn, wu_ref[:, cols])).astype(BF16))
        return _dot(jnp.concatenate(acts, axis=1), wd_ref[:width, :])

    @pl.when(j == 0)
    def _():
        xn_ref[...] = _rmsnorm(x_ref[...], gpre_ref[...]).astype(BF16)
        o_ref[...] = partial_out(tf)

    @pl.when(jnp.logical_and(j > 0, j < n_steps - 1))
    def _():
        o_ref[...] += partial_out(tf)

    @pl.when(j == n_steps - 1)
    def _():
        h = o_ref[...] + partial_out(tail_width)
        x_new = x_ref[...] + HALF_STEP * _rmsnorm(h, gpost_ref[...])
        o_ref[...] = x_new
        if emit_next:
            on_ref[...] = _rmsnorm(x_new, gnext_ref[...]).astype(BF16)


def _ffn(x, g_pre, g_post, g_next, wg, wu, wd):
    t, d = x.shape
    f = wg.shape[1]
    tm, tf = FFN_ROW_TILE, FFN_FF_TILE
    n_steps = pl.cdiv(f, tf)
    tail_width = f - (n_steps - 1) * tf
    assert t % tm == 0 and n_steps >= 2 and tail_width % LANES == 0
    emit_next = g_next is not None
    row = pl.BlockSpec((tm, d), lambda i, j: (i, 0))
    vec = pl.BlockSpec((1, d), lambda i, j: (0, 0))
    in_specs = [row, vec, vec] + ([vec] if emit_next else []) + [
        pl.BlockSpec((d, tf), lambda i, j: (0, j)),
        pl.BlockSpec((d, tf), lambda i, j: (0, j)),
        pl.BlockSpec((tf, d), lambda i, j: (j, 0)),
    ]
    out_shape = [jax.ShapeDtypeStruct((t, d), F32)]
    out_specs = [row]
    if emit_next:
        out_shape.append(jax.ShapeDtypeStruct((t, d), BF16))
        out_specs.append(row)
    args = [x, g_pre, g_post] + ([g_next] if emit_next else []) + [wg, wu, wd]
    outs = pl.pallas_call(
        functools.partial(_ffn_body, emit_next=emit_next, n_steps=n_steps, tail_width=tail_width),
        grid=(t // tm, n_steps),
        in_specs=in_specs,
        out_specs=out_specs,
        out_shape=out_shape,
        scratch_shapes=[pltpu.VMEM((tm, d), BF16)],
        compiler_params=_params(("parallel", "arbitrary")),
        name="ffn_next" if emit_next else "ffn",
    )(*args)
    return outs if emit_next else outs[0]


def _sgu_body(xn_ref, wu_ref, wv_ref, gs_ref, ws_ref, bs_ref, o_ref, u_ref, v_ref):
    tm = xn_ref.shape[0]
    heads = ws_ref.shape[0]
    xn = xn_ref[...]
    u_ref[...] = _gelu(_dot(xn, wu_ref[...]))
    v_ref[...] = _rmsnorm(_gelu(_dot(xn, wv_ref[...])), gs_ref[...]).astype(BF16)
    rows = lax.broadcasted_iota(jnp.int32, (A_CHUNK, A_CHUNK), 0)
    cols = lax.broadcasted_iota(jnp.int32, (A_CHUNK, A_CHUNK), 1)
    causal = rows >= cols
    for h in range(heads):
        w = jnp.where(causal, ws_ref[h], 0.0).astype(BF16)
        bias = bs_ref[h]
        lanes = pl.ds(h * A_HEAD_DIM, A_HEAD_DIM)
        for c in range(tm // A_CHUNK):
            rws = pl.ds(c * A_CHUNK, A_CHUNK)
            mixed = _dot(w, v_ref[rws, lanes]) + bias
            o_ref[rws, lanes] = (u_ref[rws, lanes] * mixed).astype(BF16)


def _inproj_sgu(xn, w_u, w_v, g_sgu, w_spatial, b_full):
    t, d = xn.shape
    aw = w_u.shape[1]
    heads = w_spatial.shape[0]
    tm = PROJ_ROW_TILE
    assert t % tm == 0 and tm % A_CHUNK == 0 and aw == heads * A_HEAD_DIM
    const2 = lambda i: (0, 0)
    const3 = lambda i: (0, 0, 0)
    return pl.pallas_call(
        _sgu_body,
        grid=(t // tm,),
        in_specs=[
            pl.BlockSpec((tm, d), lambda i: (i, 0)),
            pl.BlockSpec((d, aw), const2),
            pl.BlockSpec((d, aw), const2),
            pl.BlockSpec((1, aw), const2),
            pl.BlockSpec((heads, A_CHUNK, A_CHUNK), const3),
            pl.BlockSpec((heads, A_CHUNK, A_HEAD_DIM), const3),
        ],
        out_specs=pl.BlockSpec((tm, aw), lambda i: (i, 0)),
        out_shape=jax.ShapeDtypeStruct((t, aw), BF16),
        scratch_shapes=[pltpu.VMEM((tm, aw), F32), pltpu.VMEM((tm, aw), BF16)],
        compiler_params=_params(("parallel",)),
        name="inproj_sgu",
    )(xn, w_u, w_v, g_sgu, w_spatial, b_full)


Z_OFF = 0
XBC_OFF = B_WIDTH
DT_OFF = B_WIDTH + B_CONV_DIM
ZXD_WIDTH = DT_OFF + LANES
CONV_HALO = SUBLANES


def _split_cols(v, pieces):
    parts = []
    rest = v
    for _ in range(pieces):
        part = rest.astype(BF16)
        parts.append(part)
        rest = rest - part.astype(F32)
    return jnp.concatenate(parts, axis=1)


def _ssd_mix(zxd_ref, convw_ref, convb_ref, dtb_ref, alog_ref, dskip_ref, norm_ref, o_ref,
             ext_ref, state_ref, y_ref, side_work=()):
    cl = B_CHUNK
    side_work = list(side_work)

    def side(n=1):
        for _ in range(n):
            if side_work:
                side_work.pop(0)()

    ext_ref[CONV_HALO:CONV_HALO + cl, :] = zxd_ref[:, XBC_OFF:XBC_OFF + B_CONV_DIM]
    ext = ext_ref[...]
    conv = convw_ref[0:1, :] * ext
    for k in range(1, B_CONV):
        conv = pltpu.roll(conv, 1, axis=0) + convw_ref[k:k + 1, :] * ext
    conv = conv[CONV_HALO:, :] + convb_ref[...]
    ext_ref[0:CONV_HALO, :] = ext_ref[cl:cl + CONV_HALO, :]
    side()
    xbc = _silu(conv)
    xs = xbc[:, :B_WIDTH]

    dt = jax.nn.softplus(zxd_ref[:, DT_OFF:DT_OFF + LANES] + dtb_ref[...])
    da = dt * (-jnp.exp(alog_ref[...]))
    rows = lax.broadcasted_iota(jnp.int32, (cl, cl), 0)
    cols = lax.broadcasted_iota(jnp.int32, (cl, cl), 1)
    causal = rows >= cols
    tril = jnp.where(causal, 1.0, 0.0).astype(BF16)
    acs = sum(_dot(tril, part) for part in _split3(da))
    acs_t = acs.T
    side()

    e_rows = lax.broadcasted_iota(jnp.int32, (SPLIT_PIECES * LANES, B_WIDTH), 0)
    e_cols = lax.broadcasted_iota(jnp.int32, (SPLIT_PIECES * LANES, B_WIDTH), 1)
    expand = jnp.where(e_cols // B_HEAD_DIM == e_rows % LANES, 1.0, 0.0).astype(BF16)
    acs_e = _dot(_split_cols(acs, SPLIT_PIECES), expand)
    dt_e = _dot(_split_cols(dt, SPLIT_PIECES), expand)

    x = xs * dt_e
    even_head = (lax.broadcasted_iota(jnp.int32, (cl, B_WIDTH), 1) // B_HEAD_DIM) % 2 == 0
    x_even = jnp.where(even_head, x, 0.0).astype(BF16)
    x_odd = jnp.where(even_head, 0.0, x).astype(BF16)
    acs_last = acs_e[cl - 1:cl, :]
    decay_from_start = jnp.exp(acs_e)
    x_to_end = (x * jnp.exp(acs_last - acs_e)).astype(BF16)
    chunk_decay = jnp.exp(acs_last)
    side()

    heads_per_group = B_HEADS // B_GROUPS
    for g in range(B_GROUPS):
        gl = slice(g * B_GROUP_WIDTH, (g + 1) * B_GROUP_WIDTH)
        b_off = B_WIDTH + g * B_STATE
        c_off = B_WIDTH + B_GROUPS * B_STATE + g * B_STATE
        bc_t = xbc[:, b_off:b_off + B_STATE].T.astype(BF16)
        cc = xbc[:, c_off:c_off + B_STATE].astype(BF16)
        cb = _dot(cc, bc_t)
        state = state_ref[g]
        y_off = _dot(cc, state.astype(BF16)) * decay_from_start[:, gl]
        state_ref[g] = state * chunk_decay[:, gl] + _dot(bc_t, x_to_end[:, gl])
        for pair in range(heads_per_group // 2):
            head_a = g * heads_per_group + 2 * pair
            pl_off = head_a * B_HEAD_DIM
            y_pair = y_off[:, pair * LANES:(pair + 1) * LANES]
            for head, x_half in ((head_a, x_even), (head_a + 1, x_odd)):
                diff = acs[:, head:head + 1] - acs_t[head:head + 1, :]
                m = (cb * jnp.exp(jnp.where(causal, diff, -jnp.inf))).astype(BF16)
                y_pair = y_pair + _dot(m, x_half[:, pl_off:pl_off + LANES])
            y_ref[:, pl_off:pl_off + LANES] = y_pair
            side()

    side(len(side_work))
    y = (y_ref[...] + dskip_ref[...] * xs) * _silu(zxd_ref[:, Z_OFF:Z_OFF + B_WIDTH])
    for g in range(B_GROUPS):
        gl = slice(g * B_GROUP_WIDTH, (g + 1) * B_GROUP_WIDTH)
        yg = y[:, gl]
        yg = yg * lax.rsqrt(jnp.mean(yg * yg, axis=-1, keepdims=True) + EPS)
        o_ref[:, gl] = (yg * norm_ref[:, gl]).astype(BF16)


def _ssd_body(xn_ref, w_ref, convw_ref, convb_ref, dtb_ref, alog_ref, dskip_ref, norm_ref, o_ref,
              zxd0_ref, zxd1_ref, ext_ref, state_ref, y_ref, *, nc):
    s = pl.program_id(1)
    bufs = (zxd0_ref, zxd1_ref)

    def project_pieces(dst_ref):
        def piece(start):
            cols = slice(start, min(start + PROJ_PIECE, ZXD_WIDTH))

            def run():
                dst_ref[:, cols] = _dot(xn_ref[...], w_ref[:, cols])
            return run
        return [piece(start) for start in range(0, ZXD_WIDTH, PROJ_PIECE)]

    def project(dst_ref):
        for run in project_pieces(dst_ref):
            run()

    def mix(src_ref, side_work=()):
        _ssd_mix(src_ref, convw_ref, convb_ref, dtb_ref, alog_ref, dskip_ref, norm_ref, o_ref,
                 ext_ref, state_ref, y_ref, side_work)

    @pl.when(s == 0)
    def _():
        ext_ref[0:CONV_HALO, :] = jnp.zeros((CONV_HALO, B_CONV_DIM), F32)
        state_ref[...] = jnp.zeros_like(state_ref)
        project(bufs[0])

    for parity in range(2):
        @pl.when(jnp.logical_and(jnp.logical_and(s > 0, s < nc), lax.rem(s, 2) == parity))
        def _():
            mix(bufs[1 - parity], project_pieces(bufs[parity]))

    @pl.when(s == nc)
    def _():
        mix(bufs[(nc - 1) % 2])


def _inproj_ssd(xn, w_zxd, batch, conv_w, conv_b, dt_bias, a_log, d_skip, ssm_norm):
    t, d = xn.shape
    seq = t // batch
    assert seq % B_CHUNK == 0 and w_zxd.shape[1] == ZXD_WIDTH
    nc = seq // B_CHUNK
    const = lambda b, s: (0, 0)
    return pl.pallas_call(
        functools.partial(_ssd_body, nc=nc),
        grid=(batch, nc + 1),
        in_specs=[
            pl.BlockSpec((B_CHUNK, d), lambda b, s: (b * nc + jnp.minimum(s, nc - 1), 0)),
            pl.BlockSpec((d, ZXD_WIDTH), const),
            pl.BlockSpec((B_CONV, B_CONV_DIM), const),
            pl.BlockSpec((1, B_CONV_DIM), const),
            pl.BlockSpec((1, LANES), const),
            pl.BlockSpec((1, LANES), const),
            pl.BlockSpec((1, B_WIDTH), const),
            pl.BlockSpec((1, B_WIDTH), const),
        ],
        out_specs=pl.BlockSpec((B_CHUNK, B_WIDTH), lambda b, s: (b * nc + jnp.maximum(s - 1, 0), 0)),
        out_shape=jax.ShapeDtypeStruct((t, B_WIDTH), BF16),
        scratch_shapes=[
            pltpu.VMEM((B_CHUNK, ZXD_WIDTH), F32),
            pltpu.VMEM((B_CHUNK, ZXD_WIDTH), F32),
            pltpu.VMEM((CONV_HALO + B_CHUNK, B_CONV_DIM), F32),
            pltpu.VMEM((B_GROUPS, B_STATE, B_GROUP_WIDTH), F32),
            pltpu.VMEM((B_CHUNK, B_WIDTH), F32),
        ],
        compiler_params=_params(("parallel", "arbitrary")),
        name="inproj_ssd",
    )(xn, w_zxd, conv_w, conv_b, dt_bias, a_log, d_skip, ssm_norm)


def _outproj_body(x_ref, ya_ref, yb_ref, wa_ref, wb_ref, g_ref, o_ref):
    h = _dot(ya_ref[...], wa_ref[...]) + _dot(yb_ref[...], wb_ref[...])
    o_ref[...] = x_ref[...] + _rmsnorm(h, g_ref[...])


def _outproj(x, y_a, y_b, w_a, w_b, g_post):
    t, d = x.shape
    ka, kb = y_a.shape[1], y_b.shape[1]
    tm = PROJ_ROW_TILE
    assert t % tm == 0
    const = lambda i: (0, 0)
    return pl.pallas_call(
        _outproj_body,
        grid=(t // tm,),
        in_specs=[
            pl.BlockSpec((tm, d), lambda i: (i, 0)),
            pl.BlockSpec((tm, ka), lambda i: (i, 0)),
            pl.BlockSpec((tm, kb), lambda i: (i, 0)),
            pl.BlockSpec((ka, d), const),
            pl.BlockSpec((kb, d), const),
            pl.BlockSpec((1, d), const),
        ],
        out_specs=pl.BlockSpec((tm, d), lambda i: (i, 0)),
        out_shape=jax.ShapeDtypeStruct((t, d), F32),
        compiler_params=_params(("parallel",)),
        name="outproj",
    )(x, y_a, y_b, w_a, w_b, g_post)


def _pad_cols(w, n):
    return jnp.pad(w, ((0, 0), (0, n - w.shape[1])))


def _row(v):
    return v.reshape(1, -1).astype(F32)


def kernel(x, ffn1_norm_pre, ffn1_norm_post, ffn1_w_gate, ffn1_w_up, ffn1_w_down, mix_norm_pre, mix_norm_post, w_in, sgu_norm, w_spatial, b_spatial, conv_w, conv_b, dt_bias, a_log, d_skip, ssm_norm, w_out, ffn2_norm_pre, ffn2_norm_post, ffn2_w_gate, ffn2_w_up, ffn2_w_down):
    batch, seq, d = x.shape
    depth = ffn1_norm_pre.shape[0]
    a_width = sgu_norm.shape[1]
    xf = x.reshape(batch * seq, d)
    for l in range(depth):
        xf, xn = _ffn(xf, _row(ffn1_norm_pre[l]), _row(ffn1_norm_post[l]), _row(mix_norm_pre[l]),
                      ffn1_w_gate[l].astype(BF16), ffn1_w_up[l].astype(BF16), ffn1_w_down[l].astype(BF16))

        w = w_in[l]
        w_u = w[:, :a_width].astype(BF16)
        w_v = w[:, a_width:2 * a_width].astype(BF16)
        w_zxd = _pad_cols(w[:, 2 * a_width:], ZXD_WIDTH).astype(BF16)
        b_full = jnp.broadcast_to(b_spatial[l][:, :, None], b_spatial[l].shape + (A_HEAD_DIM,))
        y_a = _inproj_sgu(xn, w_u, w_v, _row(sgu_norm[l]), w_spatial[l], b_full)

        pad_heads = lambda v: jnp.pad(v.astype(F32), (0, LANES - B_HEADS)).reshape(1, LANES)
        y_b = _inproj_ssd(xn, w_zxd, batch, conv_w[l], _row(conv_b[l]), pad_heads(dt_bias[l]),
                          pad_heads(a_log[l]), _row(jnp.repeat(d_skip[l], B_HEAD_DIM)), _row(ssm_norm[l]))

        wo = w_out[l].astype(BF16)
        xf = _outproj(xf, y_a, y_b, wo[:a_width], wo[a_width:], _row(mix_norm_post[l]))

        xf = _ffn(xf, _row(ffn2_norm_pre[l]), _row(ffn2_norm_post[l]), None,
                  ffn2_w_gate[l].astype(BF16), ffn2_w_up[l].astype(BF16), ffn2_w_down[l].astype(BF16))
    return xf.reshape(batch, seq, d)
```

```python
import functools

import jax
import jax.numpy as jnp
from jax import lax
from jax.experimental import pallas as pl
from jax.experimental.pallas import tpu as pltpu

F32 = jnp.float32
BF16 = jnp.bfloat16

EPS = 1e-6
HALF_STEP = 0.5
SQRT_HALF = 0.7071067811865476

A_HEAD_DIM = 128
A_CHUNK = 128
B_HEAD_DIM = 64
B_HEADS = 16
B_GROUPS = 2
B_STATE = 128
B_CONV = 4
B_CHUNK = 256
B_WIDTH = B_HEADS * B_HEAD_DIM
B_GROUP_WIDTH = B_WIDTH // B_GROUPS
B_CONV_DIM = B_WIDTH + 2 * B_GROUPS * B_STATE

LANES = 128
SUBLANES = 8
BF16_SUBLANES = 16
VMEM_LIMIT_BYTES = 56 * 1024 * 1024

FFN_ROW_TILE = 512
FFN_FF_TILE = 1024
FFN_FF_SUB = 512
PROJ_ROW_TILE = 512
SPLIT_PIECES = 2
PROJ_PIECE = 256


def _rmsnorm(x, g):
    return x * lax.rsqrt(jnp.mean(x * x, axis=-1, keepdims=True) + EPS) * g


def _gelu(x):
    return 0.5 * x * (1.0 + lax.erf(x * SQRT_HALF))


def _silu(x):
    return x * jax.nn.sigmoid(x)


def _dot(a, b):
    return jnp.dot(a, b, preferred_element_type=F32)


def _split3(v):
    hi = v.astype(BF16)
    r1 = v - hi.astype(F32)
    mid = r1.astype(BF16)
    lo = (r1 - mid.astype(F32)).astype(BF16)
    return hi, mid, lo


def _params(semantics):
    return pltpu.CompilerParams(dimension_semantics=semantics, vmem_limit_bytes=VMEM_LIMIT_BYTES)


def _ffn_body(*refs, emit_next, n_steps, tail_width, casts):
    refs = list(refs)
    x_ref, gpre_ref, gpost_ref = refs[:3]
    del refs[:3]
    gnext_ref = refs.pop(0) if emit_next else None
    wg_ref, wu_ref, wd_ref = refs[:3]
    del refs[:3]
    cast_srcs = [refs.pop(0) for _ in casts]
    o_ref = refs.pop(0)
    on_ref = refs.pop(0) if emit_next else None
    cast_dsts = [[refs.pop(0) for _ in ranges] for ranges in casts]
    (xn_ref,) = refs
    j = pl.program_id(1)
    tf = wg_ref.shape[1]

    def side_casts():
        for src_ref, dst_refs, ranges in zip(cast_srcs, cast_dsts, casts):
            for dst_ref, (c0, c1) in zip(dst_refs, ranges):
                dst_ref[:, :c1 - c0] = src_ref[:, c0:c1].astype(BF16)
                if dst_ref.shape[1] > c1 - c0:
                    dst_ref[:, c1 - c0:] = jnp.zeros((dst_ref.shape[0], dst_ref.shape[1] - (c1 - c0)), BF16)

    def partial_out(width):
        xn = xn_ref[...]
        acts = []
        for start in range(0, width, FFN_FF_SUB):
            cols = slice(start, min(start + FFN_FF_SUB, width))
            acts.append((_silu(_dot(xn, wg_ref[:, cols])) * _dot(xn, wu_ref[:, cols])).astype(BF16))
            if start == 0:
                side_casts()
        return _dot(jnp.concatenate(acts, axis=1), wd_ref[:width, :])

    @pl.when(j == 0)
    def _():
        xn_ref[...] = _rmsnorm(x_ref[...], gpre_ref[...]).astype(BF16)
        o_ref[...] = partial_out(tf)

    @pl.when(jnp.logical_and(j > 0, j < n_steps - 1))
    def _():
        o_ref[...] += partial_out(tf)

    @pl.when(j == n_steps - 1)
    def _():
        h = o_ref[...] + partial_out(tail_width)
        x_new = x_ref[...] + HALF_STEP * _rmsnorm(h, gpost_ref[...])
        o_ref[...] = x_new
        if emit_next:
            on_ref[...] = _rmsnorm(x_new, gnext_ref[...]).astype(BF16)


def _slab_rows(rows, max_slabs):
    for slab in range(BF16_SUBLANES, rows + 1, BF16_SUBLANES):
        if rows % slab == 0 and rows // slab <= max_slabs:
            return slab
    raise ValueError(f"no slab size for {rows} rows in {max_slabs} steps")


def _ffn(x, g_pre, g_post, g_next, wg, wu, wd, side_casts=()):
    t, d = x.shape
    f = wg.shape[1]
    tm, tf = FFN_ROW_TILE, FFN_FF_TILE
    n_steps = pl.cdiv(f, tf)
    tail_width = f - (n_steps - 1) * tf
    assert t % tm == 0 and n_steps >= 2 and tail_width % LANES == 0
    emit_next = g_next is not None
    row = pl.BlockSpec((tm, d), lambda i, j: (i, 0))
    vec = pl.BlockSpec((1, d), lambda i, j: (0, 0))
    in_specs = [row, vec, vec] + ([vec] if emit_next else []) + [
        pl.BlockSpec((d, tf), lambda i, j: (0, j)),
        pl.BlockSpec((d, tf), lambda i, j: (0, j)),
        pl.BlockSpec((tf, d), lambda i, j: (j, 0)),
    ]
    out_shape = [jax.ShapeDtypeStruct((t, d), F32)]
    out_specs = [row]
    if emit_next:
        out_shape.append(jax.ShapeDtypeStruct((t, d), BF16))
        out_specs.append(row)
    args = [x, g_pre, g_post] + ([g_next] if emit_next else []) + [wg, wu, wd]
    total_steps = (t // tm) * n_steps
    cast_ranges = []
    for src, ranges in side_casts:
        slab_rows = _slab_rows(src.shape[0], total_steps)
        n_slabs = src.shape[0] // slab_rows
        slab = lambda i, j, n_slabs=n_slabs: (jnp.minimum(i * n_steps + j, n_slabs - 1), 0)
        in_specs.append(pl.BlockSpec((slab_rows, src.shape[1]), slab))
        args.append(src)
        for _, _, width in ranges:
            out_shape.append(jax.ShapeDtypeStruct((src.shape[0], width), BF16))
            out_specs.append(pl.BlockSpec((slab_rows, width), slab))
        cast_ranges.append(tuple((c0, c1) for c0, c1, _ in ranges))
    outs = pl.pallas_call(
        functools.partial(_ffn_body, emit_next=emit_next, n_steps=n_steps, tail_width=tail_width,
                          casts=tuple(cast_ranges)),
        grid=(t // tm, n_steps),
        in_specs=in_specs,
        out_specs=out_specs,
        out_shape=out_shape,
        scratch_shapes=[pltpu.VMEM((tm, d), BF16)],
        compiler_params=_params(("arbitrary", "arbitrary") if side_casts else ("parallel", "arbitrary")),
        name="ffn_next" if emit_next else "ffn",
    )(*args)
    return outs


def _sgu_body(xn_ref, wu_ref, wv_ref, gs_ref, ws_ref, bs_ref, o_ref, u_ref, v_ref):
    tm = xn_ref.shape[0]
    heads = ws_ref.shape[0]
    xn = xn_ref[...]
    u_ref[...] = _gelu(_dot(xn, wu_ref[...]))
    v_ref[...] = _rmsnorm(_gelu(_dot(xn, wv_ref[...])), gs_ref[...]).astype(BF16)
    rows = lax.broadcasted_iota(jnp.int32, (A_CHUNK, A_CHUNK), 0)
    cols = lax.broadcasted_iota(jnp.int32, (A_CHUNK, A_CHUNK), 1)
    causal = rows >= cols
    for h in range(heads):
        w = jnp.where(causal, ws_ref[h], 0.0).astype(BF16)
        bias = bs_ref[h]
        lanes = pl.ds(h * A_HEAD_DIM, A_HEAD_DIM)
        for c in range(tm // A_CHUNK):
            rws = pl.ds(c * A_CHUNK, A_CHUNK)
            mixed = _dot(w, v_ref[rws, lanes]) + bias
            o_ref[rws, lanes] = (u_ref[rws, lanes] * mixed).astype(BF16)


def _inproj_sgu(xn, w_uv, g_sgu, w_spatial, b_full):
    t, d = xn.shape
    aw = w_uv.shape[1] // 2
    heads = w_spatial.shape[0]
    tm = PROJ_ROW_TILE
    assert t % tm == 0 and tm % A_CHUNK == 0 and aw == heads * A_HEAD_DIM
    const2 = lambda i: (0, 0)
    const3 = lambda i: (0, 0, 0)
    return pl.pallas_call(
        _sgu_body,
        grid=(t // tm,),
        in_specs=[
            pl.BlockSpec((tm, d), lambda i: (i, 0)),
            pl.BlockSpec((d, aw), const2),
            pl.BlockSpec((d, aw), lambda i: (0, 1)),
            pl.BlockSpec((1, aw), const2),
            pl.BlockSpec((heads, A_CHUNK, A_CHUNK), const3),
            pl.BlockSpec((heads, A_CHUNK, A_HEAD_DIM), const3),
        ],
        out_specs=pl.BlockSpec((tm, aw), lambda i: (i, 0)),
        out_shape=jax.ShapeDtypeStruct((t, aw), BF16),
        scratch_shapes=[pltpu.VMEM((tm, aw), F32), pltpu.VMEM((tm, aw), BF16)],
        compiler_params=_params(("parallel",)),
        name="inproj_sgu",
    )(xn, w_uv, w_uv, g_sgu, w_spatial, b_full)


Z_OFF = 0
XBC_OFF = B_WIDTH
DT_OFF = B_WIDTH + B_CONV_DIM
ZXD_WIDTH = DT_OFF + LANES
CONV_HALO = SUBLANES


def _split_cols(v, pieces):
    parts = []
    rest = v
    for _ in range(pieces):
        part = rest.astype(BF16)
        parts.append(part)
        rest = rest - part.astype(F32)
    return jnp.concatenate(parts, axis=1)


def _ssd_mix(zxd_ref, convw_ref, convb_ref, dtb_ref, alog_ref, dskip_ref, norm_ref, o_ref,
             ext_ref, state_ref, y_ref, side_work=()):
    cl = B_CHUNK
    side_work = list(side_work)

    def side(n=1):
        for _ in range(n):
            if side_work:
                side_work.pop(0)()

    ext_ref[CONV_HALO:CONV_HALO + cl, :] = zxd_ref[:, XBC_OFF:XBC_OFF + B_CONV_DIM]
    ext = ext_ref[...]
    conv = convw_ref[0:1, :] * ext
    for k in range(1, B_CONV):
        conv = pltpu.roll(conv, 1, axis=0) + convw_ref[k:k + 1, :] * ext
    conv = conv[CONV_HALO:, :] + convb_ref[...]
    ext_ref[0:CONV_HALO, :] = ext_ref[cl:cl + CONV_HALO, :]
    side()
    xbc = _silu(conv)
    xs = xbc[:, :B_WIDTH]

    dt = jax.nn.softplus(zxd_ref[:, DT_OFF:DT_OFF + LANES] + dtb_ref[...])
    da = dt * (-jnp.exp(alog_ref[...]))
    rows = lax.broadcasted_iota(jnp.int32, (cl, cl), 0)
    cols = lax.broadcasted_iota(jnp.int32, (cl, cl), 1)
    causal = rows >= cols
    tril = jnp.where(causal, 1.0, 0.0).astype(BF16)
    acs = sum(_dot(tril, part) for part in _split3(da))
    acs_t = acs.T
    side()

    e_rows = lax.broadcasted_iota(jnp.int32, (SPLIT_PIECES * LANES, B_WIDTH), 0)
    e_cols = lax.broadcasted_iota(jnp.int32, (SPLIT_PIECES * LANES, B_WIDTH), 1)
    expand = jnp.where(e_cols // B_HEAD_DIM == e_rows % LANES, 1.0, 0.0).astype(BF16)
    acs_e = _dot(_split_cols(acs, SPLIT_PIECES), expand)
    dt_e = _dot(_split_cols(dt, SPLIT_PIECES), expand)

    x = xs * dt_e
    even_head = (lax.broadcasted_iota(jnp.int32, (cl, B_WIDTH), 1) // B_HEAD_DIM) % 2 == 0
    x_even = jnp.where(even_head, x, 0.0).astype(BF16)
    x_odd = jnp.where(even_head, 0.0, x).astype(BF16)
    acs_last = acs_e[cl - 1:cl, :]
    decay_from_start = jnp.exp(acs_e)
    x_to_end = (x * jnp.exp(acs_last - acs_e)).astype(BF16)
    chunk_decay = jnp.exp(acs_last)
    side()

    heads_per_group = B_HEADS // B_GROUPS
    for g in range(B_GROUPS):
        gl = slice(g * B_GROUP_WIDTH, (g + 1) * B_GROUP_WIDTH)
        b_off = B_WIDTH + g * B_STATE
        c_off = B_WIDTH + B_GROUPS * B_STATE + g * B_STATE
        bc_t = xbc[:, b_off:b_off + B_STATE].T.astype(BF16)
        cc = xbc[:, c_off:c_off + B_STATE].astype(BF16)
        cb = _dot(cc, bc_t)
        state = state_ref[g]
        y_off = _dot(cc, state.astype(BF16)) * decay_from_start[:, gl]
        state_ref[g] = state * chunk_decay[:, gl] + _dot(bc_t, x_to_end[:, gl])
        for pair in range(heads_per_group // 2):
            head_a = g * heads_per_group + 2 * pair
            pl_off = head_a * B_HEAD_DIM
            y_pair = y_off[:, pair * LANES:(pair + 1) * LANES]
            for head, x_half in ((head_a, x_even), (head_a + 1, x_odd)):
                diff = acs[:, head:head + 1] - acs_t[head:head + 1, :]
                m = (cb * jnp.exp(jnp.where(causal, diff, -jnp.inf))).astype(BF16)
                y_pair = y_pair + _dot(m, x_half[:, pl_off:pl_off + LANES])
            y_ref[:, pl_off:pl_off + LANES] = y_pair
            side()

    side(len(side_work))
    y = (y_ref[...] + dskip_ref[...] * xs) * _silu(zxd_ref[:, Z_OFF:Z_OFF + B_WIDTH])
    for g in range(B_GROUPS):
        gl = slice(g * B_GROUP_WIDTH, (g + 1) * B_GROUP_WIDTH)
        yg = y[:, gl]
        yg = yg * lax.rsqrt(jnp.mean(yg * yg, axis=-1, keepdims=True) + EPS)
        o_ref[:, gl] = (yg * norm_ref[:, gl]).astype(BF16)


def _ssd_body(xn_ref, w_ref, convw_ref, convb_ref, dtb_ref, alog_ref, dskip_ref, norm_ref, o_ref,
              zxd0_ref, zxd1_ref, ext_ref, state_ref, y_ref, *, nc):
    s = pl.program_id(1)
    bufs = (zxd0_ref, zxd1_ref)

    def project_pieces(dst_ref):
        def piece(start):
            cols = slice(start, min(start + PROJ_PIECE, ZXD_WIDTH))

            def run():
                dst_ref[:, cols] = _dot(xn_ref[...], w_ref[:, cols])
            return run
        return [piece(start) for start in range(0, ZXD_WIDTH, PROJ_PIECE)]

    def project(dst_ref):
        for run in project_pieces(dst_ref):
            run()

    def mix(src_ref, side_work=()):
        _ssd_mix(src_ref, convw_ref, convb_ref, dtb_ref, alog_ref, dskip_ref, norm_ref, o_ref,
                 ext_ref, state_ref, y_ref, side_work)

    @pl.when(s == 0)
    def _():
        ext_ref[0:CONV_HALO, :] = jnp.zeros((CONV_HALO, B_CONV_DIM), F32)
        state_ref[...] = jnp.zeros_like(state_ref)
        project(bufs[0])

    for parity in range(2):
        @pl.when(jnp.logical_and(jnp.logical_and(s > 0, s < nc), lax.rem(s, 2) == parity))
        def _():
            mix(bufs[1 - parity], project_pieces(bufs[parity]))

    @pl.when(s == nc)
    def _():
        mix(bufs[(nc - 1) % 2])


def _inproj_ssd(xn, w_zxd, batch, conv_w, conv_b, dt_bias, a_log, d_skip, ssm_norm):
    t, d = xn.shape
    seq = t // batch
    assert seq % B_CHUNK == 0 and w_zxd.shape[1] == ZXD_WIDTH
    nc = seq // B_CHUNK
    const = lambda b, s: (0, 0)
    return pl.pallas_call(
        functools.partial(_ssd_body, nc=nc),
        grid=(batch, nc + 1),
        in_specs=[
            pl.BlockSpec((B_CHUNK, d), lambda b, s: (b * nc + jnp.minimum(s, nc - 1), 0)),
            pl.BlockSpec((d, ZXD_WIDTH), const),
            pl.BlockSpec((B_CONV, B_CONV_DIM), const),
            pl.BlockSpec((1, B_CONV_DIM), const),
            pl.BlockSpec((1, LANES), const),
            pl.BlockSpec((1, LANES), const),
            pl.BlockSpec((1, B_WIDTH), const),
            pl.BlockSpec((1, B_WIDTH), const),
        ],
        out_specs=pl.BlockSpec((B_CHUNK, B_WIDTH), lambda b, s: (b * nc + jnp.maximum(s - 1, 0), 0)),
        out_shape=jax.ShapeDtypeStruct((t, B_WIDTH), BF16),
        scratch_shapes=[
            pltpu.VMEM((B_CHUNK, ZXD_WIDTH), F32),
            pltpu.VMEM((B_CHUNK, ZXD_WIDTH), F32),
            pltpu.VMEM((CONV_HALO + B_CHUNK, B_CONV_DIM), F32),
            pltpu.VMEM((B_GROUPS, B_STATE, B_GROUP_WIDTH), F32),
            pltpu.VMEM((B_CHUNK, B_WIDTH), F32),
        ],
        compiler_params=_params(("parallel", "arbitrary")),
        name="inproj_ssd",
    )(xn, w_zxd, conv_w, conv_b, dt_bias, a_log, d_skip, ssm_norm)


def _outproj_body(x_ref, ya_ref, yb_ref, wa_ref, wb_ref, g_ref, o_ref):
    h = _dot(ya_ref[...], wa_ref[...]) + _dot(yb_ref[...], wb_ref[...])
    o_ref[...] = x_ref[...] + _rmsnorm(h, g_ref[...])


def _outproj(x, y_a, y_b, w_out, g_post):
    t, d = x.shape
    ka, kb = y_a.shape[1], y_b.shape[1]
    tm = PROJ_ROW_TILE
    assert t % tm == 0 and ka == kb and w_out.shape[0] == ka + kb
    const = lambda i: (0, 0)
    return pl.pallas_call(
        _outproj_body,
        grid=(t // tm,),
        in_specs=[
            pl.BlockSpec((tm, d), lambda i: (i, 0)),
            pl.BlockSpec((tm, ka), lambda i: (i, 0)),
            pl.BlockSpec((tm, kb), lambda i: (i, 0)),
            pl.BlockSpec((ka, d), const),
            pl.BlockSpec((kb, d), lambda i: (1, 0)),
            pl.BlockSpec((1, d), const),
        ],
        out_specs=pl.BlockSpec((tm, d), lambda i: (i, 0)),
        out_shape=jax.ShapeDtypeStruct((t, d), F32),
        compiler_params=_params(("parallel",)),
        name="outproj",
    )(x, y_a, y_b, w_out, w_out, g_post)


def _row(v):
    return v.reshape(1, -1).astype(F32)


def kernel(x, ffn1_norm_pre, ffn1_norm_post, ffn1_w_gate, ffn1_w_up, ffn1_w_down, mix_norm_pre, mix_norm_post, w_in, sgu_norm, w_spatial, b_spatial, conv_w, conv_b, dt_bias, a_log, d_skip, ssm_norm, w_out, ffn2_norm_pre, ffn2_norm_post, ffn2_w_gate, ffn2_w_up, ffn2_w_down):
    batch, seq, d = x.shape
    depth = ffn1_norm_pre.shape[0]
    a_width = sgu_norm.shape[1]
    xf = x.reshape(batch * seq, d)
    for l in range(depth):
        n_in = w_in.shape[2]
        whole = lambda w: ((0, w.shape[1], w.shape[1]),)
        side_casts = (
            (ffn2_w_gate[l], whole(ffn2_w_gate[l])),
            (ffn2_w_up[l], whole(ffn2_w_up[l])),
            (ffn2_w_down[l], whole(ffn2_w_down[l])),
            (w_in[l], ((0, 2 * a_width, 2 * a_width), (2 * a_width, n_in, ZXD_WIDTH))),
            (w_out[l], whole(w_out[l])),
        )
        xf, xn, w2_gate, w2_up, w2_down, w_uv, w_zxd, wo = _ffn(
            xf, _row(ffn1_norm_pre[l]), _row(ffn1_norm_post[l]), _row(mix_norm_pre[l]),
            ffn1_w_gate[l].astype(BF16), ffn1_w_up[l].astype(BF16), ffn1_w_down[l].astype(BF16), side_casts)

        b_full = jnp.broadcast_to(b_spatial[l][:, :, None], b_spatial[l].shape + (A_HEAD_DIM,))
        y_a = _inproj_sgu(xn, w_uv, _row(sgu_norm[l]), w_spatial[l], b_full)

        pad_heads = lambda v: jnp.pad(v.astype(F32), (0, LANES - B_HEADS)).reshape(1, LANES)
        y_b = _inproj_ssd(xn, w_zxd, batch, conv_w[l], _row(conv_b[l]), pad_heads(dt_bias[l]),
                          pad_heads(a_log[l]), _row(jnp.repeat(d_skip[l], B_HEAD_DIM)), _row(ssm_norm[l]))

        xf = _outproj(xf, y_a, y_b, wo, _row(mix_norm_post[l]))

        (xf,) = _ffn(xf, _row(ffn2_norm_pre[l]), _row(ffn2_norm_post[l]), None, w2_gate, w2_up, w2_down)
    return xf.reshape(batch, seq, d)
```

```python
import functools

import jax
import jax.numpy as jnp
from jax import lax
from jax.experimental import pallas as pl
from jax.experimental.pallas import tpu as pltpu

F32 = jnp.float32
BF16 = jnp.bfloat16

EPS = 1e-6
HALF_STEP = 0.5
SQRT_HALF = 0.7071067811865476

A_HEAD_DIM = 128
A_CHUNK = 128
B_HEAD_DIM = 64
B_HEADS = 16
B_GROUPS = 2
B_STATE = 128
B_CONV = 4
B_CHUNK = 256
B_WIDTH = B_HEADS * B_HEAD_DIM
B_GROUP_WIDTH = B_WIDTH // B_GROUPS
B_CONV_DIM = B_WIDTH + 2 * B_GROUPS * B_STATE

LANES = 128
SUBLANES = 8
BF16_SUBLANES = 16
VMEM_LIMIT_BYTES = 60 * 1024 * 1024

FFN_TILES_SIDE_OUTPUTS = (512, 1024, 512)
FFN_TILES_PLAIN = (1024, 512, 256)
PROJ_ROW_TILE = 1024
SPLIT_PIECES = 2
PROJ_PIECE = 256


def _rmsnorm(x, g):
    return x * lax.rsqrt(jnp.mean(x * x, axis=-1, keepdims=True) + EPS) * g


def _gelu(x):
    return 0.5 * x * (1.0 + lax.erf(x * SQRT_HALF))


def _silu(x):
    return x * jax.nn.sigmoid(x)


def _dot(a, b):
    return jnp.dot(a, b, preferred_element_type=F32)


def _split3(v):
    hi = v.astype(BF16)
    r1 = v - hi.astype(F32)
    mid = r1.astype(BF16)
    lo = (r1 - mid.astype(F32)).astype(BF16)
    return hi, mid, lo


def _params(semantics):
    return pltpu.CompilerParams(dimension_semantics=semantics, vmem_limit_bytes=VMEM_LIMIT_BYTES)


def _ffn_body(*refs, emit_next, n_steps, tail_width, ff_sub, casts):
    refs = list(refs)
    x_ref, gpre_ref, gpost_ref = refs[:3]
    del refs[:3]
    gnext_ref = refs.pop(0) if emit_next else None
    wg_ref, wu_ref, wd_ref = refs[:3]
    del refs[:3]
    cast_srcs = [refs.pop(0) for _ in casts]
    o_ref = refs.pop(0)
    on_ref = refs.pop(0) if emit_next else None
    cast_dsts = [[refs.pop(0) for _ in ranges] for ranges in casts]
    (xn_ref,) = refs
    j = pl.program_id(1)
    tf = wg_ref.shape[1]

    def side_casts():
        for src_ref, dst_refs, ranges in zip(cast_srcs, cast_dsts, casts):
            for dst_ref, (c0, c1) in zip(dst_refs, ranges):
                dst_ref[:, :c1 - c0] = src_ref[:, c0:c1].astype(BF16)
                if dst_ref.shape[1] > c1 - c0:
                    dst_ref[:, c1 - c0:] = jnp.zeros((dst_ref.shape[0], dst_ref.shape[1] - (c1 - c0)), BF16)

    def partial_out(width):
        xn = xn_ref[...]
        acts = []
        for start in range(0, width, ff_sub):
            cols = slice(start, min(start + ff_sub, width))
            acts.append((_silu(_dot(xn, wg_ref[:, cols])) * _dot(xn, wu_ref[:, cols])).astype(BF16))
            if start == 0:
                side_casts()
        return _dot(jnp.concatenate(acts, axis=1), wd_ref[:width, :])

    @pl.when(j == 0)
    def _():
        xn_ref[...] = _rmsnorm(x_ref[...], gpre_ref[...]).astype(BF16)
        o_ref[...] = partial_out(tf)

    @pl.when(jnp.logical_and(j > 0, j < n_steps - 1))
    def _():
        o_ref[...] += partial_out(tf)

    @pl.when(j == n_steps - 1)
    def _():
        h = o_ref[...] + partial_out(tail_width)
        x_new = x_ref[...] + HALF_STEP * _rmsnorm(h, gpost_ref[...])
        o_ref[...] = x_new
        if emit_next:
            on_ref[...] = _rmsnorm(x_new, gnext_ref[...]).astype(BF16)


def _slab_rows(rows, max_slabs):
    for slab in range(BF16_SUBLANES, rows + 1, BF16_SUBLANES):
        if rows % slab == 0 and rows // slab <= max_slabs:
            return slab
    raise ValueError(f"no slab size for {rows} rows in {max_slabs} steps")


def _ffn(x, g_pre, g_post, g_next, wg, wu, wd, side_casts=()):
    t, d = x.shape
    f = wg.shape[1]
    emit_next = g_next is not None
    tm, tf, ff_sub = FFN_TILES_SIDE_OUTPUTS if (emit_next or side_casts) else FFN_TILES_PLAIN
    n_steps = pl.cdiv(f, tf)
    tail_width = f - (n_steps - 1) * tf
    assert t % tm == 0 and n_steps >= 2 and tail_width % LANES == 0
    row = pl.BlockSpec((tm, d), lambda i, j: (i, 0))
    vec = pl.BlockSpec((1, d), lambda i, j: (0, 0))
    in_specs = [row, vec, vec] + ([vec] if emit_next else []) + [
        pl.BlockSpec((d, tf), lambda i, j: (0, j)),
        pl.BlockSpec((d, tf), lambda i, j: (0, j)),
        pl.BlockSpec((tf, d), lambda i, j: (j, 0)),
    ]
    out_shape = [jax.ShapeDtypeStruct((t, d), F32)]
    out_specs = [row]
    if emit_next:
        out_shape.append(jax.ShapeDtypeStruct((t, d), BF16))
        out_specs.append(row)
    args = [x, g_pre, g_post] + ([g_next] if emit_next else []) + [wg, wu, wd]
    total_steps = (t // tm) * n_steps
    cast_ranges = []
    for src, ranges in side_casts:
        slab_rows = _slab_rows(src.shape[0], total_steps)
        n_slabs = src.shape[0] // slab_rows
        slab = lambda i, j, n_slabs=n_slabs: (jnp.minimum(i * n_steps + j, n_slabs - 1), 0)
        in_specs.append(pl.BlockSpec((slab_rows, src.shape[1]), slab))
        args.append(src)
        for _, _, width in ranges:
            out_shape.append(jax.ShapeDtypeStruct((src.shape[0], width), BF16))
            out_specs.append(pl.BlockSpec((slab_rows, width), slab))
        cast_ranges.append(tuple((c0, c1) for c0, c1, _ in ranges))
    outs = pl.pallas_call(
        functools.partial(_ffn_body, emit_next=emit_next, n_steps=n_steps, tail_width=tail_width,
                          ff_sub=ff_sub, casts=tuple(cast_ranges)),
        grid=(t // tm, n_steps),
        in_specs=in_specs,
        out_specs=out_specs,
        out_shape=out_shape,
        scratch_shapes=[pltpu.VMEM((tm, d), BF16)],
        compiler_params=_params(("arbitrary", "arbitrary") if side_casts else ("parallel", "arbitrary")),
        name="ffn_next" if emit_next else "ffn",
    )(*args)
    return outs


def _sgu_body(xn_ref, wu_ref, wv_ref, gs_ref, ws_ref, bs_ref, o_ref, u_ref, v_ref):
    tm = xn_ref.shape[0]
    heads = ws_ref.shape[0]
    xn = xn_ref[...]
    u_ref[...] = _gelu(_dot(xn, wu_ref[...]))
    v_ref[...] = _rmsnorm(_gelu(_dot(xn, wv_ref[...])), gs_ref[...]).astype(BF16)
    rows = lax.broadcasted_iota(jnp.int32, (A_CHUNK, A_CHUNK), 0)
    cols = lax.broadcasted_iota(jnp.int32, (A_CHUNK, A_CHUNK), 1)
    causal = rows >= cols
    for h in range(heads):
        w = jnp.where(causal, ws_ref[h], 0.0).astype(BF16)
        bias = bs_ref[h]
        lanes = pl.ds(h * A_HEAD_DIM, A_HEAD_DIM)
        for c in range(tm // A_CHUNK):
            rws = pl.ds(c * A_CHUNK, A_CHUNK)
            mixed = _dot(w, v_ref[rws, lanes]) + bias
            o_ref[rws, lanes] = (u_ref[rws, lanes] * mixed).astype(BF16)


def _inproj_sgu(xn, w_uv, g_sgu, w_spatial, b_full):
    t, d = xn.shape
    aw = w_uv.shape[1] // 2
    heads = w_spatial.shape[0]
    tm = PROJ_ROW_TILE
    assert t % tm == 0 and tm % A_CHUNK == 0 and aw == heads * A_HEAD_DIM
    const2 = lambda i: (0, 0)
    const3 = lambda i: (0, 0, 0)
    return pl.pallas_call(
        _sgu_body,
        grid=(t // tm,),
        in_specs=[
            pl.BlockSpec((tm, d), lambda i: (i, 0)),
            pl.BlockSpec((d, aw), const2, pipeline_mode=pl.Buffered(1)),
            pl.BlockSpec((d, aw), lambda i: (0, 1), pipeline_mode=pl.Buffered(1)),
            pl.BlockSpec((1, aw), const2),
            pl.BlockSpec((heads, A_CHUNK, A_CHUNK), const3),
            pl.BlockSpec((heads, A_CHUNK, A_HEAD_DIM), const3),
        ],
        out_specs=pl.BlockSpec((tm, aw), lambda i: (i, 0)),
        out_shape=jax.ShapeDtypeStruct((t, aw), BF16),
        scratch_shapes=[pltpu.VMEM((tm, aw), F32), pltpu.VMEM((tm, aw), BF16)],
        compiler_params=_params(("parallel",)),
        name="inproj_sgu",
    )(xn, w_uv, w_uv, g_sgu, w_spatial, b_full)


Z_OFF = 0
XBC_OFF = B_WIDTH
DT_OFF = B_WIDTH + B_CONV_DIM
ZXD_WIDTH = DT_OFF + LANES
CONV_HALO = SUBLANES


def _split_cols(v, pieces):
    parts = []
    rest = v
    for _ in range(pieces):
        part = rest.astype(BF16)
        parts.append(part)
        rest = rest - part.astype(F32)
    return jnp.concatenate(parts, axis=1)


def _ssd_mix(zxd_ref, convw_ref, convb_ref, dtb_ref, alog_ref, dskip_ref, norm_ref, o_ref,
             ext_ref, state_ref, y_ref, side_work=()):
    cl = B_CHUNK
    side_work = list(side_work)

    def side(n=1):
        for _ in range(n):
            if side_work:
                side_work.pop(0)()

    ext_ref[CONV_HALO:CONV_HALO + cl, :] = zxd_ref[:, XBC_OFF:XBC_OFF + B_CONV_DIM]
    ext = ext_ref[...]
    conv = convw_ref[0:1, :] * ext
    for k in range(1, B_CONV):
        conv = pltpu.roll(conv, 1, axis=0) + convw_ref[k:k + 1, :] * ext
    conv = conv[CONV_HALO:, :] + convb_ref[...]
    ext_ref[0:CONV_HALO, :] = ext_ref[cl:cl + CONV_HALO, :]
    side()
    xbc = _silu(conv)
    xs = xbc[:, :B_WIDTH]

    dt = jax.nn.softplus(zxd_ref[:, DT_OFF:DT_OFF + LANES] + dtb_ref[...])
    da = dt * (-jnp.exp(alog_ref[...]))
    rows = lax.broadcasted_iota(jnp.int32, (cl, cl), 0)
    cols = lax.broadcasted_iota(jnp.int32, (cl, cl), 1)
    causal = rows >= cols
    tril = jnp.where(causal, 1.0, 0.0).astype(BF16)
    acs = sum(_dot(tril, part) for part in _split3(da))
    acs_t = acs.T
    side()

    e_rows = lax.broadcasted_iota(jnp.int32, (SPLIT_PIECES * LANES, B_WIDTH), 0)
    e_cols = lax.broadcasted_iota(jnp.int32, (SPLIT_PIECES * LANES, B_WIDTH), 1)
    expand = jnp.where(e_cols // B_HEAD_DIM == e_rows % LANES, 1.0, 0.0).astype(BF16)
    acs_e = _dot(_split_cols(acs, SPLIT_PIECES), expand)
    dt_e = _dot(_split_cols(dt, SPLIT_PIECES), expand)

    x = xs * dt_e
    even_head = (lax.broadcasted_iota(jnp.int32, (cl, B_WIDTH), 1) // B_HEAD_DIM) % 2 == 0
    x_even = jnp.where(even_head, x, 0.0).astype(BF16)
    x_odd = jnp.where(even_head, 0.0, x).astype(BF16)
    acs_last = acs_e[cl - 1:cl, :]
    decay_from_start = jnp.exp(acs_e)
    x_to_end = (x * jnp.exp(acs_last - acs_e)).astype(BF16)
    chunk_decay = jnp.exp(acs_last)
    side()

    heads_per_group = B_HEADS // B_GROUPS
    for g in range(B_GROUPS):
        gl = slice(g * B_GROUP_WIDTH, (g + 1) * B_GROUP_WIDTH)
        b_off = B_WIDTH + g * B_STATE
        c_off = B_WIDTH + B_GROUPS * B_STATE + g * B_STATE
        bc_t = xbc[:, b_off:b_off + B_STATE].T.astype(BF16)
        cc = xbc[:, c_off:c_off + B_STATE].astype(BF16)
        cb = _dot(cc, bc_t)
        state = state_ref[g]
        y_off = _dot(cc, state.astype(BF16)) * decay_from_start[:, gl]
        state_ref[g] = state * chunk_decay[:, gl] + _dot(bc_t, x_to_end[:, gl])
        for pair in range(heads_per_group // 2):
            head_a = g * heads_per_group + 2 * pair
            pl_off = head_a * B_HEAD_DIM
            y_pair = y_off[:, pair * LANES:(pair + 1) * LANES]
            for head, x_half in ((head_a, x_even), (head_a + 1, x_odd)):
                diff = acs[:, head:head + 1] - acs_t[head:head + 1, :]
                m = (cb * jnp.exp(jnp.where(causal, diff, -jnp.inf))).astype(BF16)
                y_pair = y_pair + _dot(m, x_half[:, pl_off:pl_off + LANES])
            y_ref[:, pl_off:pl_off + LANES] = y_pair
            side()

    side(len(side_work))
    y = (y_ref[...] + dskip_ref[...] * xs) * _silu(zxd_ref[:, Z_OFF:Z_OFF + B_WIDTH])
    for g in range(B_GROUPS):
        gl = slice(g * B_GROUP_WIDTH, (g + 1) * B_GROUP_WIDTH)
        yg = y[:, gl]
        yg = yg * lax.rsqrt(jnp.mean(yg * yg, axis=-1, keepdims=True) + EPS)
        o_ref[:, gl] = (yg * norm_ref[:, gl]).astype(BF16)


def _ssd_body(xn_ref, w_ref, convw_ref, convb_ref, dtb_ref, alog_ref, dskip_ref, norm_ref, o_ref,
              zxd0_ref, zxd1_ref, ext_ref, state_ref, y_ref, *, nc):
    s = pl.program_id(1)
    bufs = (zxd0_ref, zxd1_ref)

    def project_pieces(dst_ref):
        def piece(start):
            cols = slice(start, min(start + PROJ_PIECE, ZXD_WIDTH))

            def run():
                dst_ref[:, cols] = _dot(xn_ref[...], w_ref[:, cols])
            return run
        return [piece(start) for start in range(0, ZXD_WIDTH, PROJ_PIECE)]

    def project(dst_ref):
        for run in project_pieces(dst_ref):
            run()

    def mix(src_ref, side_work=()):
        _ssd_mix(src_ref, convw_ref, convb_ref, dtb_ref, alog_ref, dskip_ref, norm_ref, o_ref,
                 ext_ref, state_ref, y_ref, side_work)

    @pl.when(s == 0)
    def _():
        ext_ref[0:CONV_HALO, :] = jnp.zeros((CONV_HALO, B_CONV_DIM), F32)
        state_ref[...] = jnp.zeros_like(state_ref)
        project(bufs[0])

    for parity in range(2):
        @pl.when(jnp.logical_and(jnp.logical_and(s > 0, s < nc), lax.rem(s, 2) == parity))
        def _():
            mix(bufs[1 - parity], project_pieces(bufs[parity]))

    @pl.when(s == nc)
    def _():
        mix(bufs[(nc - 1) % 2])


def _inproj_ssd(xn, w_zxd, batch, conv_w, conv_b, dt_bias, a_log, d_skip, ssm_norm):
    t, d = xn.shape
    seq = t // batch
    assert seq % B_CHUNK == 0 and w_zxd.shape[1] == ZXD_WIDTH
    nc = seq // B_CHUNK
    const = lambda b, s: (0, 0)
    return pl.pallas_call(
        functools.partial(_ssd_body, nc=nc),
        grid=(batch, nc + 1),
        in_specs=[
            pl.BlockSpec((B_CHUNK, d), lambda b, s: (b * nc + jnp.minimum(s, nc - 1), 0)),
            pl.BlockSpec((d, ZXD_WIDTH), const),
            pl.BlockSpec((B_CONV, B_CONV_DIM), const),
            pl.BlockSpec((1, B_CONV_DIM), const),
            pl.BlockSpec((1, LANES), const),
            pl.BlockSpec((1, LANES), const),
            pl.BlockSpec((1, B_WIDTH), const),
            pl.BlockSpec((1, B_WIDTH), const),
        ],
        out_specs=pl.BlockSpec((B_CHUNK, B_WIDTH), lambda b, s: (b * nc + jnp.maximum(s - 1, 0), 0)),
        out_shape=jax.ShapeDtypeStruct((t, B_WIDTH), BF16),
        scratch_shapes=[
            pltpu.VMEM((B_CHUNK, ZXD_WIDTH), F32),
            pltpu.VMEM((B_CHUNK, ZXD_WIDTH), F32),
            pltpu.VMEM((CONV_HALO + B_CHUNK, B_CONV_DIM), F32),
            pltpu.VMEM((B_GROUPS, B_STATE, B_GROUP_WIDTH), F32),
            pltpu.VMEM((B_CHUNK, B_WIDTH), F32),
        ],
        compiler_params=_params(("parallel", "arbitrary")),
        name="inproj_ssd",
    )(xn, w_zxd, conv_w, conv_b, dt_bias, a_log, d_skip, ssm_norm)


def _outproj_body(x_ref, ya_ref, yb_ref, wa_ref, wb_ref, g_ref, o_ref):
    h = _dot(ya_ref[...], wa_ref[...]) + _dot(yb_ref[...], wb_ref[...])
    o_ref[...] = x_ref[...] + _rmsnorm(h, g_ref[...])


def _outproj(x, y_a, y_b, w_out, g_post):
    t, d = x.shape
    ka, kb = y_a.shape[1], y_b.shape[1]
    tm = PROJ_ROW_TILE
    assert t % tm == 0 and ka == kb and w_out.shape[0] == ka + kb
    const = lambda i: (0, 0)
    return pl.pallas_call(
        _outproj_body,
        grid=(t // tm,),
        in_specs=[
            pl.BlockSpec((tm, d), lambda i: (i, 0)),
            pl.BlockSpec((tm, ka), lambda i: (i, 0)),
            pl.BlockSpec((tm, kb), lambda i: (i, 0)),
            pl.BlockSpec((ka, d), const, pipeline_mode=pl.Buffered(1)),
            pl.BlockSpec((kb, d), lambda i: (1, 0), pipeline_mode=pl.Buffered(1)),
            pl.BlockSpec((1, d), const),
        ],
        out_specs=pl.BlockSpec((tm, d), lambda i: (i, 0)),
        out_shape=jax.ShapeDtypeStruct((t, d), F32),
        compiler_params=_params(("parallel",)),
        name="outproj",
    )(x, y_a, y_b, w_out, w_out, g_post)


def _row(v):
    return v.reshape(1, -1).astype(F32)


def kernel(x, ffn1_norm_pre, ffn1_norm_post, ffn1_w_gate, ffn1_w_up, ffn1_w_down, mix_norm_pre, mix_norm_post, w_in, sgu_norm, w_spatial, b_spatial, conv_w, conv_b, dt_bias, a_log, d_skip, ssm_norm, w_out, ffn2_norm_pre, ffn2_norm_post, ffn2_w_gate, ffn2_w_up, ffn2_w_down):
    batch, seq, d = x.shape
    depth = ffn1_norm_pre.shape[0]
    a_width = sgu_norm.shape[1]
    xf = x.reshape(batch * seq, d)
    for l in range(depth):
        n_in = w_in.shape[2]
        whole = lambda w: ((0, w.shape[1], w.shape[1]),)
        side_casts = (
            (ffn2_w_gate[l], whole(ffn2_w_gate[l])),
            (ffn2_w_up[l], whole(ffn2_w_up[l])),
            (ffn2_w_down[l], whole(ffn2_w_down[l])),
            (w_in[l], ((0, 2 * a_width, 2 * a_width), (2 * a_width, n_in, ZXD_WIDTH))),
            (w_out[l], whole(w_out[l])),
        )
        xf, xn, w2_gate, w2_up, w2_down, w_uv, w_zxd, wo = _ffn(
            xf, _row(ffn1_norm_pre[l]), _row(ffn1_norm_post[l]), _row(mix_norm_pre[l]),
            ffn1_w_gate[l].astype(BF16), ffn1_w_up[l].astype(BF16), ffn1_w_down[l].astype(BF16), side_casts)

        b_full = jnp.broadcast_to(b_spatial[l][:, :, None], b_spatial[l].shape + (A_HEAD_DIM,))
        y_a = _inproj_sgu(xn, w_uv, _row(sgu_norm[l]), w_spatial[l], b_full)

        pad_heads = lambda v: jnp.pad(v.astype(F32), (0, LANES - B_HEADS)).reshape(1, LANES)
        y_b = _inproj_ssd(xn, w_zxd, batch, conv_w[l], _row(conv_b[l]), pad_heads(dt_bias[l]),
                          pad_heads(a_log[l]), _row(jnp.repeat(d_skip[l], B_HEAD_DIM)), _row(ssm_norm[l]))

        xf = _outproj(xf, y_a, y_b, wo, _row(mix_norm_post[l]))

        (xf,) = _ffn(xf, _row(ffn2_norm_pre[l]), _row(ffn2_norm_post[l]), None, w2_gate, w2_up, w2_down)
    return xf.reshape(batch, seq, d)
```

```python
import functools

import jax
import jax.numpy as jnp
from jax import lax
from jax.experimental import pallas as pl
from jax.experimental.pallas import tpu as pltpu

F32 = jnp.float32
BF16 = jnp.bfloat16

EPS = 1e-6
HALF_STEP = 0.5
SQRT_HALF = 0.7071067811865476

A_HEAD_DIM = 128
A_CHUNK = 128
B_HEAD_DIM = 64
B_HEADS = 16
B_GROUPS = 2
B_STATE = 128
B_CONV = 4
B_CHUNK = 256
B_WIDTH = B_HEADS * B_HEAD_DIM
B_GROUP_WIDTH = B_WIDTH // B_GROUPS
B_CONV_DIM = B_WIDTH + 2 * B_GROUPS * B_STATE

LANES = 128
SUBLANES = 8
BF16_SUBLANES = 16
VMEM_LIMIT_BYTES = 60 * 1024 * 1024

FFN_ROW_TILE = 1024
FFN_FF_TILE = 512
FFN_FF_SUB = 256
PROJ_ROW_TILE = 1024
SPLIT_PIECES = 2
PROJ_PIECE = 256


def _rmsnorm(x, g):
    return x * lax.rsqrt(jnp.mean(x * x, axis=-1, keepdims=True) + EPS) * g


def _gelu(x):
    return 0.5 * x * (1.0 + lax.erf(x * SQRT_HALF))


def _silu(x):
    return x * jax.nn.sigmoid(x)


def _dot(a, b):
    return jnp.dot(a, b, preferred_element_type=F32)


def _split3(v):
    hi = v.astype(BF16)
    r1 = v - hi.astype(F32)
    mid = r1.astype(BF16)
    lo = (r1 - mid.astype(F32)).astype(BF16)
    return hi, mid, lo


def _params(semantics):
    return pltpu.CompilerParams(dimension_semantics=semantics, vmem_limit_bytes=VMEM_LIMIT_BYTES)


def _slab_rows(rows, max_slabs):
    for slab in range(BF16_SUBLANES, rows + 1, BF16_SUBLANES):
        if rows % slab == 0 and rows // slab <= max_slabs:
            return slab
    raise ValueError(f"no slab size for {rows} rows in {max_slabs} steps")


def _side_cast_specs(side_casts, flat_step, total_steps):
    inputs, in_specs, out_shapes, out_specs, col_ranges = [], [], [], [], []
    for src, ranges in side_casts:
        slab_rows = _slab_rows(src.shape[0], total_steps)
        n_slabs = src.shape[0] // slab_rows
        slab = lambda *idx, n_slabs=n_slabs: (jnp.minimum(flat_step(*idx), n_slabs - 1), 0)
        inputs.append(src)
        in_specs.append(pl.BlockSpec((slab_rows, src.shape[1]), slab))
        for _, _, width in ranges:
            out_shapes.append(jax.ShapeDtypeStruct((src.shape[0], width), BF16))
            out_specs.append(pl.BlockSpec((slab_rows, width), slab))
        col_ranges.append(tuple((c0, c1) for c0, c1, _ in ranges))
    return inputs, in_specs, out_shapes, out_specs, tuple(col_ranges)


def _run_side_casts(src_refs, dst_refs, col_ranges):
    dst_refs = list(dst_refs)
    for src_ref, ranges in zip(src_refs, col_ranges):
        for c0, c1 in ranges:
            dst_ref = dst_refs.pop(0)
            dst_ref[:, :c1 - c0] = src_ref[:, c0:c1].astype(BF16)
            if dst_ref.shape[1] > c1 - c0:
                dst_ref[:, c1 - c0:] = jnp.zeros((dst_ref.shape[0], dst_ref.shape[1] - (c1 - c0)), BF16)


def _ffn_body(*refs, n_steps, tail_width, ff_sub, casts):
    x_ref, gpre_ref, gpost_ref, wg_ref, wu_ref, wd_ref = refs[:6]
    cast_srcs = refs[6:6 + len(casts)]
    o_ref = refs[6 + len(casts)]
    cast_dsts = refs[7 + len(casts):-1]
    xn_ref = refs[-1]
    j = pl.program_id(1)
    tf = wg_ref.shape[1]

    def partial_out(width):
        xn = xn_ref[...]
        acts = []
        for start in range(0, width, ff_sub):
            cols = slice(start, min(start + ff_sub, width))
            acts.append((_silu(_dot(xn, wg_ref[:, cols])) * _dot(xn, wu_ref[:, cols])).astype(BF16))
            if start == 0:
                _run_side_casts(cast_srcs, cast_dsts, casts)
        return _dot(jnp.concatenate(acts, axis=1), wd_ref[:width, :])

    @pl.when(j == 0)
    def _():
        xn_ref[...] = _rmsnorm(x_ref[...], gpre_ref[...]).astype(BF16)
        o_ref[...] = partial_out(tf)

    @pl.when(jnp.logical_and(j > 0, j < n_steps - 1))
    def _():
        o_ref[...] += partial_out(tf)

    @pl.when(j == n_steps - 1)
    def _():
        h = o_ref[...] + partial_out(tail_width)
        o_ref[...] = x_ref[...] + HALF_STEP * _rmsnorm(h, gpost_ref[...])


def _ffn(x, g_pre, g_post, wg, wu, wd, side_casts=()):
    t, d = x.shape
    f = wg.shape[1]
    tm, tf, ff_sub = FFN_ROW_TILE, FFN_FF_TILE, FFN_FF_SUB
    n_steps = pl.cdiv(f, tf)
    tail_width = f - (n_steps - 1) * tf
    assert t % tm == 0 and n_steps >= 2 and tail_width % LANES == 0
    row = pl.BlockSpec((tm, d), lambda i, j: (i, 0))
    vec = pl.BlockSpec((1, d), lambda i, j: (0, 0))
    cast_in, cast_in_specs, cast_shapes, cast_out_specs, col_ranges = _side_cast_specs(
        side_casts, lambda i, j: i * n_steps + j, (t // tm) * n_steps)
    return pl.pallas_call(
        functools.partial(_ffn_body, n_steps=n_steps, tail_width=tail_width, ff_sub=ff_sub, casts=col_ranges),
        grid=(t // tm, n_steps),
        in_specs=[row, vec, vec,
                  pl.BlockSpec((d, tf), lambda i, j: (0, j)),
                  pl.BlockSpec((d, tf), lambda i, j: (0, j)),
                  pl.BlockSpec((tf, d), lambda i, j: (j, 0))] + cast_in_specs,
        out_specs=[row] + cast_out_specs,
        out_shape=[jax.ShapeDtypeStruct((t, d), F32)] + cast_shapes,
        scratch_shapes=[pltpu.VMEM((tm, d), BF16)],
        compiler_params=_params(("arbitrary", "arbitrary") if side_casts else ("parallel", "arbitrary")),
        name="ffn",
    )(x, g_pre, g_post, wg, wu, wd, *cast_in)


def _sgu_body(x_ref, gpre_ref, wu_ref, wv_ref, gs_ref, ws_ref, bs_ref, o_ref, u_ref, v_ref):
    tm = x_ref.shape[0]
    heads = ws_ref.shape[0]
    xn = _rmsnorm(x_ref[...], gpre_ref[...]).astype(BF16)
    u_ref[...] = _gelu(_dot(xn, wu_ref[...]))
    v_ref[...] = _rmsnorm(_gelu(_dot(xn, wv_ref[...])), gs_ref[...]).astype(BF16)
    rows = lax.broadcasted_iota(jnp.int32, (A_CHUNK, A_CHUNK), 0)
    cols = lax.broadcasted_iota(jnp.int32, (A_CHUNK, A_CHUNK), 1)
    causal = rows >= cols
    for h in range(heads):
        w = jnp.where(causal, ws_ref[h], 0.0).astype(BF16)
        bias = bs_ref[h]
        lanes = pl.ds(h * A_HEAD_DIM, A_HEAD_DIM)
        for c in range(tm // A_CHUNK):
            rws = pl.ds(c * A_CHUNK, A_CHUNK)
            mixed = _dot(w, v_ref[rws, lanes]) + bias
            o_ref[rws, lanes] = (u_ref[rws, lanes] * mixed).astype(BF16)


def _inproj_sgu(x, g_pre, w_uv, g_sgu, w_spatial, b_full):
    t, d = x.shape
    aw = w_uv.shape[1] // 2
    heads = w_spatial.shape[0]
    tm = PROJ_ROW_TILE
    assert t % tm == 0 and tm % A_CHUNK == 0 and aw == heads * A_HEAD_DIM
    const2 = lambda i: (0, 0)
    const3 = lambda i: (0, 0, 0)
    return pl.pallas_call(
        _sgu_body,
        grid=(t // tm,),
        in_specs=[
            pl.BlockSpec((tm, d), lambda i: (i, 0)),
            pl.BlockSpec((1, d), const2),
            pl.BlockSpec((d, aw), const2, pipeline_mode=pl.Buffered(1)),
            pl.BlockSpec((d, aw), lambda i: (0, 1), pipeline_mode=pl.Buffered(1)),
            pl.BlockSpec((1, aw), const2),
            pl.BlockSpec((heads, A_CHUNK, A_CHUNK), const3),
            pl.BlockSpec((heads, A_CHUNK, A_HEAD_DIM), const3),
        ],
        out_specs=pl.BlockSpec((tm, aw), lambda i: (i, 0)),
        out_shape=jax.ShapeDtypeStruct((t, aw), BF16),
        scratch_shapes=[pltpu.VMEM((tm, aw), F32), pltpu.VMEM((tm, aw), BF16)],
        compiler_params=_params(("parallel",)),
        name="inproj_sgu",
    )(x, g_pre, w_uv, w_uv, g_sgu, w_spatial, b_full)


Z_OFF = 0
XBC_OFF = B_WIDTH
DT_OFF = B_WIDTH + B_CONV_DIM
ZXD_WIDTH = DT_OFF + LANES
CONV_HALO = SUBLANES


def _split_cols(v, pieces):
    parts = []
    rest = v
    for _ in range(pieces):
        part = rest.astype(BF16)
        parts.append(part)
        rest = rest - part.astype(F32)
    return jnp.concatenate(parts, axis=1)


def _ssd_mix(zxd_ref, convw_ref, convb_ref, dtb_ref, alog_ref, dskip_ref, norm_ref, o_ref,
             ext_ref, state_ref, y_ref, side_work=()):
    cl = B_CHUNK
    side_work = list(side_work)

    def side(n=1):
        for _ in range(n):
            if side_work:
                side_work.pop(0)()

    ext_ref[CONV_HALO:CONV_HALO + cl, :] = zxd_ref[:, XBC_OFF:XBC_OFF + B_CONV_DIM]
    ext = ext_ref[...]
    conv = convw_ref[0:1, :] * ext
    for k in range(1, B_CONV):
        conv = pltpu.roll(conv, 1, axis=0) + convw_ref[k:k + 1, :] * ext
    conv = conv[CONV_HALO:, :] + convb_ref[...]
    ext_ref[0:CONV_HALO, :] = ext_ref[cl:cl + CONV_HALO, :]
    side()
    xbc = _silu(conv)
    xs = xbc[:, :B_WIDTH]

    dt = jax.nn.softplus(zxd_ref[:, DT_OFF:DT_OFF + LANES] + dtb_ref[...])
    da = dt * (-jnp.exp(alog_ref[...]))
    rows = lax.broadcasted_iota(jnp.int32, (cl, cl), 0)
    cols = lax.broadcasted_iota(jnp.int32, (cl, cl), 1)
    causal = rows >= cols
    tril = jnp.where(causal, 1.0, 0.0).astype(BF16)
    acs = sum(_dot(tril, part) for part in _split3(da))
    acs_t = acs.T
    side()

    e_rows = lax.broadcasted_iota(jnp.int32, (SPLIT_PIECES * LANES, B_WIDTH), 0)
    e_cols = lax.broadcasted_iota(jnp.int32, (SPLIT_PIECES * LANES, B_WIDTH), 1)
    expand = jnp.where(e_cols // B_HEAD_DIM == e_rows % LANES, 1.0, 0.0).astype(BF16)
    acs_e = _dot(_split_cols(acs, SPLIT_PIECES), expand)
    dt_e = _dot(_split_cols(dt, SPLIT_PIECES), expand)

    x = xs * dt_e
    even_head = (lax.broadcasted_iota(jnp.int32, (cl, B_WIDTH), 1) // B_HEAD_DIM) % 2 == 0
    x_even = jnp.where(even_head, x, 0.0).astype(BF16)
    x_odd = jnp.where(even_head, 0.0, x).astype(BF16)
    acs_last = acs_e[cl - 1:cl, :]
    decay_from_start = jnp.exp(acs_e)
    x_to_end = (x * jnp.exp(acs_last - acs_e)).astype(BF16)
    chunk_decay = jnp.exp(acs_last)
    side()

    heads_per_group = B_HEADS // B_GROUPS
    for g in range(B_GROUPS):
        gl = slice(g * B_GROUP_WIDTH, (g + 1) * B_GROUP_WIDTH)
        b_off = B_WIDTH + g * B_STATE
        c_off = B_WIDTH + B_GROUPS * B_STATE + g * B_STATE
        bc_t = xbc[:, b_off:b_off + B_STATE].T.astype(BF16)
        cc = xbc[:, c_off:c_off + B_STATE].astype(BF16)
        cb = _dot(cc, bc_t)
        state = state_ref[g]
        y_off = _dot(cc, state.astype(BF16)) * decay_from_start[:, gl]
        state_ref[g] = state * chunk_decay[:, gl] + _dot(bc_t, x_to_end[:, gl])
        for pair in range(heads_per_group // 2):
            head_a = g * heads_per_group + 2 * pair
            pl_off = head_a * B_HEAD_DIM
            y_pair = y_off[:, pair * LANES:(pair + 1) * LANES]
            for head, x_half in ((head_a, x_even), (head_a + 1, x_odd)):
                diff = acs[:, head:head + 1] - acs_t[head:head + 1, :]
                m = (cb * jnp.exp(jnp.where(causal, diff, -jnp.inf))).astype(BF16)
                y_pair = y_pair + _dot(m, x_half[:, pl_off:pl_off + LANES])
            y_ref[:, pl_off:pl_off + LANES] = y_pair
            side()

    side(len(side_work))
    y = (y_ref[...] + dskip_ref[...] * xs) * _silu(zxd_ref[:, Z_OFF:Z_OFF + B_WIDTH])
    for g in range(B_GROUPS):
        gl = slice(g * B_GROUP_WIDTH, (g + 1) * B_GROUP_WIDTH)
        yg = y[:, gl]
        yg = yg * lax.rsqrt(jnp.mean(yg * yg, axis=-1, keepdims=True) + EPS)
        o_ref[:, gl] = (yg * norm_ref[:, gl]).astype(BF16)


def _ssd_body(*refs, nc, casts):
    x_ref, gpre_ref, w_ref, convw_ref, convb_ref, dtb_ref, alog_ref, dskip_ref, norm_ref = refs[:9]
    cast_srcs = refs[9:9 + len(casts)]
    o_ref = refs[9 + len(casts)]
    cast_dsts = refs[10 + len(casts):-6]
    zxd0_ref, zxd1_ref, xn_ref, ext_ref, state_ref, y_ref = refs[-6:]
    s = pl.program_id(1)
    bufs = (zxd0_ref, zxd1_ref)

    def project_pieces(dst_ref):
        def prenorm():
            xn_ref[...] = _rmsnorm(x_ref[...], gpre_ref[...]).astype(BF16)

        def piece(start):
            cols = slice(start, min(start + PROJ_PIECE, ZXD_WIDTH))

            def run():
                dst_ref[:, cols] = _dot(xn_ref[...], w_ref[:, cols])
            return run

        def casts_then(run):
            def both():
                _run_side_casts(cast_srcs, cast_dsts, casts)
                run()
            return both
        pieces = [piece(start) for start in range(0, ZXD_WIDTH, PROJ_PIECE)]
        return [prenorm, casts_then(pieces[0])] + pieces[1:]

    def project(dst_ref):
        for run in project_pieces(dst_ref):
            run()

    def mix(src_ref, side_work=()):
        _ssd_mix(src_ref, convw_ref, convb_ref, dtb_ref, alog_ref, dskip_ref, norm_ref, o_ref,
                 ext_ref, state_ref, y_ref, side_work)

    @pl.when(s == 0)
    def _():
        ext_ref[0:CONV_HALO, :] = jnp.zeros((CONV_HALO, B_CONV_DIM), F32)
        state_ref[...] = jnp.zeros_like(state_ref)
        project(bufs[0])

    for parity in range(2):
        @pl.when(jnp.logical_and(jnp.logical_and(s > 0, s < nc), lax.rem(s, 2) == parity))
        def _():
            mix(bufs[1 - parity], project_pieces(bufs[parity]))

    @pl.when(s == nc)
    def _():
        _run_side_casts(cast_srcs, cast_dsts, casts)
        mix(bufs[(nc - 1) % 2])


def _inproj_ssd(x, g_pre, w_zxd, batch, conv_w, conv_b, dt_bias, a_log, d_skip, ssm_norm, side_casts=()):
    t, d = x.shape
    seq = t // batch
    assert seq % B_CHUNK == 0 and w_zxd.shape[1] == ZXD_WIDTH
    nc = seq // B_CHUNK
    const = lambda b, s: (0, 0)
    cast_in, cast_in_specs, cast_shapes, cast_out_specs, col_ranges = _side_cast_specs(
        side_casts, lambda b, s: b * (nc + 1) + s, batch * (nc + 1))
    return pl.pallas_call(
        functools.partial(_ssd_body, nc=nc, casts=col_ranges),
        grid=(batch, nc + 1),
        in_specs=[
            pl.BlockSpec((B_CHUNK, d), lambda b, s: (b * nc + jnp.minimum(s, nc - 1), 0)),
            pl.BlockSpec((1, d), const),
            pl.BlockSpec((d, ZXD_WIDTH), const),
            pl.BlockSpec((B_CONV, B_CONV_DIM), const),
            pl.BlockSpec((1, B_CONV_DIM), const),
            pl.BlockSpec((1, LANES), const),
            pl.BlockSpec((1, LANES), const),
            pl.BlockSpec((1, B_WIDTH), const),
            pl.BlockSpec((1, B_WIDTH), const),
        ] + cast_in_specs,
        out_specs=[pl.BlockSpec((B_CHUNK, B_WIDTH), lambda b, s: (b * nc + jnp.maximum(s - 1, 0), 0))]
        + cast_out_specs,
        out_shape=[jax.ShapeDtypeStruct((t, B_WIDTH), BF16)] + cast_shapes,
        scratch_shapes=[
            pltpu.VMEM((B_CHUNK, ZXD_WIDTH), F32),
            pltpu.VMEM((B_CHUNK, ZXD_WIDTH), F32),
            pltpu.VMEM((B_CHUNK, d), BF16),
            pltpu.VMEM((CONV_HALO + B_CHUNK, B_CONV_DIM), F32),
            pltpu.VMEM((B_GROUPS, B_STATE, B_GROUP_WIDTH), F32),
            pltpu.VMEM((B_CHUNK, B_WIDTH), F32),
        ],
        compiler_params=_params(("arbitrary", "arbitrary")),
        name="inproj_ssd",
    )(x, g_pre, w_zxd, conv_w, conv_b, dt_bias, a_log, d_skip, ssm_norm, *cast_in)


def _outproj_body(x_ref, ya_ref, yb_ref, wa_ref, wb_ref, g_ref, o_ref):
    h = _dot(ya_ref[...], wa_ref[...]) + _dot(yb_ref[...], wb_ref[...])
    o_ref[...] = x_ref[...] + _rmsnorm(h, g_ref[...])


def _outproj(x, y_a, y_b, w_out, g_post):
    t, d = x.shape
    ka, kb = y_a.shape[1], y_b.shape[1]
    tm = PROJ_ROW_TILE
    assert t % tm == 0 and ka == kb and w_out.shape[0] == ka + kb
    const = lambda i: (0, 0)
    return pl.pallas_call(
        _outproj_body,
        grid=(t // tm,),
        in_specs=[
            pl.BlockSpec((tm, d), lambda i: (i, 0)),
            pl.BlockSpec((tm, ka), lambda i: (i, 0)),
            pl.BlockSpec((tm, kb), lambda i: (i, 0)),
            pl.BlockSpec((ka, d), const, pipeline_mode=pl.Buffered(1)),
            pl.BlockSpec((kb, d), lambda i: (1, 0), pipeline_mode=pl.Buffered(1)),
            pl.BlockSpec((1, d), const),
        ],
        out_specs=pl.BlockSpec((tm, d), lambda i: (i, 0)),
        out_shape=jax.ShapeDtypeStruct((t, d), F32),
        compiler_params=_params(("parallel",)),
        name="outproj",
    )(x, y_a, y_b, w_out, w_out, g_post)


def _row(v):
    return v.reshape(1, -1).astype(F32)


def kernel(x, ffn1_norm_pre, ffn1_norm_post, ffn1_w_gate, ffn1_w_up, ffn1_w_down, mix_norm_pre, mix_norm_post, w_in, sgu_norm, w_spatial, b_spatial, conv_w, conv_b, dt_bias, a_log, d_skip, ssm_norm, w_out, ffn2_norm_pre, ffn2_norm_post, ffn2_w_gate, ffn2_w_up, ffn2_w_down):
    batch, seq, d = x.shape
    depth = ffn1_norm_pre.shape[0]
    a_width = sgu_norm.shape[1]
    xf = x.reshape(batch * seq, d)
    for l in range(depth):
        n_in = w_in.shape[2]
        whole = lambda w: (w, ((0, w.shape[1], w.shape[1]),))
        xf, w_uv, w_zxd = _ffn(
            xf, _row(ffn1_norm_pre[l]), _row(ffn1_norm_post[l]),
            ffn1_w_gate[l].astype(BF16), ffn1_w_up[l].astype(BF16), ffn1_w_down[l].astype(BF16),
            ((w_in[l], ((0, 2 * a_width, 2 * a_width), (2 * a_width, n_in, ZXD_WIDTH))),))

        g_mix = _row(mix_norm_pre[l])
        b_full = jnp.broadcast_to(b_spatial[l][:, :, None], b_spatial[l].shape + (A_HEAD_DIM,))
        y_a = _inproj_sgu(xf, g_mix, w_uv, _row(sgu_norm[l]), w_spatial[l], b_full)

        pad_heads = lambda v: jnp.pad(v.astype(F32), (0, LANES - B_HEADS)).reshape(1, LANES)
        y_b, w2_gate, w2_up, w2_down, wo = _inproj_ssd(
            xf, g_mix, w_zxd, batch, conv_w[l], _row(conv_b[l]), pad_heads(dt_bias[l]), pad_heads(a_log[l]),
            _row(jnp.repeat(d_skip[l], B_HEAD_DIM)), _row(ssm_norm[l]),
            (whole(ffn2_w_gate[l]), whole(ffn2_w_up[l]), whole(ffn2_w_down[l]), whole(w_out[l])))

        xf = _outproj(xf, y_a, y_b, wo, _row(mix_norm_post[l]))

        (xf,) = _ffn(xf, _row(ffn2_norm_pre[l]), _row(ffn2_norm_post[l]), w2_gate, w2_up, w2_down)
    return xf.reshape(batch, seq, d)
```

```python
import functools

import jax
import jax.numpy as jnp
from jax import lax
from jax.experimental import pallas as pl
from jax.experimental.pallas import tpu as pltpu

F32 = jnp.float32
BF16 = jnp.bfloat16

EPS = 1e-6
HALF_STEP = 0.5
SQRT_HALF = 0.7071067811865476

A_HEAD_DIM = 128
A_CHUNK = 128
B_HEAD_DIM = 64
B_HEADS = 16
B_GROUPS = 2
B_STATE = 128
B_CONV = 4
B_CHUNK = 256
B_WIDTH = B_HEADS * B_HEAD_DIM
B_GROUP_WIDTH = B_WIDTH // B_GROUPS
B_CONV_DIM = B_WIDTH + 2 * B_GROUPS * B_STATE

LANES = 128
SUBLANES = 8
BF16_SUBLANES = 16
VMEM_LIMIT_BYTES = 60 * 1024 * 1024

FFN_ROW_TILE = 1024
FFN_FF_TILE = 512
FFN_FF_SUB = 256
FFN_FINISH_ROWS = 256
PROJ_ROW_TILE = 1024
SPLIT_PIECES = 2
PROJ_PIECE = 256


def _rmsnorm(x, g):
    return x * lax.rsqrt(jnp.mean(x * x, axis=-1, keepdims=True) + EPS) * g


def _gelu(x):
    return 0.5 * x * (1.0 + lax.erf(x * SQRT_HALF))


def _silu(x):
    return x * jax.nn.sigmoid(x)


def _dot(a, b):
    return jnp.dot(a, b, preferred_element_type=F32)


def _dot_nt(a, b_t):
    return lax.dot_general(a, b_t, (((1,), (1,)), ((), ())), preferred_element_type=F32)


def _split3(v):
    hi = v.astype(BF16)
    r1 = v - hi.astype(F32)
    mid = r1.astype(BF16)
    lo = (r1 - mid.astype(F32)).astype(BF16)
    return hi, mid, lo


def _params(semantics):
    return pltpu.CompilerParams(dimension_semantics=semantics, vmem_limit_bytes=VMEM_LIMIT_BYTES)


def _slab_rows(rows, max_slabs, align):
    for slab in range(align, rows + 1, align):
        if rows % slab == 0 and rows // slab <= max_slabs:
            return slab
    raise ValueError(f"no slab size for {rows} rows in {max_slabs} steps")


def _side_cast_specs(side_casts, flat_step, total_steps):
    inputs, in_specs, out_shapes, out_specs, plans = [], [], [], [], []
    for src, layer, row_start, n_rows, out_rows, transpose in side_casts:
        n_cols = src.shape[2]
        slab_rows = _slab_rows(out_rows, total_steps, LANES if transpose else BF16_SUBLANES)
        assert row_start % slab_rows == 0
        n_real, n_out, first = pl.cdiv(n_rows, slab_rows), out_rows // slab_rows, row_start // slab_rows
        src_slab = lambda *idx, n=n_real, first=first, layer=layer: (
            layer, first + jnp.minimum(flat_step(*idx), n - 1), 0)
        dst_slab = lambda *idx, n=n_out: jnp.minimum(flat_step(*idx), n - 1)
        inputs.append(src)
        in_specs.append(pl.BlockSpec((None, slab_rows, n_cols), src_slab))
        if transpose:
            out_shapes.append(jax.ShapeDtypeStruct((n_cols, out_rows), BF16))
            out_specs.append(pl.BlockSpec((n_cols, slab_rows), lambda *idx, f=dst_slab: (0, f(*idx))))
        else:
            out_shapes.append(jax.ShapeDtypeStruct((out_rows, n_cols), BF16))
            out_specs.append(pl.BlockSpec((slab_rows, n_cols), lambda *idx, f=dst_slab: (f(*idx), 0)))
        plans.append((n_rows, n_out, transpose))
    return inputs, in_specs, out_shapes, out_specs, tuple(plans)


def _run_side_casts(step, src_refs, dst_refs, plans):
    for src_ref, dst_ref, (n_rows, n_slabs, transpose) in zip(src_refs, dst_refs, plans):
        slab = src_ref[...]
        if n_rows != n_slabs * src_ref.shape[0]:
            row = jnp.minimum(step, n_slabs - 1) * src_ref.shape[0] + lax.broadcasted_iota(jnp.int32, slab.shape, 0)
            slab = jnp.where(row < n_rows, slab, 0.0)
        dst_ref[...] = (slab.T if transpose else slab).astype(BF16)


def _ffn_body(*refs, n_steps, tail_width, ff_sub, casts):
    x_ref, gpre_ref, gpost_ref, wg_ref, wu_ref, wd_ref = refs[:6]
    cast_srcs = refs[6:6 + len(casts)]
    o_ref = refs[6 + len(casts)]
    cast_dsts = refs[7 + len(casts):-1]
    xn_ref = refs[-1]
    j = pl.program_id(1)
    tf = wg_ref.shape[1]

    def partial_out(width, rows=slice(None)):
        xn = xn_ref[rows, :]
        acts = []
        for start in range(0, width, ff_sub):
            cols = slice(start, min(start + ff_sub, width))
            acts.append((_silu(_dot(xn, wg_ref[:, cols])) * _dot(xn, wu_ref[:, cols])).astype(BF16))
            if start == 0 and rows.start in (None, 0):
                _run_side_casts(pl.program_id(0) * n_steps + j, cast_srcs, cast_dsts, casts)
        return _dot(jnp.concatenate(acts, axis=1), wd_ref[:width, :])

    @pl.when(j == 0)
    def _():
        xn_ref[...] = _rmsnorm(x_ref[...], gpre_ref[...]).astype(BF16)
        o_ref[...] = partial_out(tf)

    @pl.when(jnp.logical_and(j > 0, j < n_steps - 1))
    def _():
        o_ref[...] += partial_out(tf)

    @pl.when(j == n_steps - 1)
    def _():
        for start in range(0, x_ref.shape[0], FFN_FINISH_ROWS):
            rows = slice(start, start + FFN_FINISH_ROWS)
            h = o_ref[rows, :] + partial_out(tail_width, rows)
            o_ref[rows, :] = x_ref[rows, :] + HALF_STEP * _rmsnorm(h, gpost_ref[...])


def _ffn(x, g_pre, g_post, wg, wu, wd, side_casts=()):
    t, d = x.shape
    f = wg.shape[1]
    tm, tf, ff_sub = FFN_ROW_TILE, FFN_FF_TILE, FFN_FF_SUB
    n_steps = pl.cdiv(f, tf)
    tail_width = f - (n_steps - 1) * tf
    assert t % tm == 0 and n_steps >= 2 and tail_width % LANES == 0
    row = pl.BlockSpec((tm, d), lambda i, j: (i, 0))
    vec = pl.BlockSpec((1, d), lambda i, j: (0, 0))
    cast_in, cast_in_specs, cast_shapes, cast_out_specs, cast_plans = _side_cast_specs(
        side_casts, lambda i, j: i * n_steps + j, (t // tm) * n_steps)
    return pl.pallas_call(
        functools.partial(_ffn_body, n_steps=n_steps, tail_width=tail_width, ff_sub=ff_sub, casts=cast_plans),
        grid=(t // tm, n_steps),
        in_specs=[row, vec, vec,
                  pl.BlockSpec((d, tf), lambda i, j: (0, j)),
                  pl.BlockSpec((d, tf), lambda i, j: (0, j)),
                  pl.BlockSpec((tf, d), lambda i, j: (j, 0))] + cast_in_specs,
        out_specs=[row] + cast_out_specs,
        out_shape=[jax.ShapeDtypeStruct((t, d), F32)] + cast_shapes,
        scratch_shapes=[pltpu.VMEM((tm, d), BF16)],
        compiler_params=_params(("arbitrary", "arbitrary") if side_casts else ("parallel", "arbitrary")),
        name="ffn",
    )(x, g_pre, g_post, wg, wu, wd, *cast_in)


def _sgu_body(x_ref, gpre_ref, wu_ref, wv_ref, gs_ref, ws_ref, bs_ref, o_ref, xn_ref, u_ref, v_ref):
    tm = x_ref.shape[0]
    heads = ws_ref.shape[0]
    xn = _rmsnorm(x_ref[...], gpre_ref[...]).astype(BF16)
    xn_ref[...] = xn
    u_ref[...] = _gelu(_dot_nt(xn, wu_ref[...]))
    v_ref[...] = _rmsnorm(_gelu(_dot_nt(xn, wv_ref[...])), gs_ref[...]).astype(BF16)
    rows = lax.broadcasted_iota(jnp.int32, (A_CHUNK, A_CHUNK), 0)
    cols = lax.broadcasted_iota(jnp.int32, (A_CHUNK, A_CHUNK), 1)
    causal = rows >= cols
    for h in range(heads):
        w = jnp.where(causal, ws_ref[h], 0.0).astype(BF16)
        bias = bs_ref[h]
        lanes = pl.ds(h * A_HEAD_DIM, A_HEAD_DIM)
        for c in range(tm // A_CHUNK):
            rws = pl.ds(c * A_CHUNK, A_CHUNK)
            mixed = _dot(w, v_ref[rws, lanes]) + bias
            o_ref[rws, lanes] = (u_ref[rws, lanes] * mixed).astype(BF16)


def _inproj_sgu(x, g_pre, w_uv, g_sgu, w_spatial, b_full):
    t, d = x.shape
    aw = w_uv.shape[0] // 2
    heads = w_spatial.shape[0]
    tm = PROJ_ROW_TILE
    assert t % tm == 0 and tm % A_CHUNK == 0 and aw == heads * A_HEAD_DIM
    const2 = lambda i: (0, 0)
    const3 = lambda i: (0, 0, 0)
    return pl.pallas_call(
        _sgu_body,
        grid=(t // tm,),
        in_specs=[
            pl.BlockSpec((tm, d), lambda i: (i, 0)),
            pl.BlockSpec((1, d), const2),
            pl.BlockSpec((aw, d), const2, pipeline_mode=pl.Buffered(1)),
            pl.BlockSpec((aw, d), lambda i: (1, 0), pipeline_mode=pl.Buffered(1)),
            pl.BlockSpec((1, aw), const2),
            pl.BlockSpec((heads, A_CHUNK, A_CHUNK), const3),
            pl.BlockSpec((heads, A_CHUNK, A_HEAD_DIM), const3),
        ],
        out_specs=[pl.BlockSpec((tm, aw), lambda i: (i, 0)), pl.BlockSpec((tm, d), lambda i: (i, 0))],
        out_shape=[jax.ShapeDtypeStruct((t, aw), BF16), jax.ShapeDtypeStruct((t, d), BF16)],
        scratch_shapes=[pltpu.VMEM((tm, aw), F32), pltpu.VMEM((tm, aw), BF16)],
        compiler_params=_params(("parallel",)),
        name="inproj_sgu",
    )(x, g_pre, w_uv, w_uv, g_sgu, w_spatial, b_full)


Z_OFF = 0
XBC_OFF = B_WIDTH
DT_OFF = B_WIDTH + B_CONV_DIM
ZXD_WIDTH = DT_OFF + LANES
CONV_HALO = SUBLANES


def _split_cols(v, pieces):
    parts = []
    rest = v
    for _ in range(pieces):
        part = rest.astype(BF16)
        parts.append(part)
        rest = rest - part.astype(F32)
    return jnp.concatenate(parts, axis=1)


def _ssd_mix(zxd_ref, convw_ref, convb_ref, dtb_ref, alog_ref, dskip_ref, norm_ref, o_ref,
             ext_ref, state_ref, y_ref, side_work=()):
    cl = B_CHUNK
    side_work = list(side_work)

    def side(n=1):
        for _ in range(n):
            if side_work:
                side_work.pop(0)()

    ext_ref[CONV_HALO:CONV_HALO + cl, :] = zxd_ref[:, XBC_OFF:XBC_OFF + B_CONV_DIM]
    ext = ext_ref[...]
    conv = convw_ref[0:1, :] * ext
    for k in range(1, B_CONV):
        conv = pltpu.roll(conv, 1, axis=0) + convw_ref[k:k + 1, :] * ext
    conv = conv[CONV_HALO:, :] + convb_ref[...]
    ext_ref[0:CONV_HALO, :] = ext_ref[cl:cl + CONV_HALO, :]
    side()
    xbc = _silu(conv)
    xs = xbc[:, :B_WIDTH]

    dt = jax.nn.softplus(zxd_ref[:, DT_OFF:DT_OFF + LANES] + dtb_ref[...])
    da = dt * (-jnp.exp(alog_ref[...]))
    rows = lax.broadcasted_iota(jnp.int32, (cl, cl), 0)
    cols = lax.broadcasted_iota(jnp.int32, (cl, cl), 1)
    causal = rows >= cols
    tril = jnp.where(causal, 1.0, 0.0).astype(BF16)
    acs = sum(_dot(tril, part) for part in _split3(da))
    acs_t = acs.T
    side()

    e_rows = lax.broadcasted_iota(jnp.int32, (SPLIT_PIECES * LANES, B_WIDTH), 0)
    e_cols = lax.broadcasted_iota(jnp.int32, (SPLIT_PIECES * LANES, B_WIDTH), 1)
    expand = jnp.where(e_cols // B_HEAD_DIM == e_rows % LANES, 1.0, 0.0).astype(BF16)
    acs_e = _dot(_split_cols(acs, SPLIT_PIECES), expand)
    dt_e = _dot(_split_cols(dt, SPLIT_PIECES), expand)

    x = xs * dt_e
    even_head = (lax.broadcasted_iota(jnp.int32, (cl, B_WIDTH), 1) // B_HEAD_DIM) % 2 == 0
    x_even = jnp.where(even_head, x, 0.0).astype(BF16)
    x_odd = jnp.where(even_head, 0.0, x).astype(BF16)
    acs_last = acs_e[cl - 1:cl, :]
    decay_from_start = jnp.exp(acs_e)
    x_to_end = (x * jnp.exp(acs_last - acs_e)).astype(BF16)
    chunk_decay = jnp.exp(acs_last)
    side()

    heads_per_group = B_HEADS // B_GROUPS
    for g in range(B_GROUPS):
        gl = slice(g * B_GROUP_WIDTH, (g + 1) * B_GROUP_WIDTH)
        b_off = B_WIDTH + g * B_STATE
        c_off = B_WIDTH + B_GROUPS * B_STATE + g * B_STATE
        bc_t = xbc[:, b_off:b_off + B_STATE].T.astype(BF16)
        cc = xbc[:, c_off:c_off + B_STATE].astype(BF16)
        cb = _dot(cc, bc_t)
        state = state_ref[g]
        y_off = _dot(cc, state.astype(BF16)) * decay_from_start[:, gl]
        state_ref[g] = state * chunk_decay[:, gl] + _dot(bc_t, x_to_end[:, gl])
        for pair in range(heads_per_group // 2):
            head_a = g * heads_per_group + 2 * pair
            pl_off = head_a * B_HEAD_DIM
            y_pair = y_off[:, pair * LANES:(pair + 1) * LANES]
            for head, x_half in ((head_a, x_even), (head_a + 1, x_odd)):
                diff = acs[:, head:head + 1] - acs_t[head:head + 1, :]
                m = (cb * jnp.exp(jnp.where(causal, diff, -jnp.inf))).astype(BF16)
                y_pair = y_pair + _dot(m, x_half[:, pl_off:pl_off + LANES])
            y_ref[:, pl_off:pl_off + LANES] = y_pair
            side()

    side(len(side_work))
    y = (y_ref[...] + dskip_ref[...] * xs) * _silu(zxd_ref[:, Z_OFF:Z_OFF + B_WIDTH])
    for g in range(B_GROUPS):
        gl = slice(g * B_GROUP_WIDTH, (g + 1) * B_GROUP_WIDTH)
        yg = y[:, gl]
        yg = yg * lax.rsqrt(jnp.mean(yg * yg, axis=-1, keepdims=True) + EPS)
        o_ref[:, gl] = (yg * norm_ref[:, gl]).astype(BF16)


def _ssd_body(*refs, nc, casts):
    xn_ref, w_ref, convw_ref, convb_ref, dtb_ref, alog_ref, dskip_ref, norm_ref = refs[:8]
    cast_srcs = refs[8:8 + len(casts)]
    o_ref = refs[8 + len(casts)]
    cast_dsts = refs[9 + len(casts):-5]
    zxd0_ref, zxd1_ref, ext_ref, state_ref, y_ref = refs[-5:]
    s = pl.program_id(1)
    bufs = (zxd0_ref, zxd1_ref)

    def project_pieces(dst_ref):
        def piece(start):
            cols = slice(start, min(start + PROJ_PIECE, ZXD_WIDTH))

            def run():
                dst_ref[:, cols] = _dot(xn_ref[...], w_ref[:, cols])
            return run

        def casts_then(run):
            def both():
                _run_side_casts(pl.program_id(0) * (nc + 1) + s, cast_srcs, cast_dsts, casts)
                run()
            return both
        pieces = [piece(start) for start in range(0, ZXD_WIDTH, PROJ_PIECE)]
        return [casts_then(pieces[0])] + pieces[1:]

    def project(dst_ref):
        for run in project_pieces(dst_ref):
            run()

    def mix(src_ref, side_work=()):
        _ssd_mix(src_ref, convw_ref, convb_ref, dtb_ref, alog_ref, dskip_ref, norm_ref, o_ref,
                 ext_ref, state_ref, y_ref, side_work)

    @pl.when(s == 0)
    def _():
        ext_ref[0:CONV_HALO, :] = jnp.zeros((CONV_HALO, B_CONV_DIM), F32)
        state_ref[...] = jnp.zeros_like(state_ref)
        project(bufs[0])

    for parity in range(2):
        @pl.when(jnp.logical_and(jnp.logical_and(s > 0, s < nc), lax.rem(s, 2) == parity))
        def _():
            mix(bufs[1 - parity], project_pieces(bufs[parity]))

    @pl.when(s == nc)
    def _():
        _run_side_casts(pl.program_id(0) * (nc + 1) + s, cast_srcs, cast_dsts, casts)
        mix(bufs[(nc - 1) % 2])


def _inproj_ssd(xn, w_zxd, batch, conv_w, conv_b, dt_bias, a_log, d_skip, ssm_norm, side_casts=()):
    t, d = xn.shape
    seq = t // batch
    assert seq % B_CHUNK == 0 and w_zxd.shape[1] == ZXD_WIDTH
    nc = seq // B_CHUNK
    const = lambda b, s: (0, 0)
    cast_in, cast_in_specs, cast_shapes, cast_out_specs, cast_plans = _side_cast_specs(
        side_casts, lambda b, s: b * (nc + 1) + s, batch * (nc + 1))
    return pl.pallas_call(
        functools.partial(_ssd_body, nc=nc, casts=cast_plans),
        grid=(batch, nc + 1),
        in_specs=[
            pl.BlockSpec((B_CHUNK, d), lambda b, s: (b * nc + jnp.minimum(s, nc - 1), 0)),
            pl.BlockSpec((d, ZXD_WIDTH), const),
            pl.BlockSpec((B_CONV, B_CONV_DIM), const),
            pl.BlockSpec((1, B_CONV_DIM), const),
            pl.BlockSpec((1, LANES), const),
            pl.BlockSpec((1, LANES), const),
            pl.BlockSpec((1, B_WIDTH), const),
            pl.BlockSpec((1, B_WIDTH), const),
        ] + cast_in_specs,
        out_specs=[pl.BlockSpec((B_CHUNK, B_WIDTH), lambda b, s: (b * nc + jnp.maximum(s - 1, 0), 0))]
        + cast_out_specs,
        out_shape=[jax.ShapeDtypeStruct((t, B_WIDTH), BF16)] + cast_shapes,
        scratch_shapes=[
            pltpu.VMEM((B_CHUNK, ZXD_WIDTH), F32),
            pltpu.VMEM((B_CHUNK, ZXD_WIDTH), F32),
            pltpu.VMEM((CONV_HALO + B_CHUNK, B_CONV_DIM), F32),
            pltpu.VMEM((B_GROUPS, B_STATE, B_GROUP_WIDTH), F32),
            pltpu.VMEM((B_CHUNK, B_WIDTH), F32),
        ],
        compiler_params=_params(("arbitrary", "arbitrary")),
        name="inproj_ssd",
    )(xn, w_zxd, conv_w, conv_b, dt_bias, a_log, d_skip, ssm_norm, *cast_in)


def _outproj_body(x_ref, ya_ref, yb_ref, wa_ref, wb_ref, g_ref, o_ref):
    h = _dot(ya_ref[...], wa_ref[...]) + _dot(yb_ref[...], wb_ref[...])
    o_ref[...] = x_ref[...] + _rmsnorm(h, g_ref[...])


def _outproj(x, y_a, y_b, w_out, g_post):
    t, d = x.shape
    ka, kb = y_a.shape[1], y_b.shape[1]
    tm = PROJ_ROW_TILE
    assert t % tm == 0 and ka == kb and w_out.shape[0] == ka + kb
    const = lambda i: (0, 0)
    return pl.pallas_call(
        _outproj_body,
        grid=(t // tm,),
        in_specs=[
            pl.BlockSpec((tm, d), lambda i: (i, 0)),
            pl.BlockSpec((tm, ka), lambda i: (i, 0)),
            pl.BlockSpec((tm, kb), lambda i: (i, 0)),
            pl.BlockSpec((ka, d), const, pipeline_mode=pl.Buffered(1)),
            pl.BlockSpec((kb, d), lambda i: (1, 0), pipeline_mode=pl.Buffered(1)),
            pl.BlockSpec((1, d), const),
        ],
        out_specs=pl.BlockSpec((tm, d), lambda i: (i, 0)),
        out_shape=jax.ShapeDtypeStruct((t, d), F32),
        compiler_params=_params(("parallel",)),
        name="outproj",
    )(x, y_a, y_b, w_out, w_out, g_post)


def _row(v):
    return v.reshape(1, -1).astype(F32)


def kernel(x, ffn1_norm_pre, ffn1_norm_post, ffn1_w_gate, ffn1_w_up, ffn1_w_down, mix_norm_pre, mix_norm_post, w_in, sgu_norm, w_spatial, b_spatial, conv_w, conv_b, dt_bias, a_log, d_skip, ssm_norm, w_out, ffn2_norm_pre, ffn2_norm_post, ffn2_w_gate, ffn2_w_up, ffn2_w_down):
    batch, seq, d = x.shape
    depth = ffn1_norm_pre.shape[0]
    a_width = sgu_norm.shape[1]
    xf = x.reshape(batch * seq, d)
    for l in range(depth):
        w_in_t = jnp.swapaxes(w_in, 1, 2)
        n_in = w_in.shape[2]
        whole = lambda w: (w, l, 0, w.shape[1], w.shape[1], False)
        xf, w_uv, w_zxd = _ffn(
            xf, _row(ffn1_norm_pre[l]), _row(ffn1_norm_post[l]),
            ffn1_w_gate[l].astype(BF16), ffn1_w_up[l].astype(BF16), ffn1_w_down[l].astype(BF16),
            ((w_in_t, l, 0, 2 * a_width, 2 * a_width, False),
             (w_in_t, l, 2 * a_width, n_in - 2 * a_width, ZXD_WIDTH, True)))

        g_mix = _row(mix_norm_pre[l])
        b_full = jnp.broadcast_to(b_spatial[l][:, :, None], b_spatial[l].shape + (A_HEAD_DIM,))
        y_a, xn = _inproj_sgu(xf, g_mix, w_uv, _row(sgu_norm[l]), w_spatial[l], b_full)

        pad_heads = lambda v: jnp.pad(v.astype(F32), (0, LANES - B_HEADS)).reshape(1, LANES)
        y_b, w2_gate, w2_up, w2_down, wo = _inproj_ssd(
            xn, w_zxd, batch, conv_w[l], _row(conv_b[l]), pad_heads(dt_bias[l]), pad_heads(a_log[l]),
            _row(jnp.repeat(d_skip[l], B_HEAD_DIM)), _row(ssm_norm[l]),
            (whole(ffn2_w_gate), whole(ffn2_w_up), whole(ffn2_w_down), whole(w_out)))

        xf = _outproj(xf, y_a, y_b, wo, _row(mix_norm_post[l]))

        (xf,) = _ffn(xf, _row(ffn2_norm_pre[l]), _row(ffn2_norm_post[l]), w2_gate, w2_up, w2_down)
    return xf.reshape(batch, seq, d)
```

```python
import functools

import jax
import jax.numpy as jnp
from jax import lax
from jax.experimental import pallas as pl
from jax.experimental.pallas import tpu as pltpu

F32 = jnp.float32
BF16 = jnp.bfloat16

EPS = 1e-6
HALF_STEP = 0.5
SQRT_HALF = 0.7071067811865476

A_HEAD_DIM = 128
A_CHUNK = 128
B_HEAD_DIM = 64
B_HEADS = 16
B_GROUPS = 2
B_STATE = 128
B_CONV = 4
B_CHUNK = 256
B_WIDTH = B_HEADS * B_HEAD_DIM
B_GROUP_WIDTH = B_WIDTH // B_GROUPS
B_CONV_DIM = B_WIDTH + 2 * B_GROUPS * B_STATE

LANES = 128
SUBLANES = 8
BF16_SUBLANES = 16
VMEM_LIMIT_BYTES = 60 * 1024 * 1024

FFN_ROW_TILE = 1024
FFN_FF_TILE = 512
FFN_FF_SUB = 256
FFN_FINISH_ROWS = 256
FFN_HEAD_FF_TILE = 256
PROJ_ROW_TILE = 1024
SPLIT_PIECES = 2
PROJ_PIECE = 256


def _rmsnorm(x, g):
    return x * lax.rsqrt(jnp.mean(x * x, axis=-1, keepdims=True) + EPS) * g


def _gelu(x):
    return 0.5 * x * (1.0 + lax.erf(x * SQRT_HALF))


def _silu(x):
    return x * jax.nn.sigmoid(x)


def _dot(a, b):
    return jnp.dot(a, b, preferred_element_type=F32)


def _dot_nt(a, b_t):
    return lax.dot_general(a, b_t, (((1,), (1,)), ((), ())), preferred_element_type=F32)


def _split3(v):
    hi = v.astype(BF16)
    r1 = v - hi.astype(F32)
    mid = r1.astype(BF16)
    lo = (r1 - mid.astype(F32)).astype(BF16)
    return hi, mid, lo


def _params(semantics):
    return pltpu.CompilerParams(dimension_semantics=semantics, vmem_limit_bytes=VMEM_LIMIT_BYTES)


def _slab_rows(rows, max_slabs, align):
    for slab in range(align, rows + 1, align):
        if rows % slab == 0 and rows // slab <= max_slabs:
            return slab
    raise ValueError(f"no slab size for {rows} rows in {max_slabs} steps")


def _side_cast_specs(side_casts, flat_step, total_steps):
    inputs, in_specs, out_shapes, out_specs, plans = [], [], [], [], []
    for src, layer, row_start, n_rows, out_rows, transpose in side_casts:
        n_cols = src.shape[2]
        slab_rows = _slab_rows(out_rows, total_steps, LANES if transpose else BF16_SUBLANES)
        assert row_start % slab_rows == 0
        n_real, n_out, first = pl.cdiv(n_rows, slab_rows), out_rows // slab_rows, row_start // slab_rows
        src_slab = lambda *idx, n=n_real, first=first, layer=layer: (
            layer, first + jnp.minimum(flat_step(*idx), n - 1), 0)
        dst_slab = lambda *idx, n=n_out: jnp.minimum(flat_step(*idx), n - 1)
        inputs.append(src)
        in_specs.append(pl.BlockSpec((None, slab_rows, n_cols), src_slab))
        if transpose:
            out_shapes.append(jax.ShapeDtypeStruct((n_cols, out_rows), BF16))
            out_specs.append(pl.BlockSpec((n_cols, slab_rows), lambda *idx, f=dst_slab: (0, f(*idx))))
        else:
            out_shapes.append(jax.ShapeDtypeStruct((out_rows, n_cols), BF16))
            out_specs.append(pl.BlockSpec((slab_rows, n_cols), lambda *idx, f=dst_slab: (f(*idx), 0)))
        plans.append((n_rows, n_out, transpose))
    return inputs, in_specs, out_shapes, out_specs, tuple(plans)


def _run_side_casts(step, src_refs, dst_refs, plans):
    for src_ref, dst_ref, (n_rows, n_slabs, transpose) in zip(src_refs, dst_refs, plans):
        slab = src_ref[...]
        if n_rows != n_slabs * src_ref.shape[0]:
            row = jnp.minimum(step, n_slabs - 1) * src_ref.shape[0] + lax.broadcasted_iota(jnp.int32, slab.shape, 0)
            slab = jnp.where(row < n_rows, slab, 0.0)
        dst_ref[...] = (slab.T if transpose else slab).astype(BF16)


def _swiglu_partial(xn, wg, wu, wd, ff_sub, after_first=None):
    width = wg.shape[1]
    acts = []
    for start in range(0, width, ff_sub):
        cols = slice(start, min(start + ff_sub, width))
        acts.append((_silu(_dot(xn, wg[:, cols])) * _dot(xn, wu[:, cols])).astype(BF16))
        if start == 0 and after_first is not None:
            after_first()
    return _dot(jnp.concatenate(acts, axis=1), wd)


def _ffn_body(*refs, n_steps, tail_width, ff_sub, head_piece, casts):
    refs = list(refs)
    x_ref, gpre_ref, gpost_ref, wg_ref, wu_ref, wd_ref = refs[:6]
    del refs[:6]
    head_ref = refs.pop(0) if head_piece else None
    cast_srcs = refs[:len(casts)]
    o_ref = refs[len(casts)]
    cast_dsts = refs[len(casts) + 1:-1]
    xn_ref = refs[-1]
    i = pl.program_id(0)
    j = pl.program_id(1)
    tf = wg_ref.shape[1]
    computed = (i > 0) if head_piece else True

    def partial_out(width, rows=slice(None)):
        def side():
            if rows.start in (None, 0):
                _run_side_casts(i * n_steps + j, cast_srcs, cast_dsts, casts)
        return _swiglu_partial(xn_ref[rows, :], wg_ref[:, :width], wu_ref[:, :width], wd_ref[:width, :], ff_sub, side)

    if head_piece:
        @pl.when(jnp.logical_and(i == 0, j < o_ref.shape[0] // head_piece))
        def _():
            o_ref[pl.ds(pl.multiple_of(j * head_piece, head_piece), head_piece), :] = head_ref[...]

    @pl.when(jnp.logical_and(computed, j == 0))
    def _():
        xn_ref[...] = _rmsnorm(x_ref[...], gpre_ref[...]).astype(BF16)
        o_ref[...] = partial_out(tf)

    @pl.when(jnp.logical_and(computed, jnp.logical_and(j > 0, j < n_steps - 1)))
    def _():
        o_ref[...] += partial_out(tf)

    @pl.when(jnp.logical_and(computed, j == n_steps - 1))
    def _():
        for start in range(0, x_ref.shape[0], FFN_FINISH_ROWS):
            rows = slice(start, start + FFN_FINISH_ROWS)
            h = o_ref[rows, :] + partial_out(tail_width, rows)
            o_ref[rows, :] = x_ref[rows, :] + HALF_STEP * _rmsnorm(h, gpost_ref[...])


def _ffn(x, g_pre, g_post, wg, wu, wd, head=None, side_casts=()):
    t, d = x.shape
    f = wg.shape[1]
    tm, tf, ff_sub = FFN_ROW_TILE, FFN_FF_TILE, FFN_FF_SUB
    n_steps = pl.cdiv(f, tf)
    tail_width = f - (n_steps - 1) * tf
    assert t % tm == 0 and n_steps >= 2 and tail_width % LANES == 0
    row = pl.BlockSpec((tm, d), lambda i, j: (i, 0))
    vec = pl.BlockSpec((1, d), lambda i, j: (0, 0))
    ff_tile = (lambda i, j: jnp.where(i == 0, 0, j)) if head is not None else (lambda i, j: j)
    head_in, head_specs, head_piece = [], [], None
    if head is not None:
        assert head.shape == (tm, d)
        head_piece = _slab_rows(tm, n_steps, SUBLANES)
        n_head = tm // head_piece
        head_in = [head]
        head_specs = [pl.BlockSpec((head_piece, d), lambda i, j: (jnp.where(i == 0, jnp.minimum(j, n_head - 1), n_head - 1), 0))]
    cast_in, cast_in_specs, cast_shapes, cast_out_specs, cast_plans = _side_cast_specs(
        side_casts, lambda i, j: i * n_steps + j, (t // tm) * n_steps)
    return pl.pallas_call(
        functools.partial(_ffn_body, n_steps=n_steps, tail_width=tail_width, ff_sub=ff_sub,
                          head_piece=head_piece, casts=cast_plans),
        grid=(t // tm, n_steps),
        in_specs=[row, vec, vec,
                  pl.BlockSpec((d, tf), lambda i, j: (0, ff_tile(i, j))),
                  pl.BlockSpec((d, tf), lambda i, j: (0, ff_tile(i, j))),
                  pl.BlockSpec((tf, d), lambda i, j: (ff_tile(i, j), 0))] + head_specs + cast_in_specs,
        out_specs=[row] + cast_out_specs,
        out_shape=[jax.ShapeDtypeStruct((t, d), F32)] + cast_shapes,
        scratch_shapes=[pltpu.VMEM((tm, d), BF16)],
        compiler_params=_params(("arbitrary", "arbitrary") if side_casts else ("parallel", "arbitrary")),
        name="ffn",
    )(x, g_pre, g_post, wg, wu, wd, *head_in, *cast_in)


def _ffn_head_body(*refs, n_steps, tail_width, ff_sub, casts):
    x_ref, gpre_ref, gpost_ref, wg32_ref, wu32_ref, wd32_ref = refs[:6]
    cast_srcs = refs[6:6 + len(casts)]
    o_ref, wg16_ref, wu16_ref, wd16_ref = refs[6 + len(casts):10 + len(casts)]
    cast_dsts = refs[10 + len(casts):-1]
    xn_ref = refs[-1]
    j = pl.program_id(0)
    tf = wg32_ref.shape[1]

    def partial_out(width):
        wg = wg32_ref[:, :width].astype(BF16)
        wu = wu32_ref[:, :width].astype(BF16)
        wd = wd32_ref[:width, :].astype(BF16)
        wg16_ref[:, :width] = wg
        wu16_ref[:, :width] = wu
        wd16_ref[:width, :] = wd
        return _swiglu_partial(xn_ref[...], wg, wu, wd, ff_sub,
                               lambda: _run_side_casts(j, cast_srcs, cast_dsts, casts))

    @pl.when(j == 0)
    def _():
        xn_ref[...] = _rmsnorm(x_ref[...], gpre_ref[...]).astype(BF16)
        o_ref[...] = partial_out(tf)

    @pl.when(jnp.logical_and(j > 0, j < n_steps - 1))
    def _():
        o_ref[...] += partial_out(tf)

    @pl.when(j == n_steps - 1)
    def _():
        h = o_ref[...] + partial_out(tail_width)
        o_ref[...] = x_ref[...] + HALF_STEP * _rmsnorm(h, gpost_ref[...])


def _ffn_head(x, g_pre, g_post, w_gate, w_up, w_down, layer, side_casts=()):
    d = x.shape[1]
    f = w_gate.shape[2]
    tm, tf, ff_sub = FFN_ROW_TILE, FFN_HEAD_FF_TILE, FFN_FF_SUB
    n_steps = pl.cdiv(f, tf)
    tail_width = f - (n_steps - 1) * tf
    assert n_steps >= 2 and tail_width % LANES == 0
    const = lambda j: (0, 0)
    vec = pl.BlockSpec((1, d), const)
    cast_in, cast_in_specs, cast_shapes, cast_out_specs, cast_plans = _side_cast_specs(
        side_casts, lambda j: j, n_steps)
    return pl.pallas_call(
        functools.partial(_ffn_head_body, n_steps=n_steps, tail_width=tail_width, ff_sub=ff_sub, casts=cast_plans),
        grid=(n_steps,),
        in_specs=[pl.BlockSpec((tm, d), const, pipeline_mode=pl.Buffered(1)), vec, vec,
                  pl.BlockSpec((None, d, tf), lambda j: (layer, 0, j)),
                  pl.BlockSpec((None, d, tf), lambda j: (layer, 0, j)),
                  pl.BlockSpec((None, tf, d), lambda j: (layer, j, 0))] + cast_in_specs,
        out_specs=[pl.BlockSpec((tm, d), const),
                   pl.BlockSpec((d, tf), lambda j: (0, j)),
                   pl.BlockSpec((d, tf), lambda j: (0, j)),
                   pl.BlockSpec((tf, d), lambda j: (j, 0))] + cast_out_specs,
        out_shape=[jax.ShapeDtypeStruct((tm, d), F32),
                   jax.ShapeDtypeStruct((d, f), BF16), jax.ShapeDtypeStruct((d, f), BF16),
                   jax.ShapeDtypeStruct((f, d), BF16)] + cast_shapes,
        scratch_shapes=[pltpu.VMEM((tm, d), BF16)],
        compiler_params=_params(("arbitrary",)),
        name="ffn_head",
    )(x, g_pre, g_post, w_gate, w_up, w_down, *cast_in)


def _sgu_body(x_ref, gpre_ref, wu_ref, wv_ref, gs_ref, ws_ref, bs_ref, o_ref, xn_ref, u_ref, v_ref):
    tm = x_ref.shape[0]
    heads = ws_ref.shape[0]
    xn = _rmsnorm(x_ref[...], gpre_ref[...]).astype(BF16)
    xn_ref[...] = xn
    u_ref[...] = _gelu(_dot_nt(xn, wu_ref[...]))
    v_ref[...] = _rmsnorm(_gelu(_dot_nt(xn, wv_ref[...])), gs_ref[...]).astype(BF16)
    rows = lax.broadcasted_iota(jnp.int32, (A_CHUNK, A_CHUNK), 0)
    cols = lax.broadcasted_iota(jnp.int32, (A_CHUNK, A_CHUNK), 1)
    causal = rows >= cols
    for h in range(heads):
        w = jnp.where(causal, ws_ref[h], 0.0).astype(BF16)
        bias = bs_ref[h]
        lanes = pl.ds(h * A_HEAD_DIM, A_HEAD_DIM)
        for c in range(tm // A_CHUNK):
            rws = pl.ds(c * A_CHUNK, A_CHUNK)
            mixed = _dot(w, v_ref[rws, lanes]) + bias
            o_ref[rws, lanes] = (u_ref[rws, lanes] * mixed).astype(BF16)


def _inproj_sgu(x, g_pre, w_uv, g_sgu, w_spatial, b_full):
    t, d = x.shape
    aw = w_uv.shape[0] // 2
    heads = w_spatial.shape[0]
    tm = PROJ_ROW_TILE
    assert t % tm == 0 and tm % A_CHUNK == 0 and aw == heads * A_HEAD_DIM
    const2 = lambda i: (0, 0)
    const3 = lambda i: (0, 0, 0)
    return pl.pallas_call(
        _sgu_body,
        grid=(t // tm,),
        in_specs=[
            pl.BlockSpec((tm, d), lambda i: (i, 0)),
            pl.BlockSpec((1, d), const2),
            pl.BlockSpec((aw, d), const2, pipeline_mode=pl.Buffered(1)),
            pl.BlockSpec((aw, d), lambda i: (1, 0), pipeline_mode=pl.Buffered(1)),
            pl.BlockSpec((1, aw), const2),
            pl.BlockSpec((heads, A_CHUNK, A_CHUNK), const3),
            pl.BlockSpec((heads, A_CHUNK, A_HEAD_DIM), const3),
        ],
        out_specs=[pl.BlockSpec((tm, aw), lambda i: (i, 0)), pl.BlockSpec((tm, d), lambda i: (i, 0))],
        out_shape=[jax.ShapeDtypeStruct((t, aw), BF16), jax.ShapeDtypeStruct((t, d), BF16)],
        scratch_shapes=[pltpu.VMEM((tm, aw), F32), pltpu.VMEM((tm, aw), BF16)],
        compiler_params=_params(("parallel",)),
        name="inproj_sgu",
    )(x, g_pre, w_uv, w_uv, g_sgu, w_spatial, b_full)


Z_OFF = 0
XBC_OFF = B_WIDTH
DT_OFF = B_WIDTH + B_CONV_DIM
ZXD_WIDTH = DT_OFF + LANES
CONV_HALO = SUBLANES


def _split_cols(v, pieces):
    parts = []
    rest = v
    for _ in range(pieces):
        part = rest.astype(BF16)
        parts.append(part)
        rest = rest - part.astype(F32)
    return jnp.concatenate(parts, axis=1)


def _ssd_mix(zxd_ref, convw_ref, convb_ref, dtb_ref, alog_ref, dskip_ref, norm_ref, o_ref,
             ext_ref, state_ref, y_ref, side_work=()):
    cl = B_CHUNK
    side_work = list(side_work)

    def side(n=1):
        for _ in range(n):
            if side_work:
                side_work.pop(0)()

    ext_ref[CONV_HALO:CONV_HALO + cl, :] = zxd_ref[:, XBC_OFF:XBC_OFF + B_CONV_DIM]
    ext = ext_ref[...]
    conv = convw_ref[0:1, :] * ext
    for k in range(1, B_CONV):
        conv = pltpu.roll(conv, 1, axis=0) + convw_ref[k:k + 1, :] * ext
    conv = conv[CONV_HALO:, :] + convb_ref[...]
    ext_ref[0:CONV_HALO, :] = ext_ref[cl:cl + CONV_HALO, :]
    side()
    xbc = _silu(conv)
    xs = xbc[:, :B_WIDTH]

    dt = jax.nn.softplus(zxd_ref[:, DT_OFF:DT_OFF + LANES] + dtb_ref[...])
    da = dt * (-jnp.exp(alog_ref[...]))
    rows = lax.broadcasted_iota(jnp.int32, (cl, cl), 0)
    cols = lax.broadcasted_iota(jnp.int32, (cl, cl), 1)
    causal = rows >= cols
    tril = jnp.where(causal, 1.0, 0.0).astype(BF16)
    acs = sum(_dot(tril, part) for part in _split3(da))
    acs_t = acs.T
    side()

    e_rows = lax.broadcasted_iota(jnp.int32, (SPLIT_PIECES * LANES, B_WIDTH), 0)
    e_cols = lax.broadcasted_iota(jnp.int32, (SPLIT_PIECES * LANES, B_WIDTH), 1)
    expand = jnp.where(e_cols // B_HEAD_DIM == e_rows % LANES, 1.0, 0.0).astype(BF16)
    acs_e = _dot(_split_cols(acs, SPLIT_PIECES), expand)
    dt_e = _dot(_split_cols(dt, SPLIT_PIECES), expand)

    x = xs * dt_e
    even_head = (lax.broadcasted_iota(jnp.int32, (cl, B_WIDTH), 1) // B_HEAD_DIM) % 2 == 0
    x_even = jnp.where(even_head, x, 0.0).astype(BF16)
    x_odd = jnp.where(even_head, 0.0, x).astype(BF16)
    acs_last = acs_e[cl - 1:cl, :]
    decay_from_start = jnp.exp(acs_e)
    x_to_end = (x * jnp.exp(acs_last - acs_e)).astype(BF16)
    chunk_decay = jnp.exp(acs_last)
    side()

    heads_per_group = B_HEADS // B_GROUPS
    for g in range(B_GROUPS):
        gl = slice(g * B_GROUP_WIDTH, (g + 1) * B_GROUP_WIDTH)
        b_off = B_WIDTH + g * B_STATE
        c_off = B_WIDTH + B_GROUPS * B_STATE + g * B_STATE
        bc_t = xbc[:, b_off:b_off + B_STATE].T.astype(BF16)
        cc = xbc[:, c_off:c_off + B_STATE].astype(BF16)
        cb = _dot(cc, bc_t)
        state = state_ref[g]
        y_off = _dot(cc, state.astype(BF16)) * decay_from_start[:, gl]
        state_ref[g] = state * chunk_decay[:, gl] + _dot(bc_t, x_to_end[:, gl])
        for pair in range(heads_per_group // 2):
            head_a = g * heads_per_group + 2 * pair
            pl_off = head_a * B_HEAD_DIM
            y_pair = y_off[:, pair * LANES:(pair + 1) * LANES]
            for head, x_half in ((head_a, x_even), (head_a + 1, x_odd)):
                diff = acs[:, head:head + 1] - acs_t[head:head + 1, :]
                m = (cb * jnp.exp(jnp.where(causal, diff, -jnp.inf))).astype(BF16)
                y_pair = y_pair + _dot(m, x_half[:, pl_off:pl_off + LANES])
            y_ref[:, pl_off:pl_off + LANES] = y_pair
            side()

    side(len(side_work))
    y = (y_ref[...] + dskip_ref[...] * xs) * _silu(zxd_ref[:, Z_OFF:Z_OFF + B_WIDTH])
    for g in range(B_GROUPS):
        gl = slice(g * B_GROUP_WIDTH, (g + 1) * B_GROUP_WIDTH)
        yg = y[:, gl]
        yg = yg * lax.rsqrt(jnp.mean(yg * yg, axis=-1, keepdims=True) + EPS)
        o_ref[:, gl] = (yg * norm_ref[:, gl]).astype(BF16)


def _ssd_body(*refs, nc, casts):
    xn_ref, w_ref, convw_ref, convb_ref, dtb_ref, alog_ref, dskip_ref, norm_ref = refs[:8]
    cast_srcs = refs[8:8 + len(casts)]
    o_ref = refs[8 + len(casts)]
    cast_dsts = refs[9 + len(casts):-5]
    zxd0_ref, zxd1_ref, ext_ref, state_ref, y_ref = refs[-5:]
    s = pl.program_id(1)
    bufs = (zxd0_ref, zxd1_ref)

    def project_pieces(dst_ref):
        def piece(start):
            cols = slice(start, min(start + PROJ_PIECE, ZXD_WIDTH))

            def run():
                dst_ref[:, cols] = _dot(xn_ref[...], w_ref[:, cols])
            return run

        def casts_then(run):
            def both():
                _run_side_casts(pl.program_id(0) * (nc + 1) + s, cast_srcs, cast_dsts, casts)
                run()
            return both
        pieces = [piece(start) for start in range(0, ZXD_WIDTH, PROJ_PIECE)]
        return [casts_then(pieces[0])] + pieces[1:]

    def project(dst_ref):
        for run in project_pieces(dst_ref):
            run()

    def mix(src_ref, side_work=()):
        _ssd_mix(src_ref, convw_ref, convb_ref, dtb_ref, alog_ref, dskip_ref, norm_ref, o_ref,
                 ext_ref, state_ref, y_ref, side_work)

    @pl.when(s == 0)
    def _():
        ext_ref[0:CONV_HALO, :] = jnp.zeros((CONV_HALO, B_CONV_DIM), F32)
        state_ref[...] = jnp.zeros_like(state_ref)
        project(bufs[0])

    for parity in range(2):
        @pl.when(jnp.logical_and(jnp.logical_and(s > 0, s < nc), lax.rem(s, 2) == parity))
        def _():
            mix(bufs[1 - parity], project_pieces(bufs[parity]))

    @pl.when(s == nc)
    def _():
        _run_side_casts(pl.program_id(0) * (nc + 1) + s, cast_srcs, cast_dsts, casts)
        mix(bufs[(nc - 1) % 2])


def _inproj_ssd(xn, w_zxd, batch, conv_w, conv_b, dt_bias, a_log, d_skip, ssm_norm, side_casts=()):
    t, d = xn.shape
    seq = t // batch
    assert seq % B_CHUNK == 0 and w_zxd.shape[1] == ZXD_WIDTH
    nc = seq // B_CHUNK
    const = lambda b, s: (0, 0)
    cast_in, cast_in_specs, cast_shapes, cast_out_specs, cast_plans = _side_cast_specs(
        side_casts, lambda b, s: b * (nc + 1) + s, batch * (nc + 1))
    return pl.pallas_call(
        functools.partial(_ssd_body, nc=nc, casts=cast_plans),
        grid=(batch, nc + 1),
        in_specs=[
            pl.BlockSpec((B_CHUNK, d), lambda b, s: (b * nc + jnp.minimum(s, nc - 1), 0)),
            pl.BlockSpec((d, ZXD_WIDTH), const),
            pl.BlockSpec((B_CONV, B_CONV_DIM), const),
            pl.BlockSpec((1, B_CONV_DIM), const),
            pl.BlockSpec((1, LANES), const),
            pl.BlockSpec((1, LANES), const),
            pl.BlockSpec((1, B_WIDTH), const),
            pl.BlockSpec((1, B_WIDTH), const),
        ] + cast_in_specs,
        out_specs=[pl.BlockSpec((B_CHUNK, B_WIDTH), lambda b, s: (b * nc + jnp.maximum(s - 1, 0), 0))]
        + cast_out_specs,
        out_shape=[jax.ShapeDtypeStruct((t, B_WIDTH), BF16)] + cast_shapes,
        scratch_shapes=[
            pltpu.VMEM((B_CHUNK, ZXD_WIDTH), F32),
            pltpu.VMEM((B_CHUNK, ZXD_WIDTH), F32),
            pltpu.VMEM((CONV_HALO + B_CHUNK, B_CONV_DIM), F32),
            pltpu.VMEM((B_GROUPS, B_STATE, B_GROUP_WIDTH), F32),
            pltpu.VMEM((B_CHUNK, B_WIDTH), F32),
        ],
        compiler_params=_params(("arbitrary", "arbitrary")),
        name="inproj_ssd",
    )(xn, w_zxd, conv_w, conv_b, dt_bias, a_log, d_skip, ssm_norm, *cast_in)


def _outproj_body(x_ref, ya_ref, yb_ref, wa_ref, wb_ref, g_ref, o_ref):
    h = _dot(ya_ref[...], wa_ref[...]) + _dot(yb_ref[...], wb_ref[...])
    o_ref[...] = x_ref[...] + _rmsnorm(h, g_ref[...])


def _outproj(x, y_a, y_b, w_out, g_post):
    t, d = x.shape
    ka, kb = y_a.shape[1], y_b.shape[1]
    tm = PROJ_ROW_TILE
    assert t % tm == 0 and ka == kb and w_out.shape[0] == ka + kb
    const = lambda i: (0, 0)
    return pl.pallas_call(
        _outproj_body,
        grid=(t // tm,),
        in_specs=[
            pl.BlockSpec((tm, d), lambda i: (i, 0)),
            pl.BlockSpec((tm, ka), lambda i: (i, 0)),
            pl.BlockSpec((tm, kb), lambda i: (i, 0)),
            pl.BlockSpec((ka, d), const, pipeline_mode=pl.Buffered(1)),
            pl.BlockSpec((kb, d), lambda i: (1, 0), pipeline_mode=pl.Buffered(1)),
            pl.BlockSpec((1, d), const),
        ],
        out_specs=pl.BlockSpec((tm, d), lambda i: (i, 0)),
        out_shape=jax.ShapeDtypeStruct((t, d), F32),
        compiler_params=_params(("parallel",)),
        name="outproj",
    )(x, y_a, y_b, w_out, w_out, g_post)


def _row(v):
    return v.reshape(1, -1).astype(F32)


def kernel(x, ffn1_norm_pre, ffn1_norm_post, ffn1_w_gate, ffn1_w_up, ffn1_w_down, mix_norm_pre, mix_norm_post, w_in, sgu_norm, w_spatial, b_spatial, conv_w, conv_b, dt_bias, a_log, d_skip, ssm_norm, w_out, ffn2_norm_pre, ffn2_norm_post, ffn2_w_gate, ffn2_w_up, ffn2_w_down):
    batch, seq, d = x.shape
    depth = ffn1_norm_pre.shape[0]
    a_width = sgu_norm.shape[1]
    xf = x.reshape(batch * seq, d)
    for l in range(depth):
        w_in_t = jnp.swapaxes(w_in, 1, 2)
        n_in = w_in.shape[2]
        whole = lambda w: (w, l, 0, w.shape[1], w.shape[1], False)
        head, w1_gate, w1_up, w1_down, w_uv, w_zxd = _ffn_head(
            xf, _row(ffn1_norm_pre[l]), _row(ffn1_norm_post[l]), ffn1_w_gate, ffn1_w_up, ffn1_w_down, l,
            ((w_in_t, l, 0, 2 * a_width, 2 * a_width, False),
             (w_in_t, l, 2 * a_width, n_in - 2 * a_width, ZXD_WIDTH, True)))
        (xf,) = _ffn(xf, _row(ffn1_norm_pre[l]), _row(ffn1_norm_post[l]), w1_gate, w1_up, w1_down, head=head)

        g_mix = _row(mix_norm_pre[l])
        b_full = jnp.broadcast_to(b_spatial[l][:, :, None], b_spatial[l].shape + (A_HEAD_DIM,))
        y_a, xn = _inproj_sgu(xf, g_mix, w_uv, _row(sgu_norm[l]), w_spatial[l], b_full)

        pad_heads = lambda v: jnp.pad(v.astype(F32), (0, LANES - B_HEADS)).reshape(1, LANES)
        y_b, w2_gate, w2_up, w2_down, wo = _inproj_ssd(
            xn, w_zxd, batch, conv_w[l], _row(conv_b[l]), pad_heads(dt_bias[l]), pad_heads(a_log[l]),
            _row(jnp.repeat(d_skip[l], B_HEAD_DIM)), _row(ssm_norm[l]),
            (whole(ffn2_w_gate), whole(ffn2_w_up), whole(ffn2_w_down), whole(w_out)))

        xf = _outproj(xf, y_a, y_b, wo, _row(mix_norm_post[l]))

        (xf,) = _ffn(xf, _row(ffn2_norm_pre[l]), _row(ffn2_norm_post[l]), w2_gate, w2_up, w2_down)
    return xf.reshape(batch, seq, d)
```

```python
import functools

import jax
import jax.numpy as jnp
from jax import lax
from jax.experimental import pallas as pl
from jax.experimental.pallas import tpu as pltpu

F32 = jnp.float32
BF16 = jnp.bfloat16

EPS = 1e-6
HALF_STEP = 0.5
SQRT_HALF = 0.7071067811865476

A_HEAD_DIM = 128
A_CHUNK = 128
B_HEAD_DIM = 64
B_HEADS = 16
B_GROUPS = 2
B_STATE = 128
B_CONV = 4
B_CHUNK = 256
B_WIDTH = B_HEADS * B_HEAD_DIM
B_GROUP_WIDTH = B_WIDTH // B_GROUPS
B_CONV_DIM = B_WIDTH + 2 * B_GROUPS * B_STATE

LANES = 128
SUBLANES = 8
BF16_SUBLANES = 16
VMEM_LIMIT_BYTES = 60 * 1024 * 1024

FFN_ROW_TILE = 1024
FFN_FF_TILE = 512
FFN_FF_SUB = 256
FFN_FINISH_ROWS = 256
FFN_HEAD_FF_TILE = 256
PROJ_ROW_TILE = 1024
SPLIT_PIECES = 2
PROJ_PIECE = 256


def _rmsnorm(x, g):
    return x * lax.rsqrt(jnp.mean(x * x, axis=-1, keepdims=True) + EPS) * g


def _gelu(x):
    return 0.5 * x * (1.0 + lax.erf(x * SQRT_HALF))


def _silu(x):
    return x * jax.nn.sigmoid(x)


def _dot(a, b):
    return jnp.dot(a, b, preferred_element_type=F32)


def _dot_nt(a, b_t):
    return lax.dot_general(a, b_t, (((1,), (1,)), ((), ())), preferred_element_type=F32)


def _split3(v):
    hi = v.astype(BF16)
    r1 = v - hi.astype(F32)
    mid = r1.astype(BF16)
    lo = (r1 - mid.astype(F32)).astype(BF16)
    return hi, mid, lo


def _params(semantics):
    return pltpu.CompilerParams(dimension_semantics=semantics, vmem_limit_bytes=VMEM_LIMIT_BYTES)


def _slab_rows(rows, max_slabs, align):
    for slab in range(align, rows + 1, align):
        if rows % slab == 0 and rows // slab <= max_slabs:
            return slab
    raise ValueError(f"no slab size for {rows} rows in {max_slabs} steps")


def _side_cast_specs(side_casts, flat_step, total_steps):
    inputs, in_specs, out_shapes, out_specs, plans = [], [], [], [], []
    for src, layer, row_start, n_rows, out_rows, transpose in side_casts:
        n_cols = src.shape[2]
        slab_rows = _slab_rows(out_rows, total_steps, LANES if transpose else BF16_SUBLANES)
        assert row_start % slab_rows == 0
        n_real, n_out, first = pl.cdiv(n_rows, slab_rows), out_rows // slab_rows, row_start // slab_rows
        src_slab = lambda *idx, n=n_real, first=first, layer=layer: (
            layer, first + jnp.minimum(flat_step(*idx), n - 1), 0)
        dst_slab = lambda *idx, n=n_out: jnp.minimum(flat_step(*idx), n - 1)
        inputs.append(src)
        in_specs.append(pl.BlockSpec((None, slab_rows, n_cols), src_slab))
        if transpose:
            out_shapes.append(jax.ShapeDtypeStruct((n_cols, out_rows), BF16))
            out_specs.append(pl.BlockSpec((n_cols, slab_rows), lambda *idx, f=dst_slab: (0, f(*idx))))
        else:
            out_shapes.append(jax.ShapeDtypeStruct((out_rows, n_cols), BF16))
            out_specs.append(pl.BlockSpec((slab_rows, n_cols), lambda *idx, f=dst_slab: (f(*idx), 0)))
        plans.append((n_rows, n_out, transpose))
    return inputs, in_specs, out_shapes, out_specs, tuple(plans)


def _run_side_casts(step, src_refs, dst_refs, plans):
    for src_ref, dst_ref, (n_rows, n_slabs, transpose) in zip(src_refs, dst_refs, plans):
        slab = src_ref[...]
        if n_rows != n_slabs * src_ref.shape[0]:
            row = jnp.minimum(step, n_slabs - 1) * src_ref.shape[0] + lax.broadcasted_iota(jnp.int32, slab.shape, 0)
            slab = jnp.where(row < n_rows, slab, 0.0)
        dst_ref[...] = (slab.T if transpose else slab).astype(BF16)


def _swiglu_partial(xn, wg, wu, wd, ff_sub, after_first=None, merge_narrow=False):
    width = wg.shape[1]
    acts = []
    for start in range(0, width, ff_sub):
        cols = slice(start, min(start + ff_sub, width))
        sub = cols.stop - start
        if merge_narrow and 2 * sub <= ff_sub:
            both = _dot(xn, jnp.concatenate([wg[:, cols], wu[:, cols]], axis=1))
            gate, up = both[:, :sub], both[:, sub:]
        else:
            gate, up = _dot(xn, wg[:, cols]), _dot(xn, wu[:, cols])
        acts.append((_silu(gate) * up).astype(BF16))
        if start == 0 and after_first is not None:
            after_first()
    return _dot(jnp.concatenate(acts, axis=1), wd)


def _ffn_body(*refs, n_steps, tail_width, ff_sub, head_piece, casts):
    refs = list(refs)
    x_ref, gpre_ref, gpost_ref, wg_ref, wu_ref, wd_ref = refs[:6]
    del refs[:6]
    head_ref = refs.pop(0) if head_piece else None
    cast_srcs = refs[:len(casts)]
    o_ref = refs[len(casts)]
    cast_dsts = refs[len(casts) + 1:-1]
    xn_ref = refs[-1]
    i = pl.program_id(0)
    j = pl.program_id(1)
    tf = wg_ref.shape[1]
    computed = (i > 0) if head_piece else True

    def partial_out(width, rows=slice(None)):
        def side():
            if rows.start in (None, 0):
                _run_side_casts(i * n_steps + j, cast_srcs, cast_dsts, casts)
        return _swiglu_partial(xn_ref[rows, :], wg_ref[:, :width], wu_ref[:, :width], wd_ref[:width, :], ff_sub, side,
                               merge_narrow=rows.start is not None)

    if head_piece:
        @pl.when(jnp.logical_and(i == 0, j < o_ref.shape[0] // head_piece))
        def _():
            o_ref[pl.ds(pl.multiple_of(j * head_piece, head_piece), head_piece), :] = head_ref[...]

    @pl.when(jnp.logical_and(computed, j == 0))
    def _():
        xn_ref[...] = _rmsnorm(x_ref[...], gpre_ref[...]).astype(BF16)
        o_ref[...] = partial_out(tf)

    @pl.when(jnp.logical_and(computed, jnp.logical_and(j > 0, j < n_steps - 1)))
    def _():
        o_ref[...] += partial_out(tf)

    @pl.when(jnp.logical_and(computed, j == n_steps - 1))
    def _():
        for start in range(0, x_ref.shape[0], FFN_FINISH_ROWS):
            rows = slice(start, start + FFN_FINISH_ROWS)
            h = o_ref[rows, :] + partial_out(tail_width, rows)
            o_ref[rows, :] = x_ref[rows, :] + HALF_STEP * _rmsnorm(h, gpost_ref[...])


def _ffn(x, g_pre, g_post, wg, wu, wd, head=None, side_casts=()):
    t, d = x.shape
    f = wg.shape[1]
    tm, tf, ff_sub = FFN_ROW_TILE, FFN_FF_TILE, FFN_FF_SUB
    n_steps = pl.cdiv(f, tf)
    tail_width = f - (n_steps - 1) * tf
    assert t % tm == 0 and n_steps >= 2 and tail_width % LANES == 0
    row = pl.BlockSpec((tm, d), lambda i, j: (i, 0))
    vec = pl.BlockSpec((1, d), lambda i, j: (0, 0))
    ff_tile = (lambda i, j: jnp.where(i == 0, 0, j)) if head is not None else (lambda i, j: j)
    head_in, head_specs, head_piece = [], [], None
    if head is not None:
        assert head.shape == (tm, d)
        head_piece = _slab_rows(tm, n_steps, SUBLANES)
        n_head = tm // head_piece
        head_in = [head]
        head_specs = [pl.BlockSpec((head_piece, d), lambda i, j: (jnp.where(i == 0, jnp.minimum(j, n_head - 1), n_head - 1), 0))]
    assert head is None or not side_casts
    cast_in, cast_in_specs, cast_shapes, cast_out_specs, cast_plans = _side_cast_specs(
        side_casts, lambda i, j: i * n_steps + j, (t // tm) * n_steps)
    return pl.pallas_call(
        functools.partial(_ffn_body, n_steps=n_steps, tail_width=tail_width, ff_sub=ff_sub,
                          head_piece=head_piece, casts=cast_plans),
        grid=(t // tm, n_steps),
        in_specs=[row, vec, vec,
                  pl.BlockSpec((d, tf), lambda i, j: (0, ff_tile(i, j))),
                  pl.BlockSpec((d, tf), lambda i, j: (0, ff_tile(i, j))),
                  pl.BlockSpec((tf, d), lambda i, j: (ff_tile(i, j), 0))] + head_specs + cast_in_specs,
        out_specs=[row] + cast_out_specs,
        out_shape=[jax.ShapeDtypeStruct((t, d), F32)] + cast_shapes,
        scratch_shapes=[pltpu.VMEM((tm, d), BF16)],
        compiler_params=_params(("arbitrary", "arbitrary") if side_casts else ("parallel", "arbitrary")),
        name="ffn",
    )(x, g_pre, g_post, wg, wu, wd, *head_in, *cast_in)


def _ffn_head_body(*refs, n_steps, tail_width, ff_sub, casts):
    x_ref, gpre_ref, gpost_ref, wg32_ref, wu32_ref, wd32_ref = refs[:6]
    cast_srcs = refs[6:6 + len(casts)]
    o_ref, wg16_ref, wu16_ref, wd16_ref = refs[6 + len(casts):10 + len(casts)]
    cast_dsts = refs[10 + len(casts):-1]
    xn_ref = refs[-1]
    j = pl.program_id(0)
    tf = wg32_ref.shape[1]

    def partial_out(width):
        wg = wg32_ref[:, :width].astype(BF16)
        wu = wu32_ref[:, :width].astype(BF16)
        wd = wd32_ref[:width, :].astype(BF16)
        wg16_ref[:, :width] = wg
        wu16_ref[:, :width] = wu
        wd16_ref[:width, :] = wd
        return _swiglu_partial(xn_ref[...], wg, wu, wd, ff_sub,
                               lambda: _run_side_casts(j, cast_srcs, cast_dsts, casts))

    @pl.when(j == 0)
    def _():
        xn_ref[...] = _rmsnorm(x_ref[...], gpre_ref[...]).astype(BF16)
        o_ref[...] = partial_out(tf)

    @pl.when(jnp.logical_and(j > 0, j < n_steps - 1))
    def _():
        o_ref[...] += partial_out(tf)

    @pl.when(j == n_steps - 1)
    def _():
        h = o_ref[...] + partial_out(tail_width)
        o_ref[...] = x_ref[...] + HALF_STEP * _rmsnorm(h, gpost_ref[...])


def _ffn_head(x, g_pre, g_post, w_gate, w_up, w_down, layer, side_casts=()):
    d = x.shape[1]
    f = w_gate.shape[2]
    tm, tf, ff_sub = FFN_ROW_TILE, FFN_HEAD_FF_TILE, FFN_FF_SUB
    n_steps = pl.cdiv(f, tf)
    tail_width = f - (n_steps - 1) * tf
    assert n_steps >= 2 and tail_width % LANES == 0
    const = lambda j: (0, 0)
    vec = pl.BlockSpec((1, d), const)
    cast_in, cast_in_specs, cast_shapes, cast_out_specs, cast_plans = _side_cast_specs(
        side_casts, lambda j: j, n_steps)
    return pl.pallas_call(
        functools.partial(_ffn_head_body, n_steps=n_steps, tail_width=tail_width, ff_sub=ff_sub, casts=cast_plans),
        grid=(n_steps,),
        in_specs=[pl.BlockSpec((tm, d), const, pipeline_mode=pl.Buffered(1)), vec, vec,
                  pl.BlockSpec((None, d, tf), lambda j: (layer, 0, j)),
                  pl.BlockSpec((None, d, tf), lambda j: (layer, 0, j)),
                  pl.BlockSpec((None, tf, d), lambda j: (layer, j, 0))] + cast_in_specs,
        out_specs=[pl.BlockSpec((tm, d), const),
                   pl.BlockSpec((d, tf), lambda j: (0, j)),
                   pl.BlockSpec((d, tf), lambda j: (0, j)),
                   pl.BlockSpec((tf, d), lambda j: (j, 0))] + cast_out_specs,
        out_shape=[jax.ShapeDtypeStruct((tm, d), F32),
                   jax.ShapeDtypeStruct((d, f), BF16), jax.ShapeDtypeStruct((d, f), BF16),
                   jax.ShapeDtypeStruct((f, d), BF16)] + cast_shapes,
        scratch_shapes=[pltpu.VMEM((tm, d), BF16)],
        compiler_params=_params(("arbitrary",)),
        name="ffn_head",
    )(x, g_pre, g_post, w_gate, w_up, w_down, *cast_in)


def _sgu_body(x_ref, gpre_ref, wu_ref, wv_ref, gs_ref, ws_ref, bs_ref, o_ref, xn_ref, u_ref, v_ref):
    tm = x_ref.shape[0]
    heads = ws_ref.shape[0]
    xn = _rmsnorm(x_ref[...], gpre_ref[...]).astype(BF16)
    xn_ref[...] = xn
    u_ref[...] = _gelu(_dot_nt(xn, wu_ref[...]))
    v_ref[...] = _rmsnorm(_gelu(_dot_nt(xn, wv_ref[...])), gs_ref[...]).astype(BF16)
    rows = lax.broadcasted_iota(jnp.int32, (A_CHUNK, A_CHUNK), 0)
    cols = lax.broadcasted_iota(jnp.int32, (A_CHUNK, A_CHUNK), 1)
    causal = rows >= cols
    for h in range(heads):
        w = jnp.where(causal, ws_ref[h], 0.0).astype(BF16)
        bias = bs_ref[h]
        lanes = pl.ds(h * A_HEAD_DIM, A_HEAD_DIM)
        for c in range(tm // A_CHUNK):
            rws = pl.ds(c * A_CHUNK, A_CHUNK)
            mixed = _dot(w, v_ref[rws, lanes]) + bias
            o_ref[rws, lanes] = (u_ref[rws, lanes] * mixed).astype(BF16)


def _inproj_sgu(x, g_pre, w_uv, g_sgu, w_spatial, b_full):
    t, d = x.shape
    aw = w_uv.shape[0] // 2
    heads = w_spatial.shape[0]
    tm = PROJ_ROW_TILE
    assert t % tm == 0 and tm % A_CHUNK == 0 and aw == heads * A_HEAD_DIM
    const2 = lambda i: (0, 0)
    const3 = lambda i: (0, 0, 0)
    return pl.pallas_call(
        _sgu_body,
        grid=(t // tm,),
        in_specs=[
            pl.BlockSpec((tm, d), lambda i: (i, 0)),
            pl.BlockSpec((1, d), const2),
            pl.BlockSpec((aw, d), const2, pipeline_mode=pl.Buffered(1)),
            pl.BlockSpec((aw, d), lambda i: (1, 0), pipeline_mode=pl.Buffered(1)),
            pl.BlockSpec((1, aw), const2),
            pl.BlockSpec((heads, A_CHUNK, A_CHUNK), const3),
            pl.BlockSpec((heads, A_CHUNK, A_HEAD_DIM), const3),
        ],
        out_specs=[pl.BlockSpec((tm, aw), lambda i: (i, 0)), pl.BlockSpec((tm, d), lambda i: (i, 0))],
        out_shape=[jax.ShapeDtypeStruct((t, aw), BF16), jax.ShapeDtypeStruct((t, d), BF16)],
        scratch_shapes=[pltpu.VMEM((tm, aw), F32), pltpu.VMEM((tm, aw), BF16)],
        compiler_params=_params(("parallel",)),
        name="inproj_sgu",
    )(x, g_pre, w_uv, w_uv, g_sgu, w_spatial, b_full)


Z_OFF = 0
XBC_OFF = B_WIDTH
DT_OFF = B_WIDTH + B_CONV_DIM
ZXD_WIDTH = DT_OFF + LANES
CONV_HALO = SUBLANES


def _split_cols(v, pieces):
    parts = []
    rest = v
    for _ in range(pieces):
        part = rest.astype(BF16)
        parts.append(part)
        rest = rest - part.astype(F32)
    return jnp.concatenate(parts, axis=1)


def _ssd_mix(zxd_ref, convw_ref, convb_ref, dtb_ref, alog_ref, dskip_ref, norm_ref, o_ref,
             ext_ref, state_ref, y_ref, side_work=()):
    cl = B_CHUNK
    side_work = list(side_work)

    def side(n=1):
        for _ in range(n):
            if side_work:
                side_work.pop(0)()

    ext_ref[CONV_HALO:CONV_HALO + cl, :] = zxd_ref[:, XBC_OFF:XBC_OFF + B_CONV_DIM]
    ext = ext_ref[...]
    conv = convw_ref[0:1, :] * ext
    for k in range(1, B_CONV):
        conv = pltpu.roll(conv, 1, axis=0) + convw_ref[k:k + 1, :] * ext
    conv = conv[CONV_HALO:, :] + convb_ref[...]
    ext_ref[0:CONV_HALO, :] = ext_ref[cl:cl + CONV_HALO, :]
    side()
    xbc = _silu(conv)
    xs = xbc[:, :B_WIDTH]

    dt = jax.nn.softplus(zxd_ref[:, DT_OFF:DT_OFF + LANES] + dtb_ref[...])
    da = dt * (-jnp.exp(alog_ref[...]))
    rows = lax.broadcasted_iota(jnp.int32, (cl, cl), 0)
    cols = lax.broadcasted_iota(jnp.int32, (cl, cl), 1)
    causal = rows >= cols
    tril = jnp.where(causal, 1.0, 0.0).astype(BF16)
    acs = sum(_dot(tril, part) for part in _split3(da))
    acs_t = acs.T
    side()

    e_rows = lax.broadcasted_iota(jnp.int32, (SPLIT_PIECES * LANES, B_WIDTH), 0)
    e_cols = lax.broadcasted_iota(jnp.int32, (SPLIT_PIECES * LANES, B_WIDTH), 1)
    expand = jnp.where(e_cols // B_HEAD_DIM == e_rows % LANES, 1.0, 0.0).astype(BF16)
    acs_e = _dot(_split_cols(acs, SPLIT_PIECES), expand)
    dt_e = _dot(_split_cols(dt, SPLIT_PIECES), expand)

    x = xs * dt_e
    even_head = (lax.broadcasted_iota(jnp.int32, (cl, B_WIDTH), 1) // B_HEAD_DIM) % 2 == 0
    x_even = jnp.where(even_head, x, 0.0).astype(BF16)
    x_odd = jnp.where(even_head, 0.0, x).astype(BF16)
    acs_last = acs_e[cl - 1:cl, :]
    decay_from_start = jnp.exp(acs_e)
    x_to_end = (x * jnp.exp(acs_last - acs_e)).astype(BF16)
    chunk_decay = jnp.exp(acs_last)
    side()

    heads_per_group = B_HEADS // B_GROUPS
    for g in range(B_GROUPS):
        gl = slice(g * B_GROUP_WIDTH, (g + 1) * B_GROUP_WIDTH)
        b_off = B_WIDTH + g * B_STATE
        c_off = B_WIDTH + B_GROUPS * B_STATE + g * B_STATE
        bc_t = xbc[:, b_off:b_off + B_STATE].T.astype(BF16)
        cc = xbc[:, c_off:c_off + B_STATE].astype(BF16)
        cb = _dot(cc, bc_t)
        state = state_ref[g]
        y_off = _dot(cc, state.astype(BF16)) * decay_from_start[:, gl]
        state_ref[g] = state * chunk_decay[:, gl] + _dot(bc_t, x_to_end[:, gl])
        for pair in range(heads_per_group // 2):
            head_a = g * heads_per_group + 2 * pair
            pl_off = head_a * B_HEAD_DIM
            y_pair = y_off[:, pair * LANES:(pair + 1) * LANES]
            for head, x_half in ((head_a, x_even), (head_a + 1, x_odd)):
                diff = acs[:, head:head + 1] - acs_t[head:head + 1, :]
                m = (cb * jnp.exp(jnp.where(causal, diff, -jnp.inf))).astype(BF16)
                y_pair = y_pair + _dot(m, x_half[:, pl_off:pl_off + LANES])
            y_ref[:, pl_off:pl_off + LANES] = y_pair
            side()

    side(len(side_work))
    y = (y_ref[...] + dskip_ref[...] * xs) * _silu(zxd_ref[:, Z_OFF:Z_OFF + B_WIDTH])
    for g in range(B_GROUPS):
        gl = slice(g * B_GROUP_WIDTH, (g + 1) * B_GROUP_WIDTH)
        yg = y[:, gl]
        yg = yg * lax.rsqrt(jnp.mean(yg * yg, axis=-1, keepdims=True) + EPS)
        o_ref[:, gl] = (yg * norm_ref[:, gl]).astype(BF16)


def _ssd_body(*refs, nc, n_chunks, casts):
    xn_ref, w_ref, convw_ref, convb_ref, dtb_ref, alog_ref, dskip_ref, norm_ref = refs[:8]
    cast_srcs = refs[8:8 + len(casts)]
    o_ref = refs[8 + len(casts)]
    cast_dsts = refs[9 + len(casts):-5]
    zxd0_ref, zxd1_ref, ext_ref, state_ref, y_ref = refs[-5:]
    s = pl.program_id(0)
    bufs = (zxd0_ref, zxd1_ref)

    def project_pieces(dst_ref):
        def piece(start):
            cols = slice(start, min(start + PROJ_PIECE, ZXD_WIDTH))

            def run():
                dst_ref[:, cols] = _dot(xn_ref[...], w_ref[:, cols])
            return run

        def casts_then(run):
            def both():
                _run_side_casts(s, cast_srcs, cast_dsts, casts)
                run()
            return both
        pieces = [piece(start) for start in range(0, ZXD_WIDTH, PROJ_PIECE)]
        return [casts_then(pieces[0])] + pieces[1:]

    def mix(src_ref, side_work=()):
        _ssd_mix(src_ref, convw_ref, convb_ref, dtb_ref, alog_ref, dskip_ref, norm_ref, o_ref,
                 ext_ref, state_ref, y_ref, side_work)

    @pl.when(lax.rem(s - 1, nc) == 0)
    def _():
        ext_ref[0:CONV_HALO, :] = jnp.zeros((CONV_HALO, B_CONV_DIM), F32)
        state_ref[...] = jnp.zeros_like(state_ref)

    @pl.when(s == 0)
    def _():
        for run in project_pieces(bufs[0]):
            run()

    for parity in range(2):
        @pl.when(jnp.logical_and(jnp.logical_and(s > 0, s < n_chunks), lax.rem(s, 2) == parity))
        def _():
            mix(bufs[1 - parity], project_pieces(bufs[parity]))

    @pl.when(s == n_chunks)
    def _():
        _run_side_casts(s, cast_srcs, cast_dsts, casts)
        mix(bufs[(n_chunks - 1) % 2])


def _inproj_ssd(xn, w_zxd, batch, conv_w, conv_b, dt_bias, a_log, d_skip, ssm_norm, side_casts=()):
    t, d = xn.shape
    seq = t // batch
    assert seq % B_CHUNK == 0 and w_zxd.shape[1] == ZXD_WIDTH
    nc = seq // B_CHUNK
    n_chunks = batch * nc
    const = lambda s: (0, 0)
    cast_in, cast_in_specs, cast_shapes, cast_out_specs, cast_plans = _side_cast_specs(
        side_casts, lambda s: s, n_chunks + 1)
    return pl.pallas_call(
        functools.partial(_ssd_body, nc=nc, n_chunks=n_chunks, casts=cast_plans),
        grid=(n_chunks + 1,),
        in_specs=[
            pl.BlockSpec((B_CHUNK, d), lambda s: (jnp.minimum(s, n_chunks - 1), 0)),
            pl.BlockSpec((d, ZXD_WIDTH), const),
            pl.BlockSpec((B_CONV, B_CONV_DIM), const),
            pl.BlockSpec((1, B_CONV_DIM), const),
            pl.BlockSpec((1, LANES), const),
            pl.BlockSpec((1, LANES), const),
            pl.BlockSpec((1, B_WIDTH), const),
            pl.BlockSpec((1, B_WIDTH), const),
        ] + cast_in_specs,
        out_specs=[pl.BlockSpec((B_CHUNK, B_WIDTH), lambda s: (jnp.maximum(s - 1, 0), 0))] + cast_out_specs,
        out_shape=[jax.ShapeDtypeStruct((t, B_WIDTH), BF16)] + cast_shapes,
        scratch_shapes=[
            pltpu.VMEM((B_CHUNK, ZXD_WIDTH), F32),
            pltpu.VMEM((B_CHUNK, ZXD_WIDTH), F32),
            pltpu.VMEM((CONV_HALO + B_CHUNK, B_CONV_DIM), F32),
            pltpu.VMEM((B_GROUPS, B_STATE, B_GROUP_WIDTH), F32),
            pltpu.VMEM((B_CHUNK, B_WIDTH), F32),
        ],
        compiler_params=_params(("arbitrary",)),
        name="inproj_ssd",
    )(xn, w_zxd, conv_w, conv_b, dt_bias, a_log, d_skip, ssm_norm, *cast_in)


def _outproj_body(x_ref, ya_ref, yb_ref, wa_ref, wb_ref, g_ref, o_ref):
    h = _dot(ya_ref[...], wa_ref[...]) + _dot(yb_ref[...], wb_ref[...])
    o_ref[...] = x_ref[...] + _rmsnorm(h, g_ref[...])


def _outproj(x, y_a, y_b, w_out, g_post):
    t, d = x.shape
    ka, kb = y_a.shape[1], y_b.shape[1]
    tm = PROJ_ROW_TILE
    assert t % tm == 0 and ka == kb and w_out.shape[0] == ka + kb
    const = lambda i: (0, 0)
    return pl.pallas_call(
        _outproj_body,
        grid=(t // tm,),
        in_specs=[
            pl.BlockSpec((tm, d), lambda i: (i, 0)),
            pl.BlockSpec((tm, ka), lambda i: (i, 0)),
            pl.BlockSpec((tm, kb), lambda i: (i, 0)),
            pl.BlockSpec((ka, d), const, pipeline_mode=pl.Buffered(1)),
            pl.BlockSpec((kb, d), lambda i: (1, 0), pipeline_mode=pl.Buffered(1)),
            pl.BlockSpec((1, d), const),
        ],
        out_specs=pl.BlockSpec((tm, d), lambda i: (i, 0)),
        out_shape=jax.ShapeDtypeStruct((t, d), F32),
        compiler_params=_params(("parallel",)),
        name="outproj",
    )(x, y_a, y_b, w_out, w_out, g_post)


def _row(v):
    return v.reshape(1, -1).astype(F32)


def kernel(x, ffn1_norm_pre, ffn1_norm_post, ffn1_w_gate, ffn1_w_up, ffn1_w_down, mix_norm_pre, mix_norm_post, w_in, sgu_norm, w_spatial, b_spatial, conv_w, conv_b, dt_bias, a_log, d_skip, ssm_norm, w_out, ffn2_norm_pre, ffn2_norm_post, ffn2_w_gate, ffn2_w_up, ffn2_w_down):
    batch, seq, d = x.shape
    depth = ffn1_norm_pre.shape[0]
    a_width = sgu_norm.shape[1]
    xf = x.reshape(batch * seq, d)
    for l in range(depth):
        w_in_t = jnp.swapaxes(w_in, 1, 2)
        n_in = w_in.shape[2]
        whole = lambda w: (w, l, 0, w.shape[1], w.shape[1], False)
        head, w1_gate, w1_up, w1_down, w_uv, w_zxd = _ffn_head(
            xf, _row(ffn1_norm_pre[l]), _row(ffn1_norm_post[l]), ffn1_w_gate, ffn1_w_up, ffn1_w_down, l,
            ((w_in_t, l, 0, 2 * a_width, 2 * a_width, False),
             (w_in_t, l, 2 * a_width, n_in - 2 * a_width, ZXD_WIDTH, True)))
        (xf,) = _ffn(xf, _row(ffn1_norm_pre[l]), _row(ffn1_norm_post[l]), w1_gate, w1_up, w1_down, head=head)

        g_mix = _row(mix_norm_pre[l])
        b_full = jnp.broadcast_to(b_spatial[l][:, :, None], b_spatial[l].shape + (A_HEAD_DIM,))
        y_a, xn = _inproj_sgu(xf, g_mix, w_uv, _row(sgu_norm[l]), w_spatial[l], b_full)

        pad_heads = lambda v: jnp.pad(v.astype(F32), (0, LANES - B_HEADS)).reshape(1, LANES)
        y_b, w2_gate, w2_up, w2_down, wo = _inproj_ssd(
            xn, w_zxd, batch, conv_w[l], _row(conv_b[l]), pad_heads(dt_bias[l]), pad_heads(a_log[l]),
            _row(jnp.repeat(d_skip[l], B_HEAD_DIM)), _row(ssm_norm[l]),
            (whole(ffn2_w_gate), whole(ffn2_w_up), whole(ffn2_w_down), whole(w_out)))

        xf = _outproj(xf, y_a, y_b, wo, _row(mix_norm_post[l]))

        (xf,) = _ffn(xf, _row(ffn2_norm_pre[l]), _row(ffn2_norm_post[l]), w2_gate, w2_up, w2_down)
    return xf.reshape(batch, seq, d)
```

```python
import functools

import jax
import jax.numpy as jnp
from jax import lax
from jax.experimental import pallas as pl
from jax.experimental.pallas import tpu as pltpu

F32 = jnp.float32
BF16 = jnp.bfloat16

EPS = 1e-6
HALF_STEP = 0.5
SQRT_HALF = 0.7071067811865476

A_HEAD_DIM = 128
A_CHUNK = 128
B_HEAD_DIM = 64
B_HEADS = 16
B_GROUPS = 2
B_STATE = 128
B_CONV = 4
B_CHUNK = 256
B_WIDTH = B_HEADS * B_HEAD_DIM
B_GROUP_WIDTH = B_WIDTH // B_GROUPS
B_CONV_DIM = B_WIDTH + 2 * B_GROUPS * B_STATE

LANES = 128
SUBLANES = 8
BF16_SUBLANES = 16
VMEM_LIMIT_BYTES = 62 * 1024 * 1024

FFN_ROW_TILE = 1024
FFN_FF_TILE = 768
FFN_FF_SUB = 256
FFN_FINISH_ROWS = 256
FFN_HEAD_FF_TILE = 256
PROJ_ROW_TILE = 1024
SPLIT_PIECES = 2
PROJ_PIECE = 256


def _rmsnorm(x, g):
    return x * lax.rsqrt(jnp.mean(x * x, axis=-1, keepdims=True) + EPS) * g


def _gelu(x):
    return 0.5 * x * (1.0 + lax.erf(x * SQRT_HALF))


def _silu(x):
    return x * jax.nn.sigmoid(x)


def _dot(a, b):
    return jnp.dot(a, b, preferred_element_type=F32)


def _dot_nt(a, b_t):
    return lax.dot_general(a, b_t, (((1,), (1,)), ((), ())), preferred_element_type=F32)


def _split3(v):
    hi = v.astype(BF16)
    r1 = v - hi.astype(F32)
    mid = r1.astype(BF16)
    lo = (r1 - mid.astype(F32)).astype(BF16)
    return hi, mid, lo


def _params(semantics):
    return pltpu.CompilerParams(dimension_semantics=semantics, vmem_limit_bytes=VMEM_LIMIT_BYTES)


def _slab_rows(rows, max_slabs, align):
    for slab in range(align, rows + 1, align):
        if rows % slab == 0 and rows // slab <= max_slabs:
            return slab
    raise ValueError(f"no slab size for {rows} rows in {max_slabs} steps")


def _side_cast_specs(side_casts, flat_step, total_steps):
    inputs, in_specs, out_shapes, out_specs, plans = [], [], [], [], []
    for src, layer, row_start, n_rows, out_rows, transpose in side_casts:
        n_cols = src.shape[2]
        slab_rows = _slab_rows(out_rows, total_steps, LANES if transpose else BF16_SUBLANES)
        assert row_start % slab_rows == 0
        n_real, n_out, first = pl.cdiv(n_rows, slab_rows), out_rows // slab_rows, row_start // slab_rows
        src_slab = lambda *idx, n=n_real, first=first, layer=layer: (
            layer, first + jnp.minimum(flat_step(*idx), n - 1), 0)
        dst_slab = lambda *idx, n=n_out: jnp.minimum(flat_step(*idx), n - 1)
        inputs.append(src)
        in_specs.append(pl.BlockSpec((None, slab_rows, n_cols), src_slab))
        if transpose:
            out_shapes.append(jax.ShapeDtypeStruct((n_cols, out_rows), BF16))
            out_specs.append(pl.BlockSpec((n_cols, slab_rows), lambda *idx, f=dst_slab: (0, f(*idx))))
        else:
            out_shapes.append(jax.ShapeDtypeStruct((out_rows, n_cols), BF16))
            out_specs.append(pl.BlockSpec((slab_rows, n_cols), lambda *idx, f=dst_slab: (f(*idx), 0)))
        plans.append((n_rows, n_out, transpose))
    return inputs, in_specs, out_shapes, out_specs, tuple(plans)


def _run_side_casts(step, src_refs, dst_refs, plans):
    for src_ref, dst_ref, (n_rows, n_slabs, transpose) in zip(src_refs, dst_refs, plans):
        slab = src_ref[...]
        if n_rows != n_slabs * src_ref.shape[0]:
            row = jnp.minimum(step, n_slabs - 1) * src_ref.shape[0] + lax.broadcasted_iota(jnp.int32, slab.shape, 0)
            slab = jnp.where(row < n_rows, slab, 0.0)
        dst_ref[...] = (slab.T if transpose else slab).astype(BF16)


def _swiglu_partial(xn, wg, wu, wd, ff_sub, after_first=None, merge_narrow=False):
    width = wg.shape[1]
    acts = []
    for start in range(0, width, ff_sub):
        cols = slice(start, min(start + ff_sub, width))
        sub = cols.stop - start
        if merge_narrow and 2 * sub <= ff_sub:
            both = _dot(xn, jnp.concatenate([wg[:, cols], wu[:, cols]], axis=1))
            gate, up = both[:, :sub], both[:, sub:]
        else:
            gate, up = _dot(xn, wg[:, cols]), _dot(xn, wu[:, cols])
        acts.append((_silu(gate) * up).astype(BF16))
        if start == 0 and after_first is not None:
            after_first()
    return _dot(jnp.concatenate(acts, axis=1), wd)


def _ffn_body(*refs, n_steps, tail_width, ff_sub, head_piece, casts):
    refs = list(refs)
    x_ref, gpre_ref, gpost_ref, wg_ref, wu_ref, wd_ref = refs[:6]
    del refs[:6]
    head_ref = refs.pop(0) if head_piece else None
    cast_srcs = refs[:len(casts)]
    o_ref = refs[len(casts)]
    cast_dsts = refs[len(casts) + 1:-1]
    xn_ref = refs[-1]
    i = pl.program_id(0)
    j = pl.program_id(1)
    tf = wg_ref.shape[1]
    computed = (i > 0) if head_piece else True

    def partial_out(width, rows=slice(None)):
        def side():
            if rows.start in (None, 0):
                _run_side_casts(i * n_steps + j, cast_srcs, cast_dsts, casts)
        return _swiglu_partial(xn_ref[rows, :], wg_ref[:, :width], wu_ref[:, :width], wd_ref[:width, :], ff_sub, side,
                               merge_narrow=rows.start is not None)

    if head_piece:
        @pl.when(jnp.logical_and(i == 0, j < o_ref.shape[0] // head_piece))
        def _():
            o_ref[pl.ds(pl.multiple_of(j * head_piece, head_piece), head_piece), :] = head_ref[...]

    @pl.when(jnp.logical_and(computed, j == 0))
    def _():
        xn_ref[...] = _rmsnorm(x_ref[...], gpre_ref[...]).astype(BF16)
        o_ref[...] = partial_out(tf)

    @pl.when(jnp.logical_and(computed, jnp.logical_and(j > 0, j < n_steps - 1)))
    def _():
        o_ref[...] += partial_out(tf)

    @pl.when(jnp.logical_and(computed, j == n_steps - 1))
    def _():
        for start in range(0, x_ref.shape[0], FFN_FINISH_ROWS):
            rows = slice(start, start + FFN_FINISH_ROWS)
            h = o_ref[rows, :] + partial_out(tail_width, rows)
            o_ref[rows, :] = x_ref[rows, :] + HALF_STEP * _rmsnorm(h, gpost_ref[...])


def _ffn(x, g_pre, g_post, wg, wu, wd, head=None, side_casts=()):
    t, d = x.shape
    f = wg.shape[1]
    tm, tf, ff_sub = FFN_ROW_TILE, FFN_FF_TILE, FFN_FF_SUB
    n_steps = pl.cdiv(f, tf)
    tail_width = f - (n_steps - 1) * tf
    assert t % tm == 0 and n_steps >= 2 and tail_width % LANES == 0
    row = pl.BlockSpec((tm, d), lambda i, j: (i, 0))
    vec = pl.BlockSpec((1, d), lambda i, j: (0, 0))
    ff_tile = (lambda i, j: jnp.where(i == 0, 0, j)) if head is not None else (lambda i, j: j)
    head_in, head_specs, head_piece = [], [], None
    if head is not None:
        assert head.shape == (tm, d)
        head_piece = _slab_rows(tm, n_steps, SUBLANES)
        n_head = tm // head_piece
        head_in = [head]
        head_specs = [pl.BlockSpec((head_piece, d), lambda i, j: (jnp.where(i == 0, jnp.minimum(j, n_head - 1), n_head - 1), 0))]
    assert head is None or not side_casts
    cast_in, cast_in_specs, cast_shapes, cast_out_specs, cast_plans = _side_cast_specs(
        side_casts, lambda i, j: i * n_steps + j, (t // tm) * n_steps)
    return pl.pallas_call(
        functools.partial(_ffn_body, n_steps=n_steps, tail_width=tail_width, ff_sub=ff_sub,
                          head_piece=head_piece, casts=cast_plans),
        grid=(t // tm, n_steps),
        in_specs=[row, vec, vec,
                  pl.BlockSpec((d, tf), lambda i, j: (0, ff_tile(i, j))),
                  pl.BlockSpec((d, tf), lambda i, j: (0, ff_tile(i, j))),
                  pl.BlockSpec((tf, d), lambda i, j: (ff_tile(i, j), 0))] + head_specs + cast_in_specs,
        out_specs=[row] + cast_out_specs,
        out_shape=[jax.ShapeDtypeStruct((t, d), F32)] + cast_shapes,
        scratch_shapes=[pltpu.VMEM((tm, d), BF16)],
        compiler_params=_params(("arbitrary", "arbitrary") if side_casts else ("parallel", "arbitrary")),
        name="ffn",
    )(x, g_pre, g_post, wg, wu, wd, *head_in, *cast_in)


def _ffn_head_body(*refs, n_steps, tail_width, ff_sub, casts):
    x_ref, gpre_ref, gpost_ref, wg32_ref, wu32_ref, wd32_ref = refs[:6]
    cast_srcs = refs[6:6 + len(casts)]
    o_ref, wg16_ref, wu16_ref, wd16_ref = refs[6 + len(casts):10 + len(casts)]
    cast_dsts = refs[10 + len(casts):-1]
    xn_ref = refs[-1]
    j = pl.program_id(0)
    tf = wg32_ref.shape[1]

    def partial_out(width):
        wg = wg32_ref[:, :width].astype(BF16)
        wu = wu32_ref[:, :width].astype(BF16)
        wd = wd32_ref[:width, :].astype(BF16)
        wg16_ref[:, :width] = wg
        wu16_ref[:, :width] = wu
        wd16_ref[:width, :] = wd
        return _swiglu_partial(xn_ref[...], wg, wu, wd, ff_sub,
                               lambda: _run_side_casts(j, cast_srcs, cast_dsts, casts))

    @pl.when(j == 0)
    def _():
        xn_ref[...] = _rmsnorm(x_ref[...], gpre_ref[...]).astype(BF16)
        o_ref[...] = partial_out(tf)

    @pl.when(jnp.logical_and(j > 0, j < n_steps - 1))
    def _():
        o_ref[...] += partial_out(tf)

    @pl.when(j == n_steps - 1)
    def _():
        h = o_ref[...] + partial_out(tail_width)
        o_ref[...] = x_ref[...] + HALF_STEP * _rmsnorm(h, gpost_ref[...])


def _ffn_head(x, g_pre, g_post, w_gate, w_up, w_down, layer, side_casts=()):
    d = x.shape[1]
    f = w_gate.shape[2]
    tm, tf, ff_sub = FFN_ROW_TILE, FFN_HEAD_FF_TILE, FFN_FF_SUB
    n_steps = pl.cdiv(f, tf)
    tail_width = f - (n_steps - 1) * tf
    assert n_steps >= 2 and tail_width % LANES == 0
    const = lambda j: (0, 0)
    vec = pl.BlockSpec((1, d), const)
    cast_in, cast_in_specs, cast_shapes, cast_out_specs, cast_plans = _side_cast_specs(
        side_casts, lambda j: j, n_steps)
    return pl.pallas_call(
        functools.partial(_ffn_head_body, n_steps=n_steps, tail_width=tail_width, ff_sub=ff_sub, casts=cast_plans),
        grid=(n_steps,),
        in_specs=[pl.BlockSpec((tm, d), const, pipeline_mode=pl.Buffered(1)), vec, vec,
                  pl.BlockSpec((None, d, tf), lambda j: (layer, 0, j)),
                  pl.BlockSpec((None, d, tf), lambda j: (layer, 0, j)),
                  pl.BlockSpec((None, tf, d), lambda j: (layer, j, 0))] + cast_in_specs,
        out_specs=[pl.BlockSpec((tm, d), const),
                   pl.BlockSpec((d, tf), lambda j: (0, j)),
                   pl.BlockSpec((d, tf), lambda j: (0, j)),
                   pl.BlockSpec((tf, d), lambda j: (j, 0))] + cast_out_specs,
        out_shape=[jax.ShapeDtypeStruct((tm, d), F32),
                   jax.ShapeDtypeStruct((d, f), BF16), jax.ShapeDtypeStruct((d, f), BF16),
                   jax.ShapeDtypeStruct((f, d), BF16)] + cast_shapes,
        scratch_shapes=[pltpu.VMEM((tm, d), BF16)],
        compiler_params=_params(("arbitrary",)),
        name="ffn_head",
    )(x, g_pre, g_post, w_gate, w_up, w_down, *cast_in)


def _sgu_body(x_ref, gpre_ref, wu_ref, wv_ref, gs_ref, ws_ref, bs_ref, o_ref, xn_ref, u_ref, v_ref):
    tm = x_ref.shape[0]
    heads = ws_ref.shape[0]
    xn = _rmsnorm(x_ref[...], gpre_ref[...]).astype(BF16)
    xn_ref[...] = xn
    u_ref[...] = _gelu(_dot_nt(xn, wu_ref[...]))
    v_ref[...] = _rmsnorm(_gelu(_dot_nt(xn, wv_ref[...])), gs_ref[...]).astype(BF16)
    rows = lax.broadcasted_iota(jnp.int32, (A_CHUNK, A_CHUNK), 0)
    cols = lax.broadcasted_iota(jnp.int32, (A_CHUNK, A_CHUNK), 1)
    causal = rows >= cols
    for h in range(heads):
        w = jnp.where(causal, ws_ref[h], 0.0).astype(BF16)
        bias = bs_ref[h]
        lanes = pl.ds(h * A_HEAD_DIM, A_HEAD_DIM)
        for c in range(tm // A_CHUNK):
            rws = pl.ds(c * A_CHUNK, A_CHUNK)
            mixed = _dot(w, v_ref[rws, lanes]) + bias
            o_ref[rws, lanes] = (u_ref[rws, lanes] * mixed).astype(BF16)


def _inproj_sgu(x, g_pre, w_uv, g_sgu, w_spatial, b_full):
    t, d = x.shape
    aw = w_uv.shape[0] // 2
    heads = w_spatial.shape[0]
    tm = PROJ_ROW_TILE
    assert t % tm == 0 and tm % A_CHUNK == 0 and aw == heads * A_HEAD_DIM
    const2 = lambda i: (0, 0)
    const3 = lambda i: (0, 0, 0)
    return pl.pallas_call(
        _sgu_body,
        grid=(t // tm,),
        in_specs=[
            pl.BlockSpec((tm, d), lambda i: (i, 0)),
            pl.BlockSpec((1, d), const2),
            pl.BlockSpec((aw, d), const2, pipeline_mode=pl.Buffered(1)),
            pl.BlockSpec((aw, d), lambda i: (1, 0), pipeline_mode=pl.Buffered(1)),
            pl.BlockSpec((1, aw), const2),
            pl.BlockSpec((heads, A_CHUNK, A_CHUNK), const3),
            pl.BlockSpec((heads, A_CHUNK, A_HEAD_DIM), const3),
        ],
        out_specs=[pl.BlockSpec((tm, aw), lambda i: (i, 0)), pl.BlockSpec((tm, d), lambda i: (i, 0))],
        out_shape=[jax.ShapeDtypeStruct((t, aw), BF16), jax.ShapeDtypeStruct((t, d), BF16)],
        scratch_shapes=[pltpu.VMEM((tm, aw), F32), pltpu.VMEM((tm, aw), BF16)],
        compiler_params=_params(("parallel",)),
        name="inproj_sgu",
    )(x, g_pre, w_uv, w_uv, g_sgu, w_spatial, b_full)


Z_OFF = 0
XBC_OFF = B_WIDTH
DT_OFF = B_WIDTH + B_CONV_DIM
ZXD_WIDTH = DT_OFF + LANES
CONV_HALO = SUBLANES


def _split_cols(v, pieces):
    parts = []
    rest = v
    for _ in range(pieces):
        part = rest.astype(BF16)
        parts.append(part)
        rest = rest - part.astype(F32)
    return jnp.concatenate(parts, axis=1)


def _ssd_mix(zxd_ref, convw_ref, convb_ref, dtb_ref, alog_ref, dskip_ref, norm_ref, o_ref,
             ext_ref, state_ref, y_ref, side_work=()):
    cl = B_CHUNK
    side_work = list(side_work)

    def side(n=1):
        for _ in range(n):
            if side_work:
                side_work.pop(0)()

    ext_ref[CONV_HALO:CONV_HALO + cl, :] = zxd_ref[:, XBC_OFF:XBC_OFF + B_CONV_DIM]
    ext = ext_ref[...]
    conv = convw_ref[0:1, :] * ext
    for k in range(1, B_CONV):
        conv = pltpu.roll(conv, 1, axis=0) + convw_ref[k:k + 1, :] * ext
    conv = conv[CONV_HALO:, :] + convb_ref[...]
    ext_ref[0:CONV_HALO, :] = ext_ref[cl:cl + CONV_HALO, :]
    side()
    xbc = _silu(conv)
    xs = xbc[:, :B_WIDTH]

    dt = jax.nn.softplus(zxd_ref[:, DT_OFF:DT_OFF + LANES] + dtb_ref[...])
    da = dt * (-jnp.exp(alog_ref[...]))
    rows = lax.broadcasted_iota(jnp.int32, (cl, cl), 0)
    cols = lax.broadcasted_iota(jnp.int32, (cl, cl), 1)
    causal = rows >= cols
    tril = jnp.where(causal, 1.0, 0.0).astype(BF16)
    acs = sum(_dot(tril, part) for part in _split3(da))
    acs_t = acs.T
    side()

    e_rows = lax.broadcasted_iota(jnp.int32, (SPLIT_PIECES * LANES, B_WIDTH), 0)
    e_cols = lax.broadcasted_iota(jnp.int32, (SPLIT_PIECES * LANES, B_WIDTH), 1)
    expand = jnp.where(e_cols // B_HEAD_DIM == e_rows % LANES, 1.0, 0.0).astype(BF16)
    acs_e = _dot(_split_cols(acs, SPLIT_PIECES), expand)
    dt_e = _dot(_split_cols(dt, SPLIT_PIECES), expand)

    x = xs * dt_e
    even_head = (lax.broadcasted_iota(jnp.int32, (cl, B_WIDTH), 1) // B_HEAD_DIM) % 2 == 0
    x_even = jnp.where(even_head, x, 0.0).astype(BF16)
    x_odd = jnp.where(even_head, 0.0, x).astype(BF16)
    acs_last = acs_e[cl - 1:cl, :]
    decay_from_start = jnp.exp(acs_e)
    x_to_end = (x * jnp.exp(acs_last - acs_e)).astype(BF16)
    chunk_decay = jnp.exp(acs_last)
    side()

    heads_per_group = B_HEADS // B_GROUPS
    for g in range(B_GROUPS):
        gl = slice(g * B_GROUP_WIDTH, (g + 1) * B_GROUP_WIDTH)
        b_off = B_WIDTH + g * B_STATE
        c_off = B_WIDTH + B_GROUPS * B_STATE + g * B_STATE
        bc_t = xbc[:, b_off:b_off + B_STATE].T.astype(BF16)
        cc = xbc[:, c_off:c_off + B_STATE].astype(BF16)
        cb = _dot(cc, bc_t)
        state = state_ref[g]
        y_off = _dot(cc, state.astype(BF16)) * decay_from_start[:, gl]
        state_ref[g] = state * chunk_decay[:, gl] + _dot(bc_t, x_to_end[:, gl])
        for pair in range(heads_per_group // 2):
            head_a = g * heads_per_group + 2 * pair
            pl_off = head_a * B_HEAD_DIM
            y_pair = y_off[:, pair * LANES:(pair + 1) * LANES]
            for head, x_half in ((head_a, x_even), (head_a + 1, x_odd)):
                diff = acs[:, head:head + 1] - acs_t[head:head + 1, :]
                m = (cb * jnp.exp(jnp.where(causal, diff, -jnp.inf))).astype(BF16)
                y_pair = y_pair + _dot(m, x_half[:, pl_off:pl_off + LANES])
            y_ref[:, pl_off:pl_off + LANES] = y_pair
            side()

    side(len(side_work))
    y = (y_ref[...] + dskip_ref[...] * xs) * _silu(zxd_ref[:, Z_OFF:Z_OFF + B_WIDTH])
    for g in range(B_GROUPS):
        gl = slice(g * B_GROUP_WIDTH, (g + 1) * B_GROUP_WIDTH)
        yg = y[:, gl]
        yg = yg * lax.rsqrt(jnp.mean(yg * yg, axis=-1, keepdims=True) + EPS)
        o_ref[:, gl] = (yg * norm_ref[:, gl]).astype(BF16)


def _ssd_body(*refs, nc, n_chunks, casts):
    xn_ref, w_ref, convw_ref, convb_ref, dtb_ref, alog_ref, dskip_ref, norm_ref = refs[:8]
    cast_srcs = refs[8:8 + len(casts)]
    o_ref = refs[8 + len(casts)]
    cast_dsts = refs[9 + len(casts):-5]
    zxd0_ref, zxd1_ref, ext_ref, state_ref, y_ref = refs[-5:]
    s = pl.program_id(0)
    bufs = (zxd0_ref, zxd1_ref)

    def project_pieces(dst_ref):
        def piece(start):
            cols = slice(start, min(start + PROJ_PIECE, ZXD_WIDTH))

            def run():
                dst_ref[:, cols] = _dot(xn_ref[...], w_ref[:, cols])
            return run

        def casts_then(run):
            def both():
                _run_side_casts(s, cast_srcs, cast_dsts, casts)
                run()
            return both
        pieces = [piece(start) for start in range(0, ZXD_WIDTH, PROJ_PIECE)]
        return [casts_then(pieces[0])] + pieces[1:]

    def mix(src_ref, side_work=()):
        _ssd_mix(src_ref, convw_ref, convb_ref, dtb_ref, alog_ref, dskip_ref, norm_ref, o_ref,
                 ext_ref, state_ref, y_ref, side_work)

    @pl.when(lax.rem(s - 1, nc) == 0)
    def _():
        ext_ref[0:CONV_HALO, :] = jnp.zeros((CONV_HALO, B_CONV_DIM), F32)
        state_ref[...] = jnp.zeros_like(state_ref)

    @pl.when(s == 0)
    def _():
        for run in project_pieces(bufs[0]):
            run()

    for parity in range(2):
        @pl.when(jnp.logical_and(jnp.logical_and(s > 0, s < n_chunks), lax.rem(s, 2) == parity))
        def _():
            mix(bufs[1 - parity], project_pieces(bufs[parity]))

    @pl.when(s == n_chunks)
    def _():
        _run_side_casts(s, cast_srcs, cast_dsts, casts)
        mix(bufs[(n_chunks - 1) % 2])


def _inproj_ssd(xn, w_zxd, batch, conv_w, conv_b, dt_bias, a_log, d_skip, ssm_norm, side_casts=()):
    t, d = xn.shape
    seq = t // batch
    assert seq % B_CHUNK == 0 and w_zxd.shape[1] == ZXD_WIDTH
    nc = seq // B_CHUNK
    n_chunks = batch * nc
    const = lambda s: (0, 0)
    cast_in, cast_in_specs, cast_shapes, cast_out_specs, cast_plans = _side_cast_specs(
        side_casts, lambda s: s, n_chunks + 1)
    return pl.pallas_call(
        functools.partial(_ssd_body, nc=nc, n_chunks=n_chunks, casts=cast_plans),
        grid=(n_chunks + 1,),
        in_specs=[
            pl.BlockSpec((B_CHUNK, d), lambda s: (jnp.minimum(s, n_chunks - 1), 0)),
            pl.BlockSpec((d, ZXD_WIDTH), const),
            pl.BlockSpec((B_CONV, B_CONV_DIM), const),
            pl.BlockSpec((1, B_CONV_DIM), const),
            pl.BlockSpec((1, LANES), const),
            pl.BlockSpec((1, LANES), const),
            pl.BlockSpec((1, B_WIDTH), const),
            pl.BlockSpec((1, B_WIDTH), const),
        ] + cast_in_specs,
        out_specs=[pl.BlockSpec((B_CHUNK, B_WIDTH), lambda s: (jnp.maximum(s - 1, 0), 0))] + cast_out_specs,
        out_shape=[jax.ShapeDtypeStruct((t, B_WIDTH), BF16)] + cast_shapes,
        scratch_shapes=[
            pltpu.VMEM((B_CHUNK, ZXD_WIDTH), F32),
            pltpu.VMEM((B_CHUNK, ZXD_WIDTH), F32),
            pltpu.VMEM((CONV_HALO + B_CHUNK, B_CONV_DIM), F32),
            pltpu.VMEM((B_GROUPS, B_STATE, B_GROUP_WIDTH), F32),
            pltpu.VMEM((B_CHUNK, B_WIDTH), F32),
        ],
        compiler_params=_params(("arbitrary",)),
        name="inproj_ssd",
    )(xn, w_zxd, conv_w, conv_b, dt_bias, a_log, d_skip, ssm_norm, *cast_in)


def _outproj_body(x_ref, ya_ref, yb_ref, wa_ref, wb_ref, g_ref, o_ref):
    h = _dot(ya_ref[...], wa_ref[...]) + _dot(yb_ref[...], wb_ref[...])
    o_ref[...] = x_ref[...] + _rmsnorm(h, g_ref[...])


def _outproj(x, y_a, y_b, w_out, g_post):
    t, d = x.shape
    ka, kb = y_a.shape[1], y_b.shape[1]
    tm = PROJ_ROW_TILE
    assert t % tm == 0 and ka == kb and w_out.shape[0] == ka + kb
    const = lambda i: (0, 0)
    return pl.pallas_call(
        _outproj_body,
        grid=(t // tm,),
        in_specs=[
            pl.BlockSpec((tm, d), lambda i: (i, 0)),
            pl.BlockSpec((tm, ka), lambda i: (i, 0)),
            pl.BlockSpec((tm, kb), lambda i: (i, 0)),
            pl.BlockSpec((ka, d), const, pipeline_mode=pl.Buffered(1)),
            pl.BlockSpec((kb, d), lambda i: (1, 0), pipeline_mode=pl.Buffered(1)),
            pl.BlockSpec((1, d), const),
        ],
        out_specs=pl.BlockSpec((tm, d), lambda i: (i, 0)),
        out_shape=jax.ShapeDtypeStruct((t, d), F32),
        compiler_params=_params(("parallel",)),
        name="outproj",
    )(x, y_a, y_b, w_out, w_out, g_post)


def _row(v):
    return v.reshape(1, -1).astype(F32)


def kernel(x, ffn1_norm_pre, ffn1_norm_post, ffn1_w_gate, ffn1_w_up, ffn1_w_down, mix_norm_pre, mix_norm_post, w_in, sgu_norm, w_spatial, b_spatial, conv_w, conv_b, dt_bias, a_log, d_skip, ssm_norm, w_out, ffn2_norm_pre, ffn2_norm_post, ffn2_w_gate, ffn2_w_up, ffn2_w_down):
    batch, seq, d = x.shape
    depth = ffn1_norm_pre.shape[0]
    a_width = sgu_norm.shape[1]
    xf = x.reshape(batch * seq, d)
    for l in range(depth):
        w_in_t = jnp.swapaxes(w_in, 1, 2)
        n_in = w_in.shape[2]
        whole = lambda w: (w, l, 0, w.shape[1], w.shape[1], False)
        head, w1_gate, w1_up, w1_down, w_uv, w_zxd = _ffn_head(
            xf, _row(ffn1_norm_pre[l]), _row(ffn1_norm_post[l]), ffn1_w_gate, ffn1_w_up, ffn1_w_down, l,
            ((w_in_t, l, 0, 2 * a_width, 2 * a_width, False),
             (w_in_t, l, 2 * a_width, n_in - 2 * a_width, ZXD_WIDTH, True)))
        (xf,) = _ffn(xf, _row(ffn1_norm_pre[l]), _row(ffn1_norm_post[l]), w1_gate, w1_up, w1_down, head=head)

        g_mix = _row(mix_norm_pre[l])
        b_full = jnp.broadcast_to(b_spatial[l][:, :, None], b_spatial[l].shape + (A_HEAD_DIM,))
        y_a, xn = _inproj_sgu(xf, g_mix, w_uv, _row(sgu_norm[l]), w_spatial[l], b_full)

        pad_heads = lambda v: jnp.pad(v.astype(F32), (0, LANES - B_HEADS)).reshape(1, LANES)
        y_b, w2_gate, w2_up, w2_down, wo = _inproj_ssd(
            xn, w_zxd, batch, conv_w[l], _row(conv_b[l]), pad_heads(dt_bias[l]), pad_heads(a_log[l]),
            _row(jnp.repeat(d_skip[l], B_HEAD_DIM)), _row(ssm_norm[l]),
            (whole(ffn2_w_gate), whole(ffn2_w_up), whole(ffn2_w_down), whole(w_out)))

        xf = _outproj(xf, y_a, y_b, wo, _row(mix_norm_post[l]))

        (xf,) = _ffn(xf, _row(ffn2_norm_pre[l]), _row(ffn2_norm_post[l]), w2_gate, w2_up, w2_down)
    return xf.reshape(batch, seq, d)
```

```python
import functools

import jax
import jax.numpy as jnp
from jax import lax
from jax.experimental import pallas as pl
from jax.experimental.pallas import tpu as pltpu

F32 = jnp.float32
BF16 = jnp.bfloat16

EPS = 1e-6
HALF_STEP = 0.5
SQRT_HALF = 0.7071067811865476

A_HEAD_DIM = 128
A_CHUNK = 128
B_HEAD_DIM = 64
B_HEADS = 16
B_GROUPS = 2
B_STATE = 128
B_CONV = 4
B_CHUNK = 256
B_WIDTH = B_HEADS * B_HEAD_DIM
B_GROUP_WIDTH = B_WIDTH // B_GROUPS
B_CONV_DIM = B_WIDTH + 2 * B_GROUPS * B_STATE

LANES = 128
SUBLANES = 8
BF16_SUBLANES = 16
VMEM_LIMIT_BYTES = 60 * 1024 * 1024

FFN_ROW_TILE = 1024
FFN_FF_TILE = 512
FFN_FF_SUB = 256
FFN_FINISH_ROWS = 256
FFN_HEAD_FF_TILE = 256
PROJ_ROW_TILE = 1024
SPLIT_PIECES = 2
PROJ_PIECE = 256


def _rmsnorm(x, g):
    return x * lax.rsqrt(jnp.mean(x * x, axis=-1, keepdims=True) + EPS) * g


def _gelu(x):
    return 0.5 * x * (1.0 + lax.erf(x * SQRT_HALF))


def _silu(x):
    return x * jax.nn.sigmoid(x)


def _dot(a, b):
    return jnp.dot(a, b, preferred_element_type=F32)


def _dot_nt(a, b_t):
    return lax.dot_general(a, b_t, (((1,), (1,)), ((), ())), preferred_element_type=F32)


def _split3(v):
    hi = v.astype(BF16)
    r1 = v - hi.astype(F32)
    mid = r1.astype(BF16)
    lo = (r1 - mid.astype(F32)).astype(BF16)
    return hi, mid, lo


def _params(semantics):
    return pltpu.CompilerParams(dimension_semantics=semantics, vmem_limit_bytes=VMEM_LIMIT_BYTES)


def _slab_rows(rows, max_slabs, align):
    for slab in range(align, rows + 1, align):
        if rows % slab == 0 and rows // slab <= max_slabs:
            return slab
    raise ValueError(f"no slab size for {rows} rows in {max_slabs} steps")


def _side_cast_specs(side_casts, flat_step, total_steps):
    inputs, in_specs, out_shapes, out_specs, plans = [], [], [], [], []
    for src, layer, row_start, n_rows, out_rows, transpose in side_casts:
        n_cols = src.shape[2]
        slab_rows = _slab_rows(out_rows, total_steps, LANES if transpose else BF16_SUBLANES)
        assert row_start % slab_rows == 0
        n_real, n_out, first = pl.cdiv(n_rows, slab_rows), out_rows // slab_rows, row_start // slab_rows
        src_slab = lambda *idx, n=n_real, first=first, layer=layer: (
            layer, first + jnp.minimum(flat_step(*idx), n - 1), 0)
        dst_slab = lambda *idx, n=n_out: jnp.minimum(flat_step(*idx), n - 1)
        inputs.append(src)
        in_specs.append(pl.BlockSpec((None, slab_rows, n_cols), src_slab))
        if transpose:
            out_shapes.append(jax.ShapeDtypeStruct((n_cols, out_rows), BF16))
            out_specs.append(pl.BlockSpec((n_cols, slab_rows), lambda *idx, f=dst_slab: (0, f(*idx))))
        else:
            out_shapes.append(jax.ShapeDtypeStruct((out_rows, n_cols), BF16))
            out_specs.append(pl.BlockSpec((slab_rows, n_cols), lambda *idx, f=dst_slab: (f(*idx), 0)))
        plans.append((n_rows, n_out, transpose))
    return inputs, in_specs, out_shapes, out_specs, tuple(plans)


def _run_side_casts(step, src_refs, dst_refs, plans):
    for src_ref, dst_ref, (n_rows, n_slabs, transpose) in zip(src_refs, dst_refs, plans):
        slab = src_ref[...]
        if n_rows != n_slabs * src_ref.shape[0]:
            row = jnp.minimum(step, n_slabs - 1) * src_ref.shape[0] + lax.broadcasted_iota(jnp.int32, slab.shape, 0)
            slab = jnp.where(row < n_rows, slab, 0.0)
        dst_ref[...] = (slab.T if transpose else slab).astype(BF16)


def _swiglu_partial(xn, wg, wu, wd, ff_sub, after_first=None, merge_narrow=False):
    width = wg.shape[1]
    acts = []
    for start in range(0, width, ff_sub):
        cols = slice(start, min(start + ff_sub, width))
        sub = cols.stop - start
        if merge_narrow and 2 * sub <= ff_sub:
            both = _dot(xn, jnp.concatenate([wg[:, cols], wu[:, cols]], axis=1))
            gate, up = both[:, :sub], both[:, sub:]
        else:
            gate, up = _dot(xn, wg[:, cols]), _dot(xn, wu[:, cols])
        acts.append((_silu(gate) * up).astype(BF16))
        if start == 0 and after_first is not None:
            after_first()
    return _dot(jnp.concatenate(acts, axis=1), wd)


def _ffn_body(*refs, n_steps, tail_width, ff_sub, head_piece, casts):
    refs = list(refs)
    x_ref, gpre_ref, gpost_ref, wg_ref, wu_ref, wd_ref = refs[:6]
    del refs[:6]
    head_ref = refs.pop(0) if head_piece else None
    cast_srcs = refs[:len(casts)]
    o_ref = refs[len(casts)]
    cast_dsts = refs[len(casts) + 1:-1]
    xn_ref = refs[-1]
    i = pl.program_id(0)
    j = pl.program_id(1)
    tf = wg_ref.shape[1]
    computed = (i > 0) if head_piece else True

    def partial_out(width, rows=slice(None)):
        def side():
            if rows.start in (None, 0):
                _run_side_casts(i * n_steps + j, cast_srcs, cast_dsts, casts)
        return _swiglu_partial(xn_ref[rows, :], wg_ref[:, :width], wu_ref[:, :width], wd_ref[:width, :], ff_sub, side,
                               merge_narrow=rows.start is not None)

    if head_piece:
        @pl.when(jnp.logical_and(i == 0, j < o_ref.shape[0] // head_piece))
        def _():
            o_ref[pl.ds(pl.multiple_of(j * head_piece, head_piece), head_piece), :] = head_ref[...]

    @pl.when(jnp.logical_and(computed, j == 0))
    def _():
        xn_ref[...] = _rmsnorm(x_ref[...], gpre_ref[...]).astype(BF16)
        o_ref[...] = partial_out(tf)

    @pl.when(jnp.logical_and(computed, jnp.logical_and(j > 0, j < n_steps - 1)))
    def _():
        o_ref[...] += partial_out(tf)

    @pl.when(jnp.logical_and(computed, j == n_steps - 1))
    def _():
        for start in range(0, x_ref.shape[0], FFN_FINISH_ROWS):
            rows = slice(start, start + FFN_FINISH_ROWS)
            h = o_ref[rows, :] + partial_out(tail_width, rows)
            o_ref[rows, :] = x_ref[rows, :] + HALF_STEP * _rmsnorm(h, gpost_ref[...])


def _ffn(x, g_pre, g_post, wg, wu, wd, head=None, side_casts=()):
    t, d = x.shape
    f = wg.shape[1]
    tm, tf, ff_sub = FFN_ROW_TILE, FFN_FF_TILE, FFN_FF_SUB
    n_steps = pl.cdiv(f, tf)
    tail_width = f - (n_steps - 1) * tf
    assert t % tm == 0 and n_steps >= 2 and tail_width % LANES == 0
    row = pl.BlockSpec((tm, d), lambda i, j: (i, 0))
    vec = pl.BlockSpec((1, d), lambda i, j: (0, 0))
    ff_tile = (lambda i, j: jnp.where(i == 0, 0, j)) if head is not None else (lambda i, j: j)
    head_in, head_specs, head_piece = [], [], None
    if head is not None:
        assert head.shape == (tm, d)
        head_piece = _slab_rows(tm, n_steps, SUBLANES)
        n_head = tm // head_piece
        head_in = [head]
        head_specs = [pl.BlockSpec((head_piece, d), lambda i, j: (jnp.where(i == 0, jnp.minimum(j, n_head - 1), n_head - 1), 0))]
    assert head is None or not side_casts
    cast_in, cast_in_specs, cast_shapes, cast_out_specs, cast_plans = _side_cast_specs(
        side_casts, lambda i, j: i * n_steps + j, (t // tm) * n_steps)
    return pl.pallas_call(
        functools.partial(_ffn_body, n_steps=n_steps, tail_width=tail_width, ff_sub=ff_sub,
                          head_piece=head_piece, casts=cast_plans),
        grid=(t // tm, n_steps),
        in_specs=[row, vec, vec,
                  pl.BlockSpec((d, tf), lambda i, j: (0, ff_tile(i, j))),
                  pl.BlockSpec((d, tf), lambda i, j: (0, ff_tile(i, j))),
                  pl.BlockSpec((tf, d), lambda i, j: (ff_tile(i, j), 0))] + head_specs + cast_in_specs,
        out_specs=[row] + cast_out_specs,
        out_shape=[jax.ShapeDtypeStruct((t, d), F32)] + cast_shapes,
        scratch_shapes=[pltpu.VMEM((tm, d), BF16)],
        compiler_params=_params(("arbitrary", "arbitrary") if side_casts else ("parallel", "arbitrary")),
        name="ffn",
    )(x, g_pre, g_post, wg, wu, wd, *head_in, *cast_in)


def _ffn_head_body(*refs, n_steps, tail_width, ff_sub, casts):
    x_ref, gpre_ref, gpost_ref, wg32_ref, wu32_ref, wd32_ref = refs[:6]
    cast_srcs = refs[6:6 + len(casts)]
    o_ref, wg16_ref, wu16_ref, wd16_ref = refs[6 + len(casts):10 + len(casts)]
    cast_dsts = refs[10 + len(casts):-1]
    xn_ref = refs[-1]
    j = pl.program_id(0)
    tf = wg32_ref.shape[1]

    def partial_out(width):
        wg = wg32_ref[:, :width].astype(BF16)
        wu = wu32_ref[:, :width].astype(BF16)
        wd = wd32_ref[:width, :].astype(BF16)
        wg16_ref[:, :width] = wg
        wu16_ref[:, :width] = wu
        wd16_ref[:width, :] = wd
        return _swiglu_partial(xn_ref[...], wg, wu, wd, ff_sub,
                               lambda: _run_side_casts(j, cast_srcs, cast_dsts, casts))

    @pl.when(j == 0)
    def _():
        xn_ref[...] = _rmsnorm(x_ref[...], gpre_ref[...]).astype(BF16)
        o_ref[...] = partial_out(tf)

    @pl.when(jnp.logical_and(j > 0, j < n_steps - 1))
    def _():
        o_ref[...] += partial_out(tf)

    @pl.when(j == n_steps - 1)
    def _():
        h = o_ref[...] + partial_out(tail_width)
        o_ref[...] = x_ref[...] + HALF_STEP * _rmsnorm(h, gpost_ref[...])


def _ffn_head(x, g_pre, g_post, w_gate, w_up, w_down, layer, side_casts=()):
    d = x.shape[1]
    f = w_gate.shape[2]
    tm, tf, ff_sub = FFN_ROW_TILE, FFN_HEAD_FF_TILE, FFN_FF_SUB
    n_steps = pl.cdiv(f, tf)
    tail_width = f - (n_steps - 1) * tf
    assert n_steps >= 2 and tail_width % LANES == 0
    const = lambda j: (0, 0)
    vec = pl.BlockSpec((1, d), const)
    cast_in, cast_in_specs, cast_shapes, cast_out_specs, cast_plans = _side_cast_specs(
        side_casts, lambda j: j, n_steps)
    return pl.pallas_call(
        functools.partial(_ffn_head_body, n_steps=n_steps, tail_width=tail_width, ff_sub=ff_sub, casts=cast_plans),
        grid=(n_steps,),
        in_specs=[pl.BlockSpec((tm, d), const, pipeline_mode=pl.Buffered(1)), vec, vec,
                  pl.BlockSpec((None, d, tf), lambda j: (layer, 0, j)),
                  pl.BlockSpec((None, d, tf), lambda j: (layer, 0, j)),
                  pl.BlockSpec((None, tf, d), lambda j: (layer, j, 0))] + cast_in_specs,
        out_specs=[pl.BlockSpec((tm, d), const),
                   pl.BlockSpec((d, tf), lambda j: (0, j)),
                   pl.BlockSpec((d, tf), lambda j: (0, j)),
                   pl.BlockSpec((tf, d), lambda j: (j, 0))] + cast_out_specs,
        out_shape=[jax.ShapeDtypeStruct((tm, d), F32),
                   jax.ShapeDtypeStruct((d, f), BF16), jax.ShapeDtypeStruct((d, f), BF16),
                   jax.ShapeDtypeStruct((f, d), BF16)] + cast_shapes,
        scratch_shapes=[pltpu.VMEM((tm, d), BF16)],
        compiler_params=_params(("arbitrary",)),
        name="ffn_head",
    )(x, g_pre, g_post, w_gate, w_up, w_down, *cast_in)


def _sgu_body(x_ref, gpre_ref, wu_ref, wv_ref, gs_ref, ws_ref, bs_ref, o_ref, xn_ref, u_ref, v_ref):
    tm = x_ref.shape[0]
    heads = ws_ref.shape[0]
    xn = _rmsnorm(x_ref[...], gpre_ref[...]).astype(BF16)
    xn_ref[...] = xn
    u_ref[...] = _gelu(_dot_nt(xn, wu_ref[...]))
    v_ref[...] = _rmsnorm(_gelu(_dot_nt(xn, wv_ref[...])), gs_ref[...]).astype(BF16)
    rows = lax.broadcasted_iota(jnp.int32, (A_CHUNK, A_CHUNK), 0)
    cols = lax.broadcasted_iota(jnp.int32, (A_CHUNK, A_CHUNK), 1)
    causal = rows >= cols
    for h in range(heads):
        w = jnp.where(causal, ws_ref[h], 0.0).astype(BF16)
        bias = bs_ref[h]
        lanes = pl.ds(h * A_HEAD_DIM, A_HEAD_DIM)
        for c in range(tm // A_CHUNK):
            rws = pl.ds(c * A_CHUNK, A_CHUNK)
            mixed = _dot(w, v_ref[rws, lanes]) + bias
            o_ref[rws, lanes] = (u_ref[rws, lanes] * mixed).astype(BF16)


def _inproj_sgu(x, g_pre, w_uv, g_sgu, w_spatial, b_full):
    t, d = x.shape
    aw = w_uv.shape[0] // 2
    heads = w_spatial.shape[0]
    tm = PROJ_ROW_TILE
    assert t % tm == 0 and tm % A_CHUNK == 0 and aw == heads * A_HEAD_DIM
    const2 = lambda i: (0, 0)
    const3 = lambda i: (0, 0, 0)
    return pl.pallas_call(
        _sgu_body,
        grid=(t // tm,),
        in_specs=[
            pl.BlockSpec((tm, d), lambda i: (i, 0)),
            pl.BlockSpec((1, d), const2),
            pl.BlockSpec((aw, d), const2, pipeline_mode=pl.Buffered(1)),
            pl.BlockSpec((aw, d), lambda i: (1, 0), pipeline_mode=pl.Buffered(1)),
            pl.BlockSpec((1, aw), const2),
            pl.BlockSpec((heads, A_CHUNK, A_CHUNK), const3),
            pl.BlockSpec((heads, A_CHUNK, A_HEAD_DIM), const3),
        ],
        out_specs=[pl.BlockSpec((tm, aw), lambda i: (i, 0)), pl.BlockSpec((tm, d), lambda i: (i, 0))],
        out_shape=[jax.ShapeDtypeStruct((t, aw), BF16), jax.ShapeDtypeStruct((t, d), BF16)],
        scratch_shapes=[pltpu.VMEM((tm, aw), F32), pltpu.VMEM((tm, aw), BF16)],
        compiler_params=_params(("parallel",)),
        name="inproj_sgu",
    )(x, g_pre, w_uv, w_uv, g_sgu, w_spatial, b_full)


Z_OFF = 0
XBC_OFF = B_WIDTH
DT_OFF = B_WIDTH + B_CONV_DIM
ZXD_WIDTH = DT_OFF + LANES
CONV_HALO = SUBLANES


def _split_cols(v, pieces):
    parts = []
    rest = v
    for _ in range(pieces):
        part = rest.astype(BF16)
        parts.append(part)
        rest = rest - part.astype(F32)
    return jnp.concatenate(parts, axis=1)


def _ssd_mix(zxd_ref, convw_ref, convb_ref, dtb_ref, alog_ref, dskip_ref, norm_ref, o_ref,
             ext_ref, state_ref, y_ref, side_work=()):
    cl = B_CHUNK
    side_work = list(side_work)

    def side(n=1):
        for _ in range(n):
            if side_work:
                side_work.pop(0)()

    ext_ref[CONV_HALO:CONV_HALO + cl, :] = zxd_ref[:, XBC_OFF:XBC_OFF + B_CONV_DIM]
    ext = ext_ref[...]
    conv = convw_ref[0:1, :] * ext
    for k in range(1, B_CONV):
        conv = pltpu.roll(conv, 1, axis=0) + convw_ref[k:k + 1, :] * ext
    conv = conv[CONV_HALO:, :] + convb_ref[...]
    ext_ref[0:CONV_HALO, :] = ext_ref[cl:cl + CONV_HALO, :]
    side()
    xbc = _silu(conv)
    xs = xbc[:, :B_WIDTH]

    dt = jax.nn.softplus(zxd_ref[:, DT_OFF:DT_OFF + LANES] + dtb_ref[...])
    da = dt * (-jnp.exp(alog_ref[...]))
    rows = lax.broadcasted_iota(jnp.int32, (cl, cl), 0)
    cols = lax.broadcasted_iota(jnp.int32, (cl, cl), 1)
    causal = rows >= cols
    tril = jnp.where(causal, 1.0, 0.0).astype(BF16)
    acs = sum(_dot(tril, part) for part in _split3(da))
    acs_t = acs.T
    side()

    e_rows = lax.broadcasted_iota(jnp.int32, (SPLIT_PIECES * LANES, B_WIDTH), 0)
    e_cols = lax.broadcasted_iota(jnp.int32, (SPLIT_PIECES * LANES, B_WIDTH), 1)
    expand = jnp.where(e_cols // B_HEAD_DIM == e_rows % LANES, 1.0, 0.0).astype(BF16)
    acs_e = _dot(_split_cols(acs, SPLIT_PIECES), expand)
    dt_e = _dot(_split_cols(dt, SPLIT_PIECES), expand)

    x = xs * dt_e
    even_head = (lax.broadcasted_iota(jnp.int32, (cl, B_WIDTH), 1) // B_HEAD_DIM) % 2 == 0
    x_even = jnp.where(even_head, x, 0.0).astype(BF16)
    x_odd = jnp.where(even_head, 0.0, x).astype(BF16)
    acs_last = acs_e[cl - 1:cl, :]
    decay_from_start = jnp.exp(acs_e)
    x_to_end = (x * jnp.exp(acs_last - acs_e)).astype(BF16)
    chunk_decay = jnp.exp(acs_last)
    side()

    heads_per_group = B_HEADS // B_GROUPS
    for g in range(B_GROUPS):
        gl = slice(g * B_GROUP_WIDTH, (g + 1) * B_GROUP_WIDTH)
        b_off = B_WIDTH + g * B_STATE
        c_off = B_WIDTH + B_GROUPS * B_STATE + g * B_STATE
        bc_t = xbc[:, b_off:b_off + B_STATE].T.astype(BF16)
        cc = xbc[:, c_off:c_off + B_STATE].astype(BF16)
        cb = _dot(cc, bc_t)
        state = state_ref[g]
        y_off = _dot(cc, state.astype(BF16)) * decay_from_start[:, gl]
        state_ref[g] = state * chunk_decay[:, gl] + _dot(bc_t, x_to_end[:, gl])
        for pair in range(heads_per_group // 2):
            head_a = g * heads_per_group + 2 * pair
            pl_off = head_a * B_HEAD_DIM
            y_pair = y_off[:, pair * LANES:(pair + 1) * LANES]
            for head, x_half in ((head_a, x_even), (head_a + 1, x_odd)):
                diff = acs[:, head:head + 1] - acs_t[head:head + 1, :]
                m = (cb * jnp.exp(jnp.where(causal, diff, -jnp.inf))).astype(BF16)
                y_pair = y_pair + _dot(m, x_half[:, pl_off:pl_off + LANES])
            y_ref[:, pl_off:pl_off + LANES] = y_pair
            side()

    side(len(side_work))
    y = (y_ref[...] + dskip_ref[...] * xs) * _silu(zxd_ref[:, Z_OFF:Z_OFF + B_WIDTH])
    for g in range(B_GROUPS):
        gl = slice(g * B_GROUP_WIDTH, (g + 1) * B_GROUP_WIDTH)
        yg = y[:, gl]
        yg = yg * lax.rsqrt(jnp.mean(yg * yg, axis=-1, keepdims=True) + EPS)
        o_ref[:, gl] = (yg * norm_ref[:, gl]).astype(BF16)


def _ssd_body(*refs, nc, n_chunks, casts):
    xn_ref, w_ref, convw_ref, convb_ref, dtb_ref, alog_ref, dskip_ref, norm_ref = refs[:8]
    cast_srcs = refs[8:8 + len(casts)]
    o_ref = refs[8 + len(casts)]
    cast_dsts = refs[9 + len(casts):-5]
    zxd0_ref, zxd1_ref, ext_ref, state_ref, y_ref = refs[-5:]
    s = pl.program_id(0)
    bufs = (zxd0_ref, zxd1_ref)

    def project_pieces(dst_ref):
        def piece(start):
            cols = slice(start, min(start + PROJ_PIECE, ZXD_WIDTH))

            def run():
                dst_ref[:, cols] = _dot(xn_ref[...], w_ref[:, cols])
            return run

        def casts_then(run):
            def both():
                _run_side_casts(s, cast_srcs, cast_dsts, casts)
                run()
            return both
        pieces = [piece(start) for start in range(0, ZXD_WIDTH, PROJ_PIECE)]
        return [casts_then(pieces[0])] + pieces[1:]

    def mix(src_ref, side_work=()):
        _ssd_mix(src_ref, convw_ref, convb_ref, dtb_ref, alog_ref, dskip_ref, norm_ref, o_ref,
                 ext_ref, state_ref, y_ref, side_work)

    @pl.when(lax.rem(s - 1, nc) == 0)
    def _():
        ext_ref[0:CONV_HALO, :] = jnp.zeros((CONV_HALO, B_CONV_DIM), F32)
        state_ref[...] = jnp.zeros_like(state_ref)

    @pl.when(s == 0)
    def _():
        for run in project_pieces(bufs[0]):
            run()

    for parity in range(2):
        @pl.when(jnp.logical_and(jnp.logical_and(s > 0, s < n_chunks), lax.rem(s, 2) == parity))
        def _():
            mix(bufs[1 - parity], project_pieces(bufs[parity]))

    @pl.when(s == n_chunks)
    def _():
        _run_side_casts(s, cast_srcs, cast_dsts, casts)
        mix(bufs[(n_chunks - 1) % 2])


def _inproj_ssd(xn, w_zxd, batch, conv_w, conv_b, dt_bias, a_log, d_skip, ssm_norm, side_casts=()):
    t, d = xn.shape
    seq = t // batch
    assert seq % B_CHUNK == 0 and w_zxd.shape[1] == ZXD_WIDTH
    nc = seq // B_CHUNK
    n_chunks = batch * nc
    const = lambda s: (0, 0)
    cast_in, cast_in_specs, cast_shapes, cast_out_specs, cast_plans = _side_cast_specs(
        side_casts, lambda s: s, n_chunks + 1)
    return pl.pallas_call(
        functools.partial(_ssd_body, nc=nc, n_chunks=n_chunks, casts=cast_plans),
        grid=(n_chunks + 1,),
        in_specs=[
            pl.BlockSpec((B_CHUNK, d), lambda s: (jnp.minimum(s, n_chunks - 1), 0)),
            pl.BlockSpec((d, ZXD_WIDTH), const),
            pl.BlockSpec((B_CONV, B_CONV_DIM), const),
            pl.BlockSpec((1, B_CONV_DIM), const),
            pl.BlockSpec((1, LANES), const),
            pl.BlockSpec((1, LANES), const),
            pl.BlockSpec((1, B_WIDTH), const),
            pl.BlockSpec((1, B_WIDTH), const),
        ] + cast_in_specs,
        out_specs=[pl.BlockSpec((B_CHUNK, B_WIDTH), lambda s: (jnp.maximum(s - 1, 0), 0))] + cast_out_specs,
        out_shape=[jax.ShapeDtypeStruct((t, B_WIDTH), BF16)] + cast_shapes,
        scratch_shapes=[
            pltpu.VMEM((B_CHUNK, ZXD_WIDTH), F32),
            pltpu.VMEM((B_CHUNK, ZXD_WIDTH), F32),
            pltpu.VMEM((CONV_HALO + B_CHUNK, B_CONV_DIM), F32),
            pltpu.VMEM((B_GROUPS, B_STATE, B_GROUP_WIDTH), F32),
            pltpu.VMEM((B_CHUNK, B_WIDTH), F32),
        ],
        compiler_params=_params(("arbitrary",)),
        name="inproj_ssd",
    )(xn, w_zxd, conv_w, conv_b, dt_bias, a_log, d_skip, ssm_norm, *cast_in)


def _outproj_body(x_ref, ya_ref, yb_ref, w_ref, g_ref, o_ref):
    h = _dot(jnp.concatenate([ya_ref[...], yb_ref[...]], axis=1), w_ref[...])
    o_ref[...] = x_ref[...] + _rmsnorm(h, g_ref[...])


def _outproj(x, y_a, y_b, w_out, g_post):
    t, d = x.shape
    ka, kb = y_a.shape[1], y_b.shape[1]
    tm = PROJ_ROW_TILE
    assert t % tm == 0 and ka == kb and w_out.shape[0] == ka + kb
    const = lambda i: (0, 0)
    return pl.pallas_call(
        _outproj_body,
        grid=(t // tm,),
        in_specs=[
            pl.BlockSpec((tm, d), lambda i: (i, 0)),
            pl.BlockSpec((tm, ka), lambda i: (i, 0)),
            pl.BlockSpec((tm, kb), lambda i: (i, 0)),
            pl.BlockSpec((ka + kb, d), const, pipeline_mode=pl.Buffered(1)),
            pl.BlockSpec((1, d), const),
        ],
        out_specs=pl.BlockSpec((tm, d), lambda i: (i, 0)),
        out_shape=jax.ShapeDtypeStruct((t, d), F32),
        compiler_params=_params(("parallel",)),
        name="outproj",
    )(x, y_a, y_b, w_out, g_post)


def _row(v):
    return v.reshape(1, -1).astype(F32)


def kernel(x, ffn1_norm_pre, ffn1_norm_post, ffn1_w_gate, ffn1_w_up, ffn1_w_down, mix_norm_pre, mix_norm_post, w_in, sgu_norm, w_spatial, b_spatial, conv_w, conv_b, dt_bias, a_log, d_skip, ssm_norm, w_out, ffn2_norm_pre, ffn2_norm_post, ffn2_w_gate, ffn2_w_up, ffn2_w_down):
    batch, seq, d = x.shape
    depth = ffn1_norm_pre.shape[0]
    a_width = sgu_norm.shape[1]
    xf = x.reshape(batch * seq, d)
    for l in range(depth):
        w_in_t = jnp.swapaxes(w_in, 1, 2)
        n_in = w_in.shape[2]
        whole = lambda w: (w, l, 0, w.shape[1], w.shape[1], False)
        head, w1_gate, w1_up, w1_down, w_uv, w_zxd = _ffn_head(
            xf, _row(ffn1_norm_pre[l]), _row(ffn1_norm_post[l]), ffn1_w_gate, ffn1_w_up, ffn1_w_down, l,
            ((w_in_t, l, 0, 2 * a_width, 2 * a_width, False),
             (w_in_t, l, 2 * a_width, n_in - 2 * a_width, ZXD_WIDTH, True)))
        (xf,) = _ffn(xf, _row(ffn1_norm_pre[l]), _row(ffn1_norm_post[l]), w1_gate, w1_up, w1_down, head=head)

        g_mix = _row(mix_norm_pre[l])
        b_full = jnp.broadcast_to(b_spatial[l][:, :, None], b_spatial[l].shape + (A_HEAD_DIM,))
        y_a, xn = _inproj_sgu(xf, g_mix, w_uv, _row(sgu_norm[l]), w_spatial[l], b_full)

        pad_heads = lambda v: jnp.pad(v.astype(F32), (0, LANES - B_HEADS)).reshape(1, LANES)
        y_b, w2_gate, w2_up, w2_down, wo = _inproj_ssd(
            xn, w_zxd, batch, conv_w[l], _row(conv_b[l]), pad_heads(dt_bias[l]), pad_heads(a_log[l]),
            _row(jnp.repeat(d_skip[l], B_HEAD_DIM)), _row(ssm_norm[l]),
            (whole(ffn2_w_gate), whole(ffn2_w_up), whole(ffn2_w_down), whole(w_out)))

        xf = _outproj(xf, y_a, y_b, wo, _row(mix_norm_post[l]))

        (xf,) = _ffn(xf, _row(ffn2_norm_pre[l]), _row(ffn2_norm_post[l]), w2_gate, w2_up, w2_down)
    return xf.reshape(batch, seq, d)
```

```python
import functools

import jax
import jax.numpy as jnp
from jax import lax
from jax.experimental import pallas as pl
from jax.experimental.pallas import tpu as pltpu

F32 = jnp.float32
BF16 = jnp.bfloat16

EPS = 1e-6
HALF_STEP = 0.5
SQRT_HALF = 0.7071067811865476

A_HEAD_DIM = 128
A_CHUNK = 128
B_HEAD_DIM = 64
B_HEADS = 16
B_GROUPS = 2
B_STATE = 128
B_CONV = 4
B_CHUNK = 256
B_WIDTH = B_HEADS * B_HEAD_DIM
B_GROUP_WIDTH = B_WIDTH // B_GROUPS
B_CONV_DIM = B_WIDTH + 2 * B_GROUPS * B_STATE

LANES = 128
SUBLANES = 8
BF16_SUBLANES = 16
VMEM_LIMIT_BYTES = 60 * 1024 * 1024

FFN_ROW_TILE = 1024
FFN_FF_TILE = 512
FFN_FF_SUB = 256
FFN_FINISH_ROWS = 256
FFN_HEAD_FF_TILE = 256
PROJ_ROW_TILE = 1024
SPLIT_PIECES = 2
PROJ_PIECE = 256


def _rmsnorm(x, g):
    return x * lax.rsqrt(jnp.mean(x * x, axis=-1, keepdims=True) + EPS) * g


def _gelu(x):
    return 0.5 * x * (1.0 + lax.erf(x * SQRT_HALF))


def _silu(x):
    return x * jax.nn.sigmoid(x)


def _dot(a, b):
    return jnp.dot(a, b, preferred_element_type=F32)


def _dot_nt(a, b_t):
    return lax.dot_general(a, b_t, (((1,), (1,)), ((), ())), preferred_element_type=F32)


def _split3(v):
    hi = v.astype(BF16)
    r1 = v - hi.astype(F32)
    mid = r1.astype(BF16)
    lo = (r1 - mid.astype(F32)).astype(BF16)
    return hi, mid, lo


def _params(semantics):
    return pltpu.CompilerParams(dimension_semantics=semantics, vmem_limit_bytes=VMEM_LIMIT_BYTES)


def _slab_rows(rows, max_slabs, align):
    for slab in range(align, rows + 1, align):
        if rows % slab == 0 and rows // slab <= max_slabs:
            return slab
    raise ValueError(f"no slab size for {rows} rows in {max_slabs} steps")


def _side_cast_specs(side_casts, flat_step, total_steps):
    inputs, in_specs, out_shapes, out_specs, plans = [], [], [], [], []
    for src, layer, row_start, n_rows, out_rows, transpose in side_casts:
        n_cols = src.shape[2]
        slab_rows = _slab_rows(out_rows, total_steps, LANES if transpose else BF16_SUBLANES)
        assert row_start % slab_rows == 0
        n_real, n_out, first = pl.cdiv(n_rows, slab_rows), out_rows // slab_rows, row_start // slab_rows
        src_slab = lambda *idx, n=n_real, first=first, layer=layer: (
            layer, first + jnp.minimum(flat_step(*idx), n - 1), 0)
        dst_slab = lambda *idx, n=n_out: jnp.minimum(flat_step(*idx), n - 1)
        inputs.append(src)
        in_specs.append(pl.BlockSpec((None, slab_rows, n_cols), src_slab))
        if transpose:
            out_shapes.append(jax.ShapeDtypeStruct((n_cols, out_rows), BF16))
            out_specs.append(pl.BlockSpec((n_cols, slab_rows), lambda *idx, f=dst_slab: (0, f(*idx))))
        else:
            out_shapes.append(jax.ShapeDtypeStruct((out_rows, n_cols), BF16))
            out_specs.append(pl.BlockSpec((slab_rows, n_cols), lambda *idx, f=dst_slab: (f(*idx), 0)))
        plans.append((n_rows, n_out, transpose))
    return inputs, in_specs, out_shapes, out_specs, tuple(plans)


def _run_side_casts(step, src_refs, dst_refs, plans):
    for src_ref, dst_ref, (n_rows, n_slabs, transpose) in zip(src_refs, dst_refs, plans):
        slab = src_ref[...]
        if n_rows != n_slabs * src_ref.shape[0]:
            row = jnp.minimum(step, n_slabs - 1) * src_ref.shape[0] + lax.broadcasted_iota(jnp.int32, slab.shape, 0)
            slab = jnp.where(row < n_rows, slab, 0.0)
        dst_ref[...] = (slab.T if transpose else slab).astype(BF16)


def _swiglu_partial(xn, wg, wu, wd, ff_sub, after_first=None, merge_narrow=False):
    width = wg.shape[1]
    acts = []
    for start in range(0, width, ff_sub):
        cols = slice(start, min(start + ff_sub, width))
        sub = cols.stop - start
        if merge_narrow and 2 * sub <= ff_sub:
            both = _dot(xn, jnp.concatenate([wg[:, cols], wu[:, cols]], axis=1))
            gate, up = both[:, :sub], both[:, sub:]
        else:
            gate, up = _dot(xn, wg[:, cols]), _dot(xn, wu[:, cols])
        acts.append((_silu(gate) * up).astype(BF16))
        if start == 0 and after_first is not None:
            after_first()
    return _dot(jnp.concatenate(acts, axis=1), wd)


def _ffn_body(*refs, n_steps, tail_width, ff_sub, head_piece, casts):
    refs = list(refs)
    x_ref, gpre_ref, gpost_ref, wg_ref, wu_ref, wd_ref = refs[:6]
    del refs[:6]
    head_ref = refs.pop(0) if head_piece else None
    cast_srcs = refs[:len(casts)]
    o_ref = refs[len(casts)]
    cast_dsts = refs[len(casts) + 1:-1]
    xn_ref = refs[-1]
    i = pl.program_id(0)
    j = pl.program_id(1)
    tf = wg_ref.shape[1]
    computed = (i > 0) if head_piece else True

    def partial_out(width, rows=slice(None)):
        def side():
            if rows.start in (None, 0):
                _run_side_casts(i * n_steps + j, cast_srcs, cast_dsts, casts)
        return _swiglu_partial(xn_ref[rows, :], wg_ref[:, :width], wu_ref[:, :width], wd_ref[:width, :], ff_sub, side,
                               merge_narrow=rows.start is not None)

    if head_piece:
        @pl.when(jnp.logical_and(i == 0, j < o_ref.shape[0] // head_piece))
        def _():
            o_ref[pl.ds(pl.multiple_of(j * head_piece, head_piece), head_piece), :] = head_ref[...]

    @pl.when(jnp.logical_and(computed, j == 0))
    def _():
        xn_ref[...] = _rmsnorm(x_ref[...], gpre_ref[...]).astype(BF16)
        o_ref[...] = partial_out(tf)

    @pl.when(jnp.logical_and(computed, jnp.logical_and(j > 0, j < n_steps - 1)))
    def _():
        o_ref[...] += partial_out(tf)

    @pl.when(jnp.logical_and(computed, j == n_steps - 1))
    def _():
        for start in range(0, x_ref.shape[0], FFN_FINISH_ROWS):
            rows = slice(start, start + FFN_FINISH_ROWS)
            h = o_ref[rows, :] + partial_out(tail_width, rows)
            o_ref[rows, :] = x_ref[rows, :] + HALF_STEP * _rmsnorm(h, gpost_ref[...])


def _ffn(x, g_pre, g_post, wg, wu, wd, head=None, side_casts=()):
    t, d = x.shape
    f = wg.shape[1]
    tm, tf, ff_sub = FFN_ROW_TILE, FFN_FF_TILE, FFN_FF_SUB
    n_steps = pl.cdiv(f, tf)
    tail_width = f - (n_steps - 1) * tf
    assert t % tm == 0 and n_steps >= 2 and tail_width % LANES == 0
    row = pl.BlockSpec((tm, d), lambda i, j: (i, 0))
    vec = pl.BlockSpec((1, d), lambda i, j: (0, 0))
    ff_tile = (lambda i, j: jnp.where(i == 0, 0, j)) if head is not None else (lambda i, j: j)
    head_in, head_specs, head_piece = [], [], None
    if head is not None:
        assert head.shape == (tm, d)
        head_piece = _slab_rows(tm, n_steps, SUBLANES)
        n_head = tm // head_piece
        head_in = [head]
        head_specs = [pl.BlockSpec((head_piece, d), lambda i, j: (jnp.where(i == 0, jnp.minimum(j, n_head - 1), n_head - 1), 0))]
    assert head is None or not side_casts
    cast_in, cast_in_specs, cast_shapes, cast_out_specs, cast_plans = _side_cast_specs(
        side_casts, lambda i, j: i * n_steps + j, (t // tm) * n_steps)
    return pl.pallas_call(
        functools.partial(_ffn_body, n_steps=n_steps, tail_width=tail_width, ff_sub=ff_sub,
                          head_piece=head_piece, casts=cast_plans),
        grid=(t // tm, n_steps),
        in_specs=[row, vec, vec,
                  pl.BlockSpec((d, tf), lambda i, j: (0, ff_tile(i, j))),
                  pl.BlockSpec((d, tf), lambda i, j: (0, ff_tile(i, j))),
                  pl.BlockSpec((tf, d), lambda i, j: (ff_tile(i, j), 0))] + head_specs + cast_in_specs,
        out_specs=[row] + cast_out_specs,
        out_shape=[jax.ShapeDtypeStruct((t, d), F32)] + cast_shapes,
        scratch_shapes=[pltpu.VMEM((tm, d), BF16)],
        compiler_params=_params(("arbitrary", "arbitrary") if side_casts else ("parallel", "arbitrary")),
        name="ffn",
    )(x, g_pre, g_post, wg, wu, wd, *head_in, *cast_in)


def _ffn_head_body(*refs, n_steps, tail_width, ff_sub, casts):
    x_ref, gpre_ref, gpost_ref, wg32_ref, wu32_ref, wd32_ref = refs[:6]
    cast_srcs = refs[6:6 + len(casts)]
    o_ref, wg16_ref, wu16_ref, wd16_ref = refs[6 + len(casts):10 + len(casts)]
    cast_dsts = refs[10 + len(casts):-1]
    xn_ref = refs[-1]
    j = pl.program_id(0)
    tf = wg32_ref.shape[1]

    def partial_out(width):
        wg = wg32_ref[:, :width].astype(BF16)
        wu = wu32_ref[:, :width].astype(BF16)
        wd = wd32_ref[:width, :].astype(BF16)
        wg16_ref[:, :width] = wg
        wu16_ref[:, :width] = wu
        wd16_ref[:width, :] = wd
        return _swiglu_partial(xn_ref[...], wg, wu, wd, ff_sub,
                               lambda: _run_side_casts(j, cast_srcs, cast_dsts, casts))

    @pl.when(j == 0)
    def _():
        xn_ref[...] = _rmsnorm(x_ref[...], gpre_ref[...]).astype(BF16)
        o_ref[...] = partial_out(tf)

    @pl.when(jnp.logical_and(j > 0, j < n_steps - 1))
    def _():
        o_ref[...] += partial_out(tf)

    @pl.when(j == n_steps - 1)
    def _():
        h = o_ref[...] + partial_out(tail_width)
        o_ref[...] = x_ref[...] + HALF_STEP * _rmsnorm(h, gpost_ref[...])


def _ffn_head(x, g_pre, g_post, w_gate, w_up, w_down, layer, side_casts=()):
    d = x.shape[1]
    f = w_gate.shape[2]
    tm, tf, ff_sub = FFN_ROW_TILE, FFN_HEAD_FF_TILE, FFN_FF_SUB
    n_steps = pl.cdiv(f, tf)
    tail_width = f - (n_steps - 1) * tf
    assert n_steps >= 2 and tail_width % LANES == 0
    const = lambda j: (0, 0)
    vec = pl.BlockSpec((1, d), const)
    cast_in, cast_in_specs, cast_shapes, cast_out_specs, cast_plans = _side_cast_specs(
        side_casts, lambda j: j, n_steps)
    return pl.pallas_call(
        functools.partial(_ffn_head_body, n_steps=n_steps, tail_width=tail_width, ff_sub=ff_sub, casts=cast_plans),
        grid=(n_steps,),
        in_specs=[pl.BlockSpec((tm, d), const, pipeline_mode=pl.Buffered(1)), vec, vec,
                  pl.BlockSpec((None, d, tf), lambda j: (layer, 0, j)),
                  pl.BlockSpec((None, d, tf), lambda j: (layer, 0, j)),
                  pl.BlockSpec((None, tf, d), lambda j: (layer, j, 0))] + cast_in_specs,
        out_specs=[pl.BlockSpec((tm, d), const),
                   pl.BlockSpec((d, tf), lambda j: (0, j)),
                   pl.BlockSpec((d, tf), lambda j: (0, j)),
                   pl.BlockSpec((tf, d), lambda j: (j, 0))] + cast_out_specs,
        out_shape=[jax.ShapeDtypeStruct((tm, d), F32),
                   jax.ShapeDtypeStruct((d, f), BF16), jax.ShapeDtypeStruct((d, f), BF16),
                   jax.ShapeDtypeStruct((f, d), BF16)] + cast_shapes,
        scratch_shapes=[pltpu.VMEM((tm, d), BF16)],
        compiler_params=_params(("arbitrary",)),
        name="ffn_head",
    )(x, g_pre, g_post, w_gate, w_up, w_down, *cast_in)


def _sgu_body(x_ref, gpre_ref, wu_ref, wv_ref, gs_ref, ws_ref, bs_ref, o_ref, xn_ref, u_ref, v_ref):
    tm = x_ref.shape[0]
    heads = ws_ref.shape[0]
    xn = _rmsnorm(x_ref[...], gpre_ref[...]).astype(BF16)
    xn_ref[...] = xn
    u_ref[...] = _gelu(_dot_nt(xn, wu_ref[...]))
    v_ref[...] = _rmsnorm(_gelu(_dot_nt(xn, wv_ref[...])), gs_ref[...]).astype(BF16)
    rows = lax.broadcasted_iota(jnp.int32, (A_CHUNK, A_CHUNK), 0)
    cols = lax.broadcasted_iota(jnp.int32, (A_CHUNK, A_CHUNK), 1)
    causal = rows >= cols
    for h in range(heads):
        w = jnp.where(causal, ws_ref[h], 0.0).astype(BF16)
        bias = bs_ref[h]
        lanes = pl.ds(h * A_HEAD_DIM, A_HEAD_DIM)
        for c in range(tm // A_CHUNK):
            rws = pl.ds(c * A_CHUNK, A_CHUNK)
            mixed = _dot(w, v_ref[rws, lanes]) + bias
            o_ref[rws, lanes] = (u_ref[rws, lanes] * mixed).astype(BF16)


def _inproj_sgu(x, g_pre, w_uv, g_sgu, w_spatial, b_full):
    t, d = x.shape
    aw = w_uv.shape[0] // 2
    heads = w_spatial.shape[0]
    tm = PROJ_ROW_TILE
    assert t % tm == 0 and tm % A_CHUNK == 0 and aw == heads * A_HEAD_DIM
    const2 = lambda i: (0, 0)
    const3 = lambda i: (0, 0, 0)
    return pl.pallas_call(
        _sgu_body,
        grid=(t // tm,),
        in_specs=[
            pl.BlockSpec((tm, d), lambda i: (i, 0)),
            pl.BlockSpec((1, d), const2),
            pl.BlockSpec((aw, d), const2, pipeline_mode=pl.Buffered(1)),
            pl.BlockSpec((aw, d), lambda i: (1, 0), pipeline_mode=pl.Buffered(1)),
            pl.BlockSpec((1, aw), const2),
            pl.BlockSpec((heads, A_CHUNK, A_CHUNK), const3),
            pl.BlockSpec((heads, A_CHUNK, A_HEAD_DIM), const3),
        ],
        out_specs=[pl.BlockSpec((tm, aw), lambda i: (i, 0)), pl.BlockSpec((tm, d), lambda i: (i, 0))],
        out_shape=[jax.ShapeDtypeStruct((t, aw), BF16), jax.ShapeDtypeStruct((t, d), BF16)],
        scratch_shapes=[pltpu.VMEM((tm, aw), F32), pltpu.VMEM((tm, aw), BF16)],
        compiler_params=_params(("parallel",)),
        name="inproj_sgu",
    )(x, g_pre, w_uv, w_uv, g_sgu, w_spatial, b_full)


Z_OFF = 0
XBC_OFF = B_WIDTH
DT_OFF = B_WIDTH + B_CONV_DIM
ZXD_WIDTH = DT_OFF + LANES
CONV_HALO = SUBLANES


def _split_cols(v, pieces):
    parts = []
    rest = v
    for _ in range(pieces):
        part = rest.astype(BF16)
        parts.append(part)
        rest = rest - part.astype(F32)
    return jnp.concatenate(parts, axis=1)


def _ssd_mix(zxd_ref, convw_ref, convb_ref, dtb_ref, alog_ref, dskip_ref, norm_ref, o_ref,
             ext_ref, state_ref, y_ref, side_work=()):
    cl = B_CHUNK
    side_work = list(side_work)

    def side(n=1):
        for _ in range(n):
            if side_work:
                side_work.pop(0)()

    ext_ref[CONV_HALO:CONV_HALO + cl, :] = zxd_ref[:, XBC_OFF:XBC_OFF + B_CONV_DIM]
    ext = ext_ref[...]
    conv = convw_ref[0:1, :] * ext
    for k in range(1, B_CONV):
        conv = pltpu.roll(conv, 1, axis=0) + convw_ref[k:k + 1, :] * ext
    conv = conv[CONV_HALO:, :] + convb_ref[...]
    ext_ref[0:CONV_HALO, :] = ext_ref[cl:cl + CONV_HALO, :]
    side()
    xbc = _silu(conv)
    xs = xbc[:, :B_WIDTH]

    dt = jax.nn.softplus(zxd_ref[:, DT_OFF:DT_OFF + LANES] + dtb_ref[...])
    da = dt * (-jnp.exp(alog_ref[...]))
    rows = lax.broadcasted_iota(jnp.int32, (cl, cl), 0)
    cols = lax.broadcasted_iota(jnp.int32, (cl, cl), 1)
    causal = rows >= cols
    tril = jnp.where(causal, 1.0, 0.0).astype(BF16)
    acs = sum(_dot(tril, part) for part in _split3(da))
    acs_t = acs.T
    side()

    e_rows = lax.broadcasted_iota(jnp.int32, (SPLIT_PIECES * LANES, B_WIDTH), 0)
    e_cols = lax.broadcasted_iota(jnp.int32, (SPLIT_PIECES * LANES, B_WIDTH), 1)
    expand = jnp.where(e_cols // B_HEAD_DIM == e_rows % LANES, 1.0, 0.0).astype(BF16)
    acs_e = _dot(_split_cols(acs, SPLIT_PIECES), expand)
    dt_e = _dot(_split_cols(dt, SPLIT_PIECES), expand)

    x = xs * dt_e
    even_head = (lax.broadcasted_iota(jnp.int32, (cl, B_WIDTH), 1) // B_HEAD_DIM) % 2 == 0
    x_even = jnp.where(even_head, x, 0.0).astype(BF16)
    x_odd = jnp.where(even_head, 0.0, x).astype(BF16)
    acs_last = acs_e[cl - 1:cl, :]
    decay_from_start = jnp.exp(acs_e)
    x_to_end = (x * jnp.exp(acs_last - acs_e)).astype(BF16)
    chunk_decay = jnp.exp(acs_last)
    side()

    heads_per_group = B_HEADS // B_GROUPS
    for g in range(B_GROUPS):
        gl = slice(g * B_GROUP_WIDTH, (g + 1) * B_GROUP_WIDTH)
        b_off = B_WIDTH + g * B_STATE
        c_off = B_WIDTH + B_GROUPS * B_STATE + g * B_STATE
        bc_t = xbc[:, b_off:b_off + B_STATE].T.astype(BF16)
        cc = xbc[:, c_off:c_off + B_STATE].astype(BF16)
        cb = _dot(cc, bc_t)
        state = state_ref[g]
        y_off = _dot(cc, state.astype(BF16)) * decay_from_start[:, gl]
        state_ref[g] = state * chunk_decay[:, gl] + _dot(bc_t, x_to_end[:, gl])
        for pair in range(heads_per_group // 2):
            head_a = g * heads_per_group + 2 * pair
            pl_off = head_a * B_HEAD_DIM
            y_pair = y_off[:, pair * LANES:(pair + 1) * LANES]
            for head, x_half in ((head_a, x_even), (head_a + 1, x_odd)):
                diff = acs[:, head:head + 1] - acs_t[head:head + 1, :]
                m = (cb * jnp.exp(jnp.where(causal, diff, -jnp.inf))).astype(BF16)
                y_pair = y_pair + _dot(m, x_half[:, pl_off:pl_off + LANES])
            y_ref[:, pl_off:pl_off + LANES] = y_pair
            side()

    side(len(side_work))
    y = (y_ref[...] + dskip_ref[...] * xs) * _silu(zxd_ref[:, Z_OFF:Z_OFF + B_WIDTH])
    for g in range(B_GROUPS):
        gl = slice(g * B_GROUP_WIDTH, (g + 1) * B_GROUP_WIDTH)
        yg = y[:, gl]
        yg = yg * lax.rsqrt(jnp.mean(yg * yg, axis=-1, keepdims=True) + EPS)
        o_ref[:, gl] = (yg * norm_ref[:, gl]).astype(BF16)


def _ssd_body(*refs, nc, n_chunks, casts):
    xn_ref, w_ref, convw_ref, convb_ref, dtb_ref, alog_ref, dskip_ref, norm_ref = refs[:8]
    cast_srcs = refs[8:8 + len(casts)]
    o_ref = refs[8 + len(casts)]
    cast_dsts = refs[9 + len(casts):-5]
    zxd0_ref, zxd1_ref, ext_ref, state_ref, y_ref = refs[-5:]
    s = pl.program_id(0)
    bufs = (zxd0_ref, zxd1_ref)

    def project_pieces(dst_ref):
        def piece(start):
            cols = slice(start, min(start + PROJ_PIECE, ZXD_WIDTH))

            def run():
                dst_ref[:, cols] = _dot(xn_ref[...], w_ref[:, cols])
            return run

        def casts_then(run):
            def both():
                _run_side_casts(s, cast_srcs, cast_dsts, casts)
                run()
            return both
        pieces = [piece(start) for start in range(0, ZXD_WIDTH, PROJ_PIECE)]
        return pieces[:-1] + [casts_then(pieces[-1])]

    def mix(src_ref, side_work=()):
        _ssd_mix(src_ref, convw_ref, convb_ref, dtb_ref, alog_ref, dskip_ref, norm_ref, o_ref,
                 ext_ref, state_ref, y_ref, side_work)

    @pl.when(lax.rem(s - 1, nc) == 0)
    def _():
        ext_ref[0:CONV_HALO, :] = jnp.zeros((CONV_HALO, B_CONV_DIM), F32)
        state_ref[...] = jnp.zeros_like(state_ref)

    @pl.when(s == 0)
    def _():
        for run in project_pieces(bufs[0]):
            run()

    for parity in range(2):
        @pl.when(jnp.logical_and(jnp.logical_and(s > 0, s < n_chunks), lax.rem(s, 2) == parity))
        def _():
            mix(bufs[1 - parity], project_pieces(bufs[parity]))

    @pl.when(s == n_chunks)
    def _():
        _run_side_casts(s, cast_srcs, cast_dsts, casts)
        mix(bufs[(n_chunks - 1) % 2])


def _inproj_ssd(xn, w_zxd, batch, conv_w, conv_b, dt_bias, a_log, d_skip, ssm_norm, side_casts=()):
    t, d = xn.shape
    seq = t // batch
    assert seq % B_CHUNK == 0 and w_zxd.shape[1] == ZXD_WIDTH
    nc = seq // B_CHUNK
    n_chunks = batch * nc
    const = lambda s: (0, 0)
    cast_in, cast_in_specs, cast_shapes, cast_out_specs, cast_plans = _side_cast_specs(
        side_casts, lambda s: s, n_chunks + 1)
    return pl.pallas_call(
        functools.partial(_ssd_body, nc=nc, n_chunks=n_chunks, casts=cast_plans),
        grid=(n_chunks + 1,),
        in_specs=[
            pl.BlockSpec((B_CHUNK, d), lambda s: (jnp.minimum(s, n_chunks - 1), 0)),
            pl.BlockSpec((d, ZXD_WIDTH), const),
            pl.BlockSpec((B_CONV, B_CONV_DIM), const),
            pl.BlockSpec((1, B_CONV_DIM), const),
            pl.BlockSpec((1, LANES), const),
            pl.BlockSpec((1, LANES), const),
            pl.BlockSpec((1, B_WIDTH), const),
            pl.BlockSpec((1, B_WIDTH), const),
        ] + cast_in_specs,
        out_specs=[pl.BlockSpec((B_CHUNK, B_WIDTH), lambda s: (jnp.maximum(s - 1, 0), 0))] + cast_out_specs,
        out_shape=[jax.ShapeDtypeStruct((t, B_WIDTH), BF16)] + cast_shapes,
        scratch_shapes=[
            pltpu.VMEM((B_CHUNK, ZXD_WIDTH), F32),
            pltpu.VMEM((B_CHUNK, ZXD_WIDTH), F32),
            pltpu.VMEM((CONV_HALO + B_CHUNK, B_CONV_DIM), F32),
            pltpu.VMEM((B_GROUPS, B_STATE, B_GROUP_WIDTH), F32),
            pltpu.VMEM((B_CHUNK, B_WIDTH), F32),
        ],
        compiler_params=_params(("arbitrary",)),
        name="inproj_ssd",
    )(xn, w_zxd, conv_w, conv_b, dt_bias, a_log, d_skip, ssm_norm, *cast_in)


def _outproj_body(x_ref, ya_ref, yb_ref, w_ref, g_ref, o_ref):
    h = _dot(jnp.concatenate([ya_ref[...], yb_ref[...]], axis=1), w_ref[...])
    o_ref[...] = x_ref[...] + _rmsnorm(h, g_ref[...])


def _outproj(x, y_a, y_b, w_out, g_post):
    t, d = x.shape
    ka, kb = y_a.shape[1], y_b.shape[1]
    tm = PROJ_ROW_TILE
    assert t % tm == 0 and ka == kb and w_out.shape[0] == ka + kb
    const = lambda i: (0, 0)
    return pl.pallas_call(
        _outproj_body,
        grid=(t // tm,),
        in_specs=[
            pl.BlockSpec((tm, d), lambda i: (i, 0)),
            pl.BlockSpec((tm, ka), lambda i: (i, 0)),
            pl.BlockSpec((tm, kb), lambda i: (i, 0)),
            pl.BlockSpec((ka + kb, d), const, pipeline_mode=pl.Buffered(1)),
            pl.BlockSpec((1, d), const),
        ],
        out_specs=pl.BlockSpec((tm, d), lambda i: (i, 0)),
        out_shape=jax.ShapeDtypeStruct((t, d), F32),
        compiler_params=_params(("parallel",)),
        name="outproj",
    )(x, y_a, y_b, w_out, g_post)


def _row(v):
    return v.reshape(1, -1).astype(F32)


def kernel(x, ffn1_norm_pre, ffn1_norm_post, ffn1_w_gate, ffn1_w_up, ffn1_w_down, mix_norm_pre, mix_norm_post, w_in, sgu_norm, w_spatial, b_spatial, conv_w, conv_b, dt_bias, a_log, d_skip, ssm_norm, w_out, ffn2_norm_pre, ffn2_norm_post, ffn2_w_gate, ffn2_w_up, ffn2_w_down):
    batch, seq, d = x.shape
    depth = ffn1_norm_pre.shape[0]
    a_width = sgu_norm.shape[1]
    xf = x.reshape(batch * seq, d)
    for l in range(depth):
        w_in_t = jnp.swapaxes(w_in, 1, 2)
        n_in = w_in.shape[2]
        whole = lambda w: (w, l, 0, w.shape[1], w.shape[1], False)
        head, w1_gate, w1_up, w1_down, w_uv, w_zxd = _ffn_head(
            xf, _row(ffn1_norm_pre[l]), _row(ffn1_norm_post[l]), ffn1_w_gate, ffn1_w_up, ffn1_w_down, l,
            ((w_in_t, l, 0, 2 * a_width, 2 * a_width, False),
             (w_in_t, l, 2 * a_width, n_in - 2 * a_width, ZXD_WIDTH, True)))
        (xf,) = _ffn(xf, _row(ffn1_norm_pre[l]), _row(ffn1_norm_post[l]), w1_gate, w1_up, w1_down, head=head)

        g_mix = _row(mix_norm_pre[l])
        b_full = jnp.broadcast_to(b_spatial[l][:, :, None], b_spatial[l].shape + (A_HEAD_DIM,))
        y_a, xn = _inproj_sgu(xf, g_mix, w_uv, _row(sgu_norm[l]), w_spatial[l], b_full)

        pad_heads = lambda v: jnp.pad(v.astype(F32), (0, LANES - B_HEADS)).reshape(1, LANES)
        y_b, w2_gate, w2_up, w2_down, wo = _inproj_ssd(
            xn, w_zxd, batch, conv_w[l], _row(conv_b[l]), pad_heads(dt_bias[l]), pad_heads(a_log[l]),
            _row(jnp.repeat(d_skip[l], B_HEAD_DIM)), _row(ssm_norm[l]),
            (whole(ffn2_w_gate), whole(ffn2_w_up), whole(ffn2_w_down), whole(w_out)))

        xf = _outproj(xf, y_a, y_b, wo, _row(mix_norm_post[l]))

        (xf,) = _ffn(xf, _row(ffn2_norm_pre[l]), _row(ffn2_norm_post[l]), w2_gate, w2_up, w2_down)
    return xf.reshape(batch, seq, d)
```

```python
import functools

import jax
import jax.numpy as jnp
from jax import lax
from jax.experimental import pallas as pl
from jax.experimental.pallas import tpu as pltpu

F32 = jnp.float32
BF16 = jnp.bfloat16

EPS = 1e-6
HALF_STEP = 0.5
SQRT_HALF = 0.7071067811865476

A_HEAD_DIM = 128
A_CHUNK = 128
B_HEAD_DIM = 64
B_HEADS = 16
B_GROUPS = 2
B_STATE = 128
B_CONV = 4
B_CHUNK = 256
B_WIDTH = B_HEADS * B_HEAD_DIM
B_GROUP_WIDTH = B_WIDTH // B_GROUPS
B_CONV_DIM = B_WIDTH + 2 * B_GROUPS * B_STATE

LANES = 128
SUBLANES = 8
BF16_SUBLANES = 16
VMEM_LIMIT_BYTES = 60 * 1024 * 1024

FFN_ROW_TILE = 1024
FFN_FF_TILE = 512
FFN_FF_SUB = 256
FFN_FINISH_ROWS = 256
FFN_HEAD_FF_TILE = 256
PROJ_ROW_TILE = 1024
SPLIT_PIECES = 2
PROJ_PIECE = 256


def _rmsnorm(x, g):
    return x * lax.rsqrt(jnp.mean(x * x, axis=-1, keepdims=True) + EPS) * g


def _gelu(x):
    return 0.5 * x * (1.0 + lax.erf(x * SQRT_HALF))


def _silu(x):
    return x * jax.nn.sigmoid(x)


def _dot(a, b):
    return jnp.dot(a, b, preferred_element_type=F32)


def _dot_nt(a, b_t):
    return lax.dot_general(a, b_t, (((1,), (1,)), ((), ())), preferred_element_type=F32)


def _split3(v):
    hi = v.astype(BF16)
    r1 = v - hi.astype(F32)
    mid = r1.astype(BF16)
    lo = (r1 - mid.astype(F32)).astype(BF16)
    return hi, mid, lo


def _params(semantics):
    return pltpu.CompilerParams(dimension_semantics=semantics, vmem_limit_bytes=VMEM_LIMIT_BYTES)


def _slab_rows(rows, max_slabs, align):
    for slab in range(align, rows + 1, align):
        if rows % slab == 0 and rows // slab <= max_slabs:
            return slab
    raise ValueError(f"no slab size for {rows} rows in {max_slabs} steps")


def _side_cast_specs(side_casts, flat_step, total_steps):
    inputs, in_specs, out_shapes, out_specs, plans = [], [], [], [], []
    for src, layer, row_start, n_rows, out_rows, transpose in side_casts:
        n_cols = src.shape[2]
        slab_rows = _slab_rows(out_rows, total_steps, LANES if transpose else BF16_SUBLANES)
        assert row_start % slab_rows == 0
        n_real, n_out, first = pl.cdiv(n_rows, slab_rows), out_rows // slab_rows, row_start // slab_rows
        src_slab = lambda *idx, n=n_real, first=first, layer=layer: (
            layer, first + jnp.minimum(flat_step(*idx), n - 1), 0)
        dst_slab = lambda *idx, n=n_out: jnp.minimum(flat_step(*idx), n - 1)
        inputs.append(src)
        in_specs.append(pl.BlockSpec((None, slab_rows, n_cols), src_slab))
        if transpose:
            out_shapes.append(jax.ShapeDtypeStruct((n_cols, out_rows), BF16))
            out_specs.append(pl.BlockSpec((n_cols, slab_rows), lambda *idx, f=dst_slab: (0, f(*idx))))
        else:
            out_shapes.append(jax.ShapeDtypeStruct((out_rows, n_cols), BF16))
            out_specs.append(pl.BlockSpec((slab_rows, n_cols), lambda *idx, f=dst_slab: (f(*idx), 0)))
        plans.append((n_rows, n_out, transpose))
    return inputs, in_specs, out_shapes, out_specs, tuple(plans)


def _run_side_casts(step, src_refs, dst_refs, plans):
    for src_ref, dst_ref, (n_rows, n_slabs, transpose) in zip(src_refs, dst_refs, plans):
        slab = src_ref[...]
        if n_rows != n_slabs * src_ref.shape[0]:
            row = jnp.minimum(step, n_slabs - 1) * src_ref.shape[0] + lax.broadcasted_iota(jnp.int32, slab.shape, 0)
            slab = jnp.where(row < n_rows, slab, 0.0)
        dst_ref[...] = (slab.T if transpose else slab).astype(BF16)


def _swiglu_partial(xn, wg, wu, wd, ff_sub, after_first=None, merge_narrow=False):
    width = wg.shape[1]
    acts = []
    for start in range(0, width, ff_sub):
        cols = slice(start, min(start + ff_sub, width))
        sub = cols.stop - start
        if merge_narrow and 2 * sub <= ff_sub:
            both = _dot(xn, jnp.concatenate([wg[:, cols], wu[:, cols]], axis=1))
            gate, up = both[:, :sub], both[:, sub:]
        else:
            gate, up = _dot(xn, wg[:, cols]), _dot(xn, wu[:, cols])
        acts.append((_silu(gate) * up).astype(BF16))
        if start == 0 and after_first is not None:
            after_first()
    return _dot(jnp.concatenate(acts, axis=1), wd)


def _ffn_body(*refs, n_steps, tail_width, ff_sub, head_piece, casts):
    refs = list(refs)
    x_ref, gpre_ref, gpost_ref, wg_ref, wu_ref, wd_ref = refs[:6]
    del refs[:6]
    head_ref = refs.pop(0) if head_piece else None
    cast_srcs = refs[:len(casts)]
    o_ref = refs[len(casts)]
    cast_dsts = refs[len(casts) + 1:-1]
    xn_ref = refs[-1]
    i = pl.program_id(0)
    j = pl.program_id(1)
    tf = wg_ref.shape[1]
    computed = (i > 0) if head_piece else True

    def partial_out(width, rows=slice(None)):
        def side():
            if rows.start in (None, 0):
                _run_side_casts(i * n_steps + j, cast_srcs, cast_dsts, casts)
        return _swiglu_partial(xn_ref[rows, :], wg_ref[:, :width], wu_ref[:, :width], wd_ref[:width, :], ff_sub, side,
                               merge_narrow=rows.start is not None)

    if head_piece:
        @pl.when(jnp.logical_and(i == 0, j < o_ref.shape[0] // head_piece))
        def _():
            o_ref[pl.ds(pl.multiple_of(j * head_piece, head_piece), head_piece), :] = head_ref[...]

    @pl.when(jnp.logical_and(computed, j == 0))
    def _():
        xn_ref[...] = _rmsnorm(x_ref[...], gpre_ref[...]).astype(BF16)
        o_ref[...] = partial_out(tf)

    @pl.when(jnp.logical_and(computed, jnp.logical_and(j > 0, j < n_steps - 1)))
    def _():
        o_ref[...] += partial_out(tf)

    @pl.when(jnp.logical_and(computed, j == n_steps - 1))
    def _():
        for start in range(0, x_ref.shape[0], FFN_FINISH_ROWS):
            rows = slice(start, start + FFN_FINISH_ROWS)
            h = o_ref[rows, :] + partial_out(tail_width, rows)
            o_ref[rows, :] = x_ref[rows, :] + HALF_STEP * _rmsnorm(h, gpost_ref[...])


def _ffn(x, g_pre, g_post, wg, wu, wd, head=None, side_casts=()):
    t, d = x.shape
    f = wg.shape[1]
    tm, tf, ff_sub = FFN_ROW_TILE, FFN_FF_TILE, FFN_FF_SUB
    n_steps = pl.cdiv(f, tf)
    tail_width = f - (n_steps - 1) * tf
    assert t % tm == 0 and n_steps >= 2 and tail_width % LANES == 0
    row = pl.BlockSpec((tm, d), lambda i, j: (i, 0))
    vec = pl.BlockSpec((1, d), lambda i, j: (0, 0))
    ff_tile = (lambda i, j: jnp.where(i == 0, 0, j)) if head is not None else (lambda i, j: j)
    head_in, head_specs, head_piece = [], [], None
    if head is not None:
        assert head.shape == (tm, d)
        head_piece = _slab_rows(tm, n_steps, SUBLANES)
        n_head = tm // head_piece
        head_in = [head]
        head_specs = [pl.BlockSpec((head_piece, d), lambda i, j: (jnp.where(i == 0, jnp.minimum(j, n_head - 1), n_head - 1), 0))]
    assert head is None or not side_casts
    cast_in, cast_in_specs, cast_shapes, cast_out_specs, cast_plans = _side_cast_specs(
        side_casts, lambda i, j: i * n_steps + j, (t // tm) * n_steps)
    return pl.pallas_call(
        functools.partial(_ffn_body, n_steps=n_steps, tail_width=tail_width, ff_sub=ff_sub,
                          head_piece=head_piece, casts=cast_plans),
        grid=(t // tm, n_steps),
        in_specs=[row, vec, vec,
                  pl.BlockSpec((d, tf), lambda i, j: (0, ff_tile(i, j))),
                  pl.BlockSpec((d, tf), lambda i, j: (0, ff_tile(i, j))),
                  pl.BlockSpec((tf, d), lambda i, j: (ff_tile(i, j), 0))] + head_specs + cast_in_specs,
        out_specs=[row] + cast_out_specs,
        out_shape=[jax.ShapeDtypeStruct((t, d), F32)] + cast_shapes,
        scratch_shapes=[pltpu.VMEM((tm, d), BF16)],
        compiler_params=_params(("arbitrary", "arbitrary") if side_casts else ("parallel", "arbitrary")),
        name="ffn",
    )(x, g_pre, g_post, wg, wu, wd, *head_in, *cast_in)


def _ffn_head_body(*refs, n_steps, tail_width, ff_sub, casts):
    x_ref, gpre_ref, gpost_ref, wg32_ref, wu32_ref, wd32_ref = refs[:6]
    cast_srcs = refs[6:6 + len(casts)]
    o_ref, wg16_ref, wu16_ref, wd16_ref = refs[6 + len(casts):10 + len(casts)]
    cast_dsts = refs[10 + len(casts):-1]
    xn_ref = refs[-1]
    j = pl.program_id(0)
    tf = wg32_ref.shape[1]

    def partial_out(width):
        wg = wg32_ref[:, :width].astype(BF16)
        wu = wu32_ref[:, :width].astype(BF16)
        wd = wd32_ref[:width, :].astype(BF16)
        wg16_ref[:, :width] = wg
        wu16_ref[:, :width] = wu
        wd16_ref[:width, :] = wd
        return _swiglu_partial(xn_ref[...], wg, wu, wd, ff_sub,
                               lambda: _run_side_casts(j, cast_srcs, cast_dsts, casts))

    @pl.when(j == 0)
    def _():
        xn_ref[...] = _rmsnorm(x_ref[...], gpre_ref[...]).astype(BF16)
        o_ref[...] = partial_out(tf)

    @pl.when(jnp.logical_and(j > 0, j < n_steps - 1))
    def _():
        o_ref[...] += partial_out(tf)

    @pl.when(j == n_steps - 1)
    def _():
        h = o_ref[...] + partial_out(tail_width)
        o_ref[...] = x_ref[...] + HALF_STEP * _rmsnorm(h, gpost_ref[...])


def _ffn_head(x, g_pre, g_post, w_gate, w_up, w_down, layer, side_casts=()):
    d = x.shape[1]
    f = w_gate.shape[2]
    tm, tf, ff_sub = FFN_ROW_TILE, FFN_HEAD_FF_TILE, FFN_FF_SUB
    n_steps = pl.cdiv(f, tf)
    tail_width = f - (n_steps - 1) * tf
    assert n_steps >= 2 and tail_width % LANES == 0
    const = lambda j: (0, 0)
    vec = pl.BlockSpec((1, d), const)
    cast_in, cast_in_specs, cast_shapes, cast_out_specs, cast_plans = _side_cast_specs(
        side_casts, lambda j: j, n_steps)
    return pl.pallas_call(
        functools.partial(_ffn_head_body, n_steps=n_steps, tail_width=tail_width, ff_sub=ff_sub, casts=cast_plans),
        grid=(n_steps,),
        in_specs=[pl.BlockSpec((tm, d), const, pipeline_mode=pl.Buffered(1)), vec, vec,
                  pl.BlockSpec((None, d, tf), lambda j: (layer, 0, j)),
                  pl.BlockSpec((None, d, tf), lambda j: (layer, 0, j)),
                  pl.BlockSpec((None, tf, d), lambda j: (layer, j, 0))] + cast_in_specs,
        out_specs=[pl.BlockSpec((tm, d), const),
                   pl.BlockSpec((d, tf), lambda j: (0, j)),
                   pl.BlockSpec((d, tf), lambda j: (0, j)),
                   pl.BlockSpec((tf, d), lambda j: (j, 0))] + cast_out_specs,
        out_shape=[jax.ShapeDtypeStruct((tm, d), F32),
                   jax.ShapeDtypeStruct((d, f), BF16), jax.ShapeDtypeStruct((d, f), BF16),
                   jax.ShapeDtypeStruct((f, d), BF16)] + cast_shapes,
        scratch_shapes=[pltpu.VMEM((tm, d), BF16)],
        compiler_params=_params(("arbitrary",)),
        name="ffn_head",
    )(x, g_pre, g_post, w_gate, w_up, w_down, *cast_in)


def _sgu_body(x_ref, gpre_ref, wu_ref, wv_ref, gs_ref, ws_ref, bs_ref, o_ref, xn_ref, u_ref, v_ref):
    tm = x_ref.shape[0]
    heads = ws_ref.shape[0]
    xn = _rmsnorm(x_ref[...], gpre_ref[...]).astype(BF16)
    xn_ref[...] = xn
    v_ref[...] = _rmsnorm(_gelu(_dot_nt(xn, wv_ref[...])), gs_ref[...]).astype(BF16)
    u_ref[...] = _gelu(_dot_nt(xn, wu_ref[...]))
    rows = lax.broadcasted_iota(jnp.int32, (A_CHUNK, A_CHUNK), 0)
    cols = lax.broadcasted_iota(jnp.int32, (A_CHUNK, A_CHUNK), 1)
    causal = rows >= cols
    for h in range(heads):
        w = jnp.where(causal, ws_ref[h], 0.0).astype(BF16)
        bias = bs_ref[h]
        lanes = pl.ds(h * A_HEAD_DIM, A_HEAD_DIM)
        for c in range(tm // A_CHUNK):
            rws = pl.ds(c * A_CHUNK, A_CHUNK)
            mixed = _dot(w, v_ref[rws, lanes]) + bias
            o_ref[rws, lanes] = (u_ref[rws, lanes] * mixed).astype(BF16)


def _inproj_sgu(x, g_pre, w_uv, g_sgu, w_spatial, b_full):
    t, d = x.shape
    aw = w_uv.shape[0] // 2
    heads = w_spatial.shape[0]
    tm = PROJ_ROW_TILE
    assert t % tm == 0 and tm % A_CHUNK == 0 and aw == heads * A_HEAD_DIM
    const2 = lambda i: (0, 0)
    const3 = lambda i: (0, 0, 0)
    return pl.pallas_call(
        _sgu_body,
        grid=(t // tm,),
        in_specs=[
            pl.BlockSpec((tm, d), lambda i: (i, 0)),
            pl.BlockSpec((1, d), const2),
            pl.BlockSpec((aw, d), const2, pipeline_mode=pl.Buffered(1)),
            pl.BlockSpec((aw, d), lambda i: (1, 0), pipeline_mode=pl.Buffered(1)),
            pl.BlockSpec((1, aw), const2),
            pl.BlockSpec((heads, A_CHUNK, A_CHUNK), const3),
            pl.BlockSpec((heads, A_CHUNK, A_HEAD_DIM), const3),
        ],
        out_specs=[pl.BlockSpec((tm, aw), lambda i: (i, 0)), pl.BlockSpec((tm, d), lambda i: (i, 0))],
        out_shape=[jax.ShapeDtypeStruct((t, aw), BF16), jax.ShapeDtypeStruct((t, d), BF16)],
        scratch_shapes=[pltpu.VMEM((tm, aw), F32), pltpu.VMEM((tm, aw), BF16)],
        compiler_params=_params(("parallel",)),
        name="inproj_sgu",
    )(x, g_pre, w_uv, w_uv, g_sgu, w_spatial, b_full)


Z_OFF = 0
XBC_OFF = B_WIDTH
DT_OFF = B_WIDTH + B_CONV_DIM
ZXD_WIDTH = DT_OFF + LANES
CONV_HALO = SUBLANES


def _split_cols(v, pieces):
    parts = []
    rest = v
    for _ in range(pieces):
        part = rest.astype(BF16)
        parts.append(part)
        rest = rest - part.astype(F32)
    return jnp.concatenate(parts, axis=1)


def _ssd_mix(zxd_ref, convw_ref, convb_ref, dtb_ref, alog_ref, dskip_ref, norm_ref, o_ref,
             ext_ref, state_ref, y_ref, side_work=()):
    cl = B_CHUNK
    side_work = list(side_work)

    def side(n=1):
        for _ in range(n):
            if side_work:
                side_work.pop(0)()

    ext_ref[CONV_HALO:CONV_HALO + cl, :] = zxd_ref[:, XBC_OFF:XBC_OFF + B_CONV_DIM]
    ext = ext_ref[...]
    conv = convw_ref[0:1, :] * ext
    for k in range(1, B_CONV):
        conv = pltpu.roll(conv, 1, axis=0) + convw_ref[k:k + 1, :] * ext
    conv = conv[CONV_HALO:, :] + convb_ref[...]
    ext_ref[0:CONV_HALO, :] = ext_ref[cl:cl + CONV_HALO, :]
    side()
    xbc = _silu(conv)
    xs = xbc[:, :B_WIDTH]

    dt = jax.nn.softplus(zxd_ref[:, DT_OFF:DT_OFF + LANES] + dtb_ref[...])
    da = dt * (-jnp.exp(alog_ref[...]))
    rows = lax.broadcasted_iota(jnp.int32, (cl, cl), 0)
    cols = lax.broadcasted_iota(jnp.int32, (cl, cl), 1)
    causal = rows >= cols
    tril = jnp.where(causal, 1.0, 0.0).astype(BF16)
    acs = sum(_dot(tril, part) for part in _split3(da))
    acs_t = acs.T
    side()

    e_rows = lax.broadcasted_iota(jnp.int32, (SPLIT_PIECES * LANES, B_WIDTH), 0)
    e_cols = lax.broadcasted_iota(jnp.int32, (SPLIT_PIECES * LANES, B_WIDTH), 1)
    expand = jnp.where(e_cols // B_HEAD_DIM == e_rows % LANES, 1.0, 0.0).astype(BF16)
    acs_e = _dot(_split_cols(acs, SPLIT_PIECES), expand)
    dt_e = _dot(_split_cols(dt, SPLIT_PIECES), expand)

    x = xs * dt_e
    even_head = (lax.broadcasted_iota(jnp.int32, (cl, B_WIDTH), 1) // B_HEAD_DIM) % 2 == 0
    x_even = jnp.where(even_head, x, 0.0).astype(BF16)
    x_odd = jnp.where(even_head, 0.0, x).astype(BF16)
    acs_last = acs_e[cl - 1:cl, :]
    decay_from_start = jnp.exp(acs_e)
    x_to_end = (x * jnp.exp(acs_last - acs_e)).astype(BF16)
    chunk_decay = jnp.exp(acs_last)
    side()

    heads_per_group = B_HEADS // B_GROUPS
    for g in range(B_GROUPS):
        gl = slice(g * B_GROUP_WIDTH, (g + 1) * B_GROUP_WIDTH)
        b_off = B_WIDTH + g * B_STATE
        c_off = B_WIDTH + B_GROUPS * B_STATE + g * B_STATE
        bc_t = xbc[:, b_off:b_off + B_STATE].T.astype(BF16)
        cc = xbc[:, c_off:c_off + B_STATE].astype(BF16)
        cb = _dot(cc, bc_t)
        state = state_ref[g]
        y_off = _dot(cc, state.astype(BF16)) * decay_from_start[:, gl]
        state_ref[g] = state * chunk_decay[:, gl] + _dot(bc_t, x_to_end[:, gl])
        for pair in range(heads_per_group // 2):
            head_a = g * heads_per_group + 2 * pair
            pl_off = head_a * B_HEAD_DIM
            y_pair = y_off[:, pair * LANES:(pair + 1) * LANES]
            for head, x_half in ((head_a, x_even), (head_a + 1, x_odd)):
                diff = acs[:, head:head + 1] - acs_t[head:head + 1, :]
                m = (cb * jnp.exp(jnp.where(causal, diff, -jnp.inf))).astype(BF16)
                y_pair = y_pair + _dot(m, x_half[:, pl_off:pl_off + LANES])
            y_ref[:, pl_off:pl_off + LANES] = y_pair
            side()

    side(len(side_work))
    y = (y_ref[...] + dskip_ref[...] * xs) * _silu(zxd_ref[:, Z_OFF:Z_OFF + B_WIDTH])
    for g in range(B_GROUPS):
        gl = slice(g * B_GROUP_WIDTH, (g + 1) * B_GROUP_WIDTH)
        yg = y[:, gl]
        yg = yg * lax.rsqrt(jnp.mean(yg * yg, axis=-1, keepdims=True) + EPS)
        o_ref[:, gl] = (yg * norm_ref[:, gl]).astype(BF16)


def _ssd_body(*refs, nc, n_chunks, casts):
    xn_ref, w_ref, convw_ref, convb_ref, dtb_ref, alog_ref, dskip_ref, norm_ref = refs[:8]
    cast_srcs = refs[8:8 + len(casts)]
    o_ref = refs[8 + len(casts)]
    cast_dsts = refs[9 + len(casts):-5]
    zxd0_ref, zxd1_ref, ext_ref, state_ref, y_ref = refs[-5:]
    s = pl.program_id(0)
    bufs = (zxd0_ref, zxd1_ref)

    def project_pieces(dst_ref):
        def piece(start):
            cols = slice(start, min(start + PROJ_PIECE, ZXD_WIDTH))

            def run():
                dst_ref[:, cols] = _dot(xn_ref[...], w_ref[:, cols])
            return run

        def casts_then(run):
            def both():
                _run_side_casts(s, cast_srcs, cast_dsts, casts)
                run()
            return both
        pieces = [piece(start) for start in range(0, ZXD_WIDTH, PROJ_PIECE)]
        return pieces[:-1] + [casts_then(pieces[-1])]

    def mix(src_ref, side_work=()):
        _ssd_mix(src_ref, convw_ref, convb_ref, dtb_ref, alog_ref, dskip_ref, norm_ref, o_ref,
                 ext_ref, state_ref, y_ref, side_work)

    @pl.when(lax.rem(s - 1, nc) == 0)
    def _():
        ext_ref[0:CONV_HALO, :] = jnp.zeros((CONV_HALO, B_CONV_DIM), F32)
        state_ref[...] = jnp.zeros_like(state_ref)

    @pl.when(s == 0)
    def _():
        for run in project_pieces(bufs[0]):
            run()

    for parity in range(2):
        @pl.when(jnp.logical_and(jnp.logical_and(s > 0, s < n_chunks), lax.rem(s, 2) == parity))
        def _():
            mix(bufs[1 - parity], project_pieces(bufs[parity]))

    @pl.when(s == n_chunks)
    def _():
        _run_side_casts(s, cast_srcs, cast_dsts, casts)
        mix(bufs[(n_chunks - 1) % 2])


def _inproj_ssd(xn, w_zxd, batch, conv_w, conv_b, dt_bias, a_log, d_skip, ssm_norm, side_casts=()):
    t, d = xn.shape
    seq = t // batch
    assert seq % B_CHUNK == 0 and w_zxd.shape[1] == ZXD_WIDTH
    nc = seq // B_CHUNK
    n_chunks = batch * nc
    const = lambda s: (0, 0)
    cast_in, cast_in_specs, cast_shapes, cast_out_specs, cast_plans = _side_cast_specs(
        side_casts, lambda s: s, n_chunks + 1)
    return pl.pallas_call(
        functools.partial(_ssd_body, nc=nc, n_chunks=n_chunks, casts=cast_plans),
        grid=(n_chunks + 1,),
        in_specs=[
            pl.BlockSpec((B_CHUNK, d), lambda s: (jnp.minimum(s, n_chunks - 1), 0)),
            pl.BlockSpec((d, ZXD_WIDTH), const),
            pl.BlockSpec((B_CONV, B_CONV_DIM), const),
            pl.BlockSpec((1, B_CONV_DIM), const),
            pl.BlockSpec((1, LANES), const),
            pl.BlockSpec((1, LANES), const),
            pl.BlockSpec((1, B_WIDTH), const),
            pl.BlockSpec((1, B_WIDTH), const),
        ] + cast_in_specs,
        out_specs=[pl.BlockSpec((B_CHUNK, B_WIDTH), lambda s: (jnp.maximum(s - 1, 0), 0))] + cast_out_specs,
        out_shape=[jax.ShapeDtypeStruct((t, B_WIDTH), BF16)] + cast_shapes,
        scratch_shapes=[
            pltpu.VMEM((B_CHUNK, ZXD_WIDTH), F32),
            pltpu.VMEM((B_CHUNK, ZXD_WIDTH), F32),
            pltpu.VMEM((CONV_HALO + B_CHUNK, B_CONV_DIM), F32),
            pltpu.VMEM((B_GROUPS, B_STATE, B_GROUP_WIDTH), F32),
            pltpu.VMEM((B_CHUNK, B_WIDTH), F32),
        ],
        compiler_params=_params(("arbitrary",)),
        name="inproj_ssd",
    )(xn, w_zxd, conv_w, conv_b, dt_bias, a_log, d_skip, ssm_norm, *cast_in)


def _outproj_body(x_ref, ya_ref, yb_ref, w_ref, g_ref, o_ref):
    h = _dot(jnp.concatenate([ya_ref[...], yb_ref[...]], axis=1), w_ref[...])
    o_ref[...] = x_ref[...] + _rmsnorm(h, g_ref[...])


def _outproj(x, y_a, y_b, w_out, g_post):
    t, d = x.shape
    ka, kb = y_a.shape[1], y_b.shape[1]
    tm = PROJ_ROW_TILE
    assert t % tm == 0 and ka == kb and w_out.shape[0] == ka + kb
    const = lambda i: (0, 0)
    return pl.pallas_call(
        _outproj_body,
        grid=(t // tm,),
        in_specs=[
            pl.BlockSpec((tm, d), lambda i: (i, 0)),
            pl.BlockSpec((tm, ka), lambda i: (i, 0)),
            pl.BlockSpec((tm, kb), lambda i: (i, 0)),
            pl.BlockSpec((ka + kb, d), const, pipeline_mode=pl.Buffered(1)),
            pl.BlockSpec((1, d), const),
        ],
        out_specs=pl.BlockSpec((tm, d), lambda i: (i, 0)),
        out_shape=jax.ShapeDtypeStruct((t, d), F32),
        compiler_params=_params(("parallel",)),
        name="outproj",
    )(x, y_a, y_b, w_out, g_post)


def _row(v):
    return v.reshape(1, -1).astype(F32)


def kernel(x, ffn1_norm_pre, ffn1_norm_post, ffn1_w_gate, ffn1_w_up, ffn1_w_down, mix_norm_pre, mix_norm_post, w_in, sgu_norm, w_spatial, b_spatial, conv_w, conv_b, dt_bias, a_log, d_skip, ssm_norm, w_out, ffn2_norm_pre, ffn2_norm_post, ffn2_w_gate, ffn2_w_up, ffn2_w_down):
    batch, seq, d = x.shape
    depth = ffn1_norm_pre.shape[0]
    a_width = sgu_norm.shape[1]
    xf = x.reshape(batch * seq, d)
    for l in range(depth):
        w_in_t = jnp.swapaxes(w_in, 1, 2)
        n_in = w_in.shape[2]
        whole = lambda w: (w, l, 0, w.shape[1], w.shape[1], False)
        head, w1_gate, w1_up, w1_down, w_uv, w_zxd = _ffn_head(
            xf, _row(ffn1_norm_pre[l]), _row(ffn1_norm_post[l]), ffn1_w_gate, ffn1_w_up, ffn1_w_down, l,
            ((w_in_t, l, 0, 2 * a_width, 2 * a_width, False),
             (w_in_t, l, 2 * a_width, n_in - 2 * a_width, ZXD_WIDTH, True)))
        (xf,) = _ffn(xf, _row(ffn1_norm_pre[l]), _row(ffn1_norm_post[l]), w1_gate, w1_up, w1_down, head=head)

        g_mix = _row(mix_norm_pre[l])
        b_full = jnp.broadcast_to(b_spatial[l][:, :, None], b_spatial[l].shape + (A_HEAD_DIM,))
        y_a, xn = _inproj_sgu(xf, g_mix, w_uv, _row(sgu_norm[l]), w_spatial[l], b_full)

        pad_heads = lambda v: jnp.pad(v.astype(F32), (0, LANES - B_HEADS)).reshape(1, LANES)
        y_b, w2_gate, w2_up, w2_down, wo = _inproj_ssd(
            xn, w_zxd, batch, conv_w[l], _row(conv_b[l]), pad_heads(dt_bias[l]), pad_heads(a_log[l]),
            _row(jnp.repeat(d_skip[l], B_HEAD_DIM)), _row(ssm_norm[l]),
            (whole(ffn2_w_gate), whole(ffn2_w_up), whole(ffn2_w_down), whole(w_out)))

        xf = _outproj(xf, y_a, y_b, wo, _row(mix_norm_post[l]))

        (xf,) = _ffn(xf, _row(ffn2_norm_pre[l]), _row(ffn2_norm_post[l]), w2_gate, w2_up, w2_down)
    return xf.reshape(batch, seq, d)
```

```python
import functools

import jax
import jax.numpy as jnp
from jax import lax
from jax.experimental import pallas as pl
from jax.experimental.pallas import tpu as pltpu

F32 = jnp.float32
BF16 = jnp.bfloat16

EPS = 1e-6
HALF_STEP = 0.5
SQRT_HALF = 0.7071067811865476

A_HEAD_DIM = 128
A_CHUNK = 128
B_HEAD_DIM = 64
B_HEADS = 16
B_GROUPS = 2
B_STATE = 128
B_CONV = 4
B_CHUNK = 256
B_WIDTH = B_HEADS * B_HEAD_DIM
B_GROUP_WIDTH = B_WIDTH // B_GROUPS
B_CONV_DIM = B_WIDTH + 2 * B_GROUPS * B_STATE

LANES = 128
SUBLANES = 8
BF16_SUBLANES = 16
VMEM_LIMIT_BYTES = 60 * 1024 * 1024

FFN_ROW_TILE = 1024
FFN_FF_TILE = 512
FFN_FF_SUB = 256
FFN_FINISH_ROWS = 256
FFN_HEAD_FF_TILE = 256
PROJ_ROW_TILE = 1024
SPLIT_PIECES = 2
PROJ_PIECE = 256


def _rmsnorm(x, g):
    return x * lax.rsqrt(jnp.mean(x * x, axis=-1, keepdims=True) + EPS) * g


def _gelu(x):
    return 0.5 * x * (1.0 + lax.erf(x * SQRT_HALF))


def _silu(x):
    return x * jax.nn.sigmoid(x)


def _dot(a, b):
    return jnp.dot(a, b, preferred_element_type=F32)


def _dot_nt(a, b_t):
    return lax.dot_general(a, b_t, (((1,), (1,)), ((), ())), preferred_element_type=F32)


def _split3(v):
    hi = v.astype(BF16)
    r1 = v - hi.astype(F32)
    mid = r1.astype(BF16)
    lo = (r1 - mid.astype(F32)).astype(BF16)
    return hi, mid, lo


def _params(semantics):
    return pltpu.CompilerParams(dimension_semantics=semantics, vmem_limit_bytes=VMEM_LIMIT_BYTES)


def _slab_rows(rows, max_slabs, align):
    for slab in range(align, rows + 1, align):
        if rows % slab == 0 and rows // slab <= max_slabs:
            return slab
    raise ValueError(f"no slab size for {rows} rows in {max_slabs} steps")


def _side_cast_specs(side_casts, flat_step, total_steps):
    inputs, in_specs, out_shapes, out_specs, plans = [], [], [], [], []
    for src, layer, row_start, n_rows, out_rows, transpose in side_casts:
        n_cols = src.shape[2]
        slab_rows = _slab_rows(out_rows, total_steps, LANES if transpose else BF16_SUBLANES)
        assert row_start % slab_rows == 0
        n_real, n_out, first = pl.cdiv(n_rows, slab_rows), out_rows // slab_rows, row_start // slab_rows
        src_slab = lambda *idx, n=n_real, first=first, layer=layer: (
            layer, first + jnp.minimum(flat_step(*idx), n - 1), 0)
        dst_slab = lambda *idx, n=n_out: jnp.minimum(flat_step(*idx), n - 1)
        inputs.append(src)
        in_specs.append(pl.BlockSpec((None, slab_rows, n_cols), src_slab))
        if transpose:
            out_shapes.append(jax.ShapeDtypeStruct((n_cols, out_rows), BF16))
            out_specs.append(pl.BlockSpec((n_cols, slab_rows), lambda *idx, f=dst_slab: (0, f(*idx))))
        else:
            out_shapes.append(jax.ShapeDtypeStruct((out_rows, n_cols), BF16))
            out_specs.append(pl.BlockSpec((slab_rows, n_cols), lambda *idx, f=dst_slab: (f(*idx), 0)))
        plans.append((n_rows, n_out, transpose))
    return inputs, in_specs, out_shapes, out_specs, tuple(plans)


def _run_side_casts(step, src_refs, dst_refs, plans):
    for src_ref, dst_ref, (n_rows, n_slabs, transpose) in zip(src_refs, dst_refs, plans):
        slab = src_ref[...]
        if n_rows != n_slabs * src_ref.shape[0]:
            row = jnp.minimum(step, n_slabs - 1) * src_ref.shape[0] + lax.broadcasted_iota(jnp.int32, slab.shape, 0)
            slab = jnp.where(row < n_rows, slab, 0.0)
        dst_ref[...] = (slab.T if transpose else slab).astype(BF16)


def _swiglu_partial(xn, wg, wu, wd, ff_sub, after_first=None, merge_narrow=False):
    width = wg.shape[1]
    acts = []
    for start in range(0, width, ff_sub):
        cols = slice(start, min(start + ff_sub, width))
        sub = cols.stop - start
        if merge_narrow and 2 * sub <= ff_sub:
            both = _dot(xn, jnp.concatenate([wg[:, cols], wu[:, cols]], axis=1))
            gate, up = both[:, :sub], both[:, sub:]
        else:
            gate, up = _dot(xn, wg[:, cols]), _dot(xn, wu[:, cols])
        acts.append((_silu(gate) * up).astype(BF16))
        if start == 0 and after_first is not None:
            after_first()
    return _dot(jnp.concatenate(acts, axis=1), wd)


def _ffn_body(*refs, n_steps, tail_width, ff_sub, head_piece, casts):
    refs = list(refs)
    x_ref, gpre_ref, gpost_ref, wg_ref, wu_ref, wd_ref = refs[:6]
    del refs[:6]
    head_ref = refs.pop(0) if head_piece else None
    cast_srcs = refs[:len(casts)]
    o_ref = refs[len(casts)]
    cast_dsts = refs[len(casts) + 1:-1]
    xn_ref = refs[-1]
    i = pl.program_id(0)
    j = pl.program_id(1)
    tf = wg_ref.shape[1]
    computed = (i > 0) if head_piece else True

    def partial_out(width, rows=slice(None)):
        def side():
            if rows.start in (None, 0):
                _run_side_casts(i * n_steps + j, cast_srcs, cast_dsts, casts)
        return _swiglu_partial(xn_ref[rows, :], wg_ref[:, :width], wu_ref[:, :width], wd_ref[:width, :], ff_sub, side,
                               merge_narrow=rows.start is not None)

    if head_piece:
        @pl.when(jnp.logical_and(i == 0, j < o_ref.shape[0] // head_piece))
        def _():
            o_ref[pl.ds(pl.multiple_of(j * head_piece, head_piece), head_piece), :] = head_ref[...]

    @pl.when(jnp.logical_and(computed, j == 0))
    def _():
        xn_ref[...] = _rmsnorm(x_ref[...], gpre_ref[...]).astype(BF16)
        o_ref[...] = partial_out(tf)

    @pl.when(jnp.logical_and(computed, jnp.logical_and(j > 0, j < n_steps - 1)))
    def _():
        o_ref[...] += partial_out(tf)

    @pl.when(jnp.logical_and(computed, j == n_steps - 1))
    def _():
        for start in range(0, x_ref.shape[0], FFN_FINISH_ROWS):
            rows = slice(start, start + FFN_FINISH_ROWS)
            h = o_ref[rows, :] + partial_out(tail_width, rows)
            o_ref[rows, :] = x_ref[rows, :] + HALF_STEP * _rmsnorm(h, gpost_ref[...])


def _ffn(x, g_pre, g_post, wg, wu, wd, head=None, side_casts=()):
    t, d = x.shape
    f = wg.shape[1]
    tm, tf, ff_sub = FFN_ROW_TILE, FFN_FF_TILE, FFN_FF_SUB
    n_steps = pl.cdiv(f, tf)
    tail_width = f - (n_steps - 1) * tf
    assert t % tm == 0 and n_steps >= 2 and tail_width % LANES == 0
    row = pl.BlockSpec((tm, d), lambda i, j: (i, 0))
    vec = pl.BlockSpec((1, d), lambda i, j: (0, 0))
    ff_tile = (lambda i, j: jnp.where(i == 0, 0, j)) if head is not None else (lambda i, j: j)
    head_in, head_specs, head_piece = [], [], None
    if head is not None:
        assert head.shape == (tm, d)
        head_piece = _slab_rows(tm, n_steps, SUBLANES)
        n_head = tm // head_piece
        head_in = [head]
        head_specs = [pl.BlockSpec((head_piece, d), lambda i, j: (jnp.where(i == 0, jnp.minimum(j, n_head - 1), n_head - 1), 0))]
    assert head is None or not side_casts
    cast_in, cast_in_specs, cast_shapes, cast_out_specs, cast_plans = _side_cast_specs(
        side_casts, lambda i, j: i * n_steps + j, (t // tm) * n_steps)
    return pl.pallas_call(
        functools.partial(_ffn_body, n_steps=n_steps, tail_width=tail_width, ff_sub=ff_sub,
                          head_piece=head_piece, casts=cast_plans),
        grid=(t // tm, n_steps),
        in_specs=[pl.BlockSpec((tm, d), lambda i, j: (jnp.maximum(i, 1), 0)) if head is not None else row, vec, vec,
                  pl.BlockSpec((d, tf), lambda i, j: (0, ff_tile(i, j))),
                  pl.BlockSpec((d, tf), lambda i, j: (0, ff_tile(i, j))),
                  pl.BlockSpec((tf, d), lambda i, j: (ff_tile(i, j), 0))] + head_specs + cast_in_specs,
        out_specs=[row] + cast_out_specs,
        out_shape=[jax.ShapeDtypeStruct((t, d), F32)] + cast_shapes,
        scratch_shapes=[pltpu.VMEM((tm, d), BF16)],
        compiler_params=_params(("arbitrary", "arbitrary") if side_casts else ("parallel", "arbitrary")),
        name="ffn",
    )(x, g_pre, g_post, wg, wu, wd, *head_in, *cast_in)


def _ffn_head_body(*refs, n_steps, tail_width, ff_sub, casts):
    x_ref, gpre_ref, gpost_ref, wg32_ref, wu32_ref, wd32_ref = refs[:6]
    cast_srcs = refs[6:6 + len(casts)]
    o_ref, wg16_ref, wu16_ref, wd16_ref = refs[6 + len(casts):10 + len(casts)]
    cast_dsts = refs[10 + len(casts):-1]
    xn_ref = refs[-1]
    j = pl.program_id(0)
    tf = wg32_ref.shape[1]

    def partial_out(width):
        wg = wg32_ref[:, :width].astype(BF16)
        wu = wu32_ref[:, :width].astype(BF16)
        wd = wd32_ref[:width, :].astype(BF16)
        wg16_ref[:, :width] = wg
        wu16_ref[:, :width] = wu
        wd16_ref[:width, :] = wd
        return _swiglu_partial(xn_ref[...], wg, wu, wd, ff_sub,
                               lambda: _run_side_casts(j, cast_srcs, cast_dsts, casts))

    @pl.when(j == 0)
    def _():
        xn_ref[...] = _rmsnorm(x_ref[...], gpre_ref[...]).astype(BF16)
        o_ref[...] = partial_out(tf)

    @pl.when(jnp.logical_and(j > 0, j < n_steps - 1))
    def _():
        o_ref[...] += partial_out(tf)

    @pl.when(j == n_steps - 1)
    def _():
        h = o_ref[...] + partial_out(tail_width)
        o_ref[...] = x_ref[...] + HALF_STEP * _rmsnorm(h, gpost_ref[...])


def _ffn_head(x, g_pre, g_post, w_gate, w_up, w_down, layer, side_casts=()):
    d = x.shape[1]
    f = w_gate.shape[2]
    tm, tf, ff_sub = FFN_ROW_TILE, FFN_HEAD_FF_TILE, FFN_FF_SUB
    n_steps = pl.cdiv(f, tf)
    tail_width = f - (n_steps - 1) * tf
    assert n_steps >= 2 and tail_width % LANES == 0
    const = lambda j: (0, 0)
    vec = pl.BlockSpec((1, d), const)
    cast_in, cast_in_specs, cast_shapes, cast_out_specs, cast_plans = _side_cast_specs(
        side_casts, lambda j: j, n_steps)
    return pl.pallas_call(
        functools.partial(_ffn_head_body, n_steps=n_steps, tail_width=tail_width, ff_sub=ff_sub, casts=cast_plans),
        grid=(n_steps,),
        in_specs=[pl.BlockSpec((tm, d), const, pipeline_mode=pl.Buffered(1)), vec, vec,
                  pl.BlockSpec((None, d, tf), lambda j: (layer, 0, j)),
                  pl.BlockSpec((None, d, tf), lambda j: (layer, 0, j)),
                  pl.BlockSpec((None, tf, d), lambda j: (layer, j, 0))] + cast_in_specs,
        out_specs=[pl.BlockSpec((tm, d), const),
                   pl.BlockSpec((d, tf), lambda j: (0, j)),
                   pl.BlockSpec((d, tf), lambda j: (0, j)),
                   pl.BlockSpec((tf, d), lambda j: (j, 0))] + cast_out_specs,
        out_shape=[jax.ShapeDtypeStruct((tm, d), F32),
                   jax.ShapeDtypeStruct((d, f), BF16), jax.ShapeDtypeStruct((d, f), BF16),
                   jax.ShapeDtypeStruct((f, d), BF16)] + cast_shapes,
        scratch_shapes=[pltpu.VMEM((tm, d), BF16)],
        compiler_params=_params(("arbitrary",)),
        name="ffn_head",
    )(x, g_pre, g_post, w_gate, w_up, w_down, *cast_in)


def _sgu_body(x_ref, gpre_ref, wu_ref, wv_ref, gs_ref, ws_ref, bs_ref, o_ref, xn_ref, u_ref, v_ref):
    tm = x_ref.shape[0]
    heads = ws_ref.shape[0]
    xn = _rmsnorm(x_ref[...], gpre_ref[...]).astype(BF16)
    xn_ref[...] = xn
    v_ref[...] = _rmsnorm(_gelu(_dot_nt(xn, wv_ref[...])), gs_ref[...]).astype(BF16)
    u_ref[...] = _gelu(_dot_nt(xn, wu_ref[...]))
    rows = lax.broadcasted_iota(jnp.int32, (A_CHUNK, A_CHUNK), 0)
    cols = lax.broadcasted_iota(jnp.int32, (A_CHUNK, A_CHUNK), 1)
    causal = rows >= cols
    for h in range(heads):
        w = jnp.where(causal, ws_ref[h], 0.0).astype(BF16)
        bias = bs_ref[h]
        lanes = pl.ds(h * A_HEAD_DIM, A_HEAD_DIM)
        for c in range(tm // A_CHUNK):
            rws = pl.ds(c * A_CHUNK, A_CHUNK)
            mixed = _dot(w, v_ref[rws, lanes]) + bias
            o_ref[rws, lanes] = (u_ref[rws, lanes] * mixed).astype(BF16)


def _inproj_sgu(x, g_pre, w_uv, g_sgu, w_spatial, b_full):
    t, d = x.shape
    aw = w_uv.shape[0] // 2
    heads = w_spatial.shape[0]
    tm = PROJ_ROW_TILE
    assert t % tm == 0 and tm % A_CHUNK == 0 and aw == heads * A_HEAD_DIM
    const2 = lambda i: (0, 0)
    const3 = lambda i: (0, 0, 0)
    return pl.pallas_call(
        _sgu_body,
        grid=(t // tm,),
        in_specs=[
            pl.BlockSpec((tm, d), lambda i: (i, 0)),
            pl.BlockSpec((1, d), const2),
            pl.BlockSpec((aw, d), const2, pipeline_mode=pl.Buffered(1)),
            pl.BlockSpec((aw, d), lambda i: (1, 0), pipeline_mode=pl.Buffered(1)),
            pl.BlockSpec((1, aw), const2),
            pl.BlockSpec((heads, A_CHUNK, A_CHUNK), const3),
            pl.BlockSpec((heads, A_CHUNK, A_HEAD_DIM), const3),
        ],
        out_specs=[pl.BlockSpec((tm, aw), lambda i: (i, 0)), pl.BlockSpec((tm, d), lambda i: (i, 0))],
        out_shape=[jax.ShapeDtypeStruct((t, aw), BF16), jax.ShapeDtypeStruct((t, d), BF16)],
        scratch_shapes=[pltpu.VMEM((tm, aw), F32), pltpu.VMEM((tm, aw), BF16)],
        compiler_params=_params(("parallel",)),
        name="inproj_sgu",
    )(x, g_pre, w_uv, w_uv, g_sgu, w_spatial, b_full)


Z_OFF = 0
XBC_OFF = B_WIDTH
DT_OFF = B_WIDTH + B_CONV_DIM
ZXD_WIDTH = DT_OFF + LANES
CONV_HALO = SUBLANES


def _split_cols(v, pieces):
    parts = []
    rest = v
    for _ in range(pieces):
        part = rest.astype(BF16)
        parts.append(part)
        rest = rest - part.astype(F32)
    return jnp.concatenate(parts, axis=1)


def _ssd_mix(zxd_ref, convw_ref, convb_ref, dtb_ref, alog_ref, dskip_ref, norm_ref, o_ref,
             ext_ref, state_ref, y_ref, side_work=()):
    cl = B_CHUNK
    side_work = list(side_work)

    def side(n=1):
        for _ in range(n):
            if side_work:
                side_work.pop(0)()

    ext_ref[CONV_HALO:CONV_HALO + cl, :] = zxd_ref[:, XBC_OFF:XBC_OFF + B_CONV_DIM]
    ext = ext_ref[...]
    conv = convw_ref[0:1, :] * ext
    for k in range(1, B_CONV):
        conv = pltpu.roll(conv, 1, axis=0) + convw_ref[k:k + 1, :] * ext
    conv = conv[CONV_HALO:, :] + convb_ref[...]
    ext_ref[0:CONV_HALO, :] = ext_ref[cl:cl + CONV_HALO, :]
    side()
    xbc = _silu(conv)
    xs = xbc[:, :B_WIDTH]

    dt = jax.nn.softplus(zxd_ref[:, DT_OFF:DT_OFF + LANES] + dtb_ref[...])
    da = dt * (-jnp.exp(alog_ref[...]))
    rows = lax.broadcasted_iota(jnp.int32, (cl, cl), 0)
    cols = lax.broadcasted_iota(jnp.int32, (cl, cl), 1)
    causal = rows >= cols
    tril = jnp.where(causal, 1.0, 0.0).astype(BF16)
    acs = sum(_dot(tril, part) for part in _split3(da))
    acs_t = acs.T
    side()

    e_rows = lax.broadcasted_iota(jnp.int32, (SPLIT_PIECES * LANES, B_WIDTH), 0)
    e_cols = lax.broadcasted_iota(jnp.int32, (SPLIT_PIECES * LANES, B_WIDTH), 1)
    expand = jnp.where(e_cols // B_HEAD_DIM == e_rows % LANES, 1.0, 0.0).astype(BF16)
    acs_e = _dot(_split_cols(acs, SPLIT_PIECES), expand)
    dt_e = _dot(_split_cols(dt, SPLIT_PIECES), expand)

    x = xs * dt_e
    even_head = (lax.broadcasted_iota(jnp.int32, (cl, B_WIDTH), 1) // B_HEAD_DIM) % 2 == 0
    x_even = jnp.where(even_head, x, 0.0).astype(BF16)
    x_odd = jnp.where(even_head, 0.0, x).astype(BF16)
    acs_last = acs_e[cl - 1:cl, :]
    decay_from_start = jnp.exp(acs_e)
    x_to_end = (x * jnp.exp(acs_last - acs_e)).astype(BF16)
    chunk_decay = jnp.exp(acs_last)
    side()

    heads_per_group = B_HEADS // B_GROUPS
    for g in range(B_GROUPS):
        gl = slice(g * B_GROUP_WIDTH, (g + 1) * B_GROUP_WIDTH)
        b_off = B_WIDTH + g * B_STATE
        c_off = B_WIDTH + B_GROUPS * B_STATE + g * B_STATE
        bc_t = xbc[:, b_off:b_off + B_STATE].T.astype(BF16)
        cc = xbc[:, c_off:c_off + B_STATE].astype(BF16)
        cb = _dot(cc, bc_t)
        state = state_ref[g]
        y_off = _dot(cc, state.astype(BF16)) * decay_from_start[:, gl]
        state_ref[g] = state * chunk_decay[:, gl] + _dot(bc_t, x_to_end[:, gl])
        for pair in range(heads_per_group // 2):
            head_a = g * heads_per_group + 2 * pair
            pl_off = head_a * B_HEAD_DIM
            y_pair = y_off[:, pair * LANES:(pair + 1) * LANES]
            for head, x_half in ((head_a, x_even), (head_a + 1, x_odd)):
                diff = acs[:, head:head + 1] - acs_t[head:head + 1, :]
                m = (cb * jnp.exp(jnp.where(causal, diff, -jnp.inf))).astype(BF16)
                y_pair = y_pair + _dot(m, x_half[:, pl_off:pl_off + LANES])
            y_ref[:, pl_off:pl_off + LANES] = y_pair
            side()

    side(len(side_work))
    y = (y_ref[...] + dskip_ref[...] * xs) * _silu(zxd_ref[:, Z_OFF:Z_OFF + B_WIDTH])
    for g in range(B_GROUPS):
        gl = slice(g * B_GROUP_WIDTH, (g + 1) * B_GROUP_WIDTH)
        yg = y[:, gl]
        yg = yg * lax.rsqrt(jnp.mean(yg * yg, axis=-1, keepdims=True) + EPS)
        o_ref[:, gl] = (yg * norm_ref[:, gl]).astype(BF16)


def _ssd_body(*refs, nc, n_chunks, casts):
    xn_ref, w_ref, convw_ref, convb_ref, dtb_ref, alog_ref, dskip_ref, norm_ref = refs[:8]
    cast_srcs = refs[8:8 + len(casts)]
    o_ref = refs[8 + len(casts)]
    cast_dsts = refs[9 + len(casts):-5]
    zxd0_ref, zxd1_ref, ext_ref, state_ref, y_ref = refs[-5:]
    s = pl.program_id(0)
    bufs = (zxd0_ref, zxd1_ref)

    def project_pieces(dst_ref):
        def piece(start):
            cols = slice(start, min(start + PROJ_PIECE, ZXD_WIDTH))

            def run():
                dst_ref[:, cols] = _dot(xn_ref[...], w_ref[:, cols])
            return run

        def casts_then(run):
            def both():
                _run_side_casts(s, cast_srcs, cast_dsts, casts)
                run()
            return both
        pieces = [piece(start) for start in range(0, ZXD_WIDTH, PROJ_PIECE)]
        return pieces[:-1] + [casts_then(pieces[-1])]

    def mix(src_ref, side_work=()):
        _ssd_mix(src_ref, convw_ref, convb_ref, dtb_ref, alog_ref, dskip_ref, norm_ref, o_ref,
                 ext_ref, state_ref, y_ref, side_work)

    @pl.when(lax.rem(s - 1, nc) == 0)
    def _():
        ext_ref[0:CONV_HALO, :] = jnp.zeros((CONV_HALO, B_CONV_DIM), F32)
        state_ref[...] = jnp.zeros_like(state_ref)

    @pl.when(s == 0)
    def _():
        for run in project_pieces(bufs[0]):
            run()

    for parity in range(2):
        @pl.when(jnp.logical_and(jnp.logical_and(s > 0, s < n_chunks), lax.rem(s, 2) == parity))
        def _():
            mix(bufs[1 - parity], project_pieces(bufs[parity]))

    @pl.when(s == n_chunks)
    def _():
        _run_side_casts(s, cast_srcs, cast_dsts, casts)
        mix(bufs[(n_chunks - 1) % 2])


def _inproj_ssd(xn, w_zxd, batch, conv_w, conv_b, dt_bias, a_log, d_skip, ssm_norm, side_casts=()):
    t, d = xn.shape
    seq = t // batch
    assert seq % B_CHUNK == 0 and w_zxd.shape[1] == ZXD_WIDTH
    nc = seq // B_CHUNK
    n_chunks = batch * nc
    const = lambda s: (0, 0)
    cast_in, cast_in_specs, cast_shapes, cast_out_specs, cast_plans = _side_cast_specs(
        side_casts, lambda s: s, n_chunks + 1)
    return pl.pallas_call(
        functools.partial(_ssd_body, nc=nc, n_chunks=n_chunks, casts=cast_plans),
        grid=(n_chunks + 1,),
        in_specs=[
            pl.BlockSpec((B_CHUNK, d), lambda s: (jnp.minimum(s, n_chunks - 1), 0)),
            pl.BlockSpec((d, ZXD_WIDTH), const),
            pl.BlockSpec((B_CONV, B_CONV_DIM), const),
            pl.BlockSpec((1, B_CONV_DIM), const),
            pl.BlockSpec((1, LANES), const),
            pl.BlockSpec((1, LANES), const),
            pl.BlockSpec((1, B_WIDTH), const),
            pl.BlockSpec((1, B_WIDTH), const),
        ] + cast_in_specs,
        out_specs=[pl.BlockSpec((B_CHUNK, B_WIDTH), lambda s: (jnp.maximum(s - 1, 0), 0))] + cast_out_specs,
        out_shape=[jax.ShapeDtypeStruct((t, B_WIDTH), BF16)] + cast_shapes,
        scratch_shapes=[
            pltpu.VMEM((B_CHUNK, ZXD_WIDTH), F32),
            pltpu.VMEM((B_CHUNK, ZXD_WIDTH), F32),
            pltpu.VMEM((CONV_HALO + B_CHUNK, B_CONV_DIM), F32),
            pltpu.VMEM((B_GROUPS, B_STATE, B_GROUP_WIDTH), F32),
            pltpu.VMEM((B_CHUNK, B_WIDTH), F32),
        ],
        compiler_params=_params(("arbitrary",)),
        name="inproj_ssd",
    )(xn, w_zxd, conv_w, conv_b, dt_bias, a_log, d_skip, ssm_norm, *cast_in)


def _outproj_body(x_ref, ya_ref, yb_ref, w_ref, g_ref, o_ref):
    h = _dot(jnp.concatenate([ya_ref[...], yb_ref[...]], axis=1), w_ref[...])
    o_ref[...] = x_ref[...] + _rmsnorm(h, g_ref[...])


def _outproj(x, y_a, y_b, w_out, g_post):
    t, d = x.shape
    ka, kb = y_a.shape[1], y_b.shape[1]
    tm = PROJ_ROW_TILE
    assert t % tm == 0 and ka == kb and w_out.shape[0] == ka + kb
    const = lambda i: (0, 0)
    return pl.pallas_call(
        _outproj_body,
        grid=(t // tm,),
        in_specs=[
            pl.BlockSpec((tm, d), lambda i: (i, 0)),
            pl.BlockSpec((tm, ka), lambda i: (i, 0)),
            pl.BlockSpec((tm, kb), lambda i: (i, 0)),
            pl.BlockSpec((ka + kb, d), const, pipeline_mode=pl.Buffered(1)),
            pl.BlockSpec((1, d), const),
        ],
        out_specs=pl.BlockSpec((tm, d), lambda i: (i, 0)),
        out_shape=jax.ShapeDtypeStruct((t, d), F32),
        compiler_params=_params(("parallel",)),
        name="outproj",
    )(x, y_a, y_b, w_out, g_post)


def _row(v):
    return v.reshape(1, -1).astype(F32)


def kernel(x, ffn1_norm_pre, ffn1_norm_post, ffn1_w_gate, ffn1_w_up, ffn1_w_down, mix_norm_pre, mix_norm_post, w_in, sgu_norm, w_spatial, b_spatial, conv_w, conv_b, dt_bias, a_log, d_skip, ssm_norm, w_out, ffn2_norm_pre, ffn2_norm_post, ffn2_w_gate, ffn2_w_up, ffn2_w_down):
    batch, seq, d = x.shape
    depth = ffn1_norm_pre.shape[0]
    a_width = sgu_norm.shape[1]
    xf = x.reshape(batch * seq, d)
    for l in range(depth):
        w_in_t = jnp.swapaxes(w_in, 1, 2)
        n_in = w_in.shape[2]
        whole = lambda w: (w, l, 0, w.shape[1], w.shape[1], False)
        head, w1_gate, w1_up, w1_down, w_uv, w_zxd = _ffn_head(
            xf, _row(ffn1_norm_pre[l]), _row(ffn1_norm_post[l]), ffn1_w_gate, ffn1_w_up, ffn1_w_down, l,
            ((w_in_t, l, 0, 2 * a_width, 2 * a_width, False),
             (w_in_t, l, 2 * a_width, n_in - 2 * a_width, ZXD_WIDTH, True)))
        (xf,) = _ffn(xf, _row(ffn1_norm_pre[l]), _row(ffn1_norm_post[l]), w1_gate, w1_up, w1_down, head=head)

        g_mix = _row(mix_norm_pre[l])
        b_full = jnp.broadcast_to(b_spatial[l][:, :, None], b_spatial[l].shape + (A_HEAD_DIM,))
        y_a, xn = _inproj_sgu(xf, g_mix, w_uv, _row(sgu_norm[l]), w_spatial[l], b_full)

        pad_heads = lambda v: jnp.pad(v.astype(F32), (0, LANES - B_HEADS)).reshape(1, LANES)
        y_b, w2_gate, w2_up, w2_down, wo = _inproj_ssd(
            xn, w_zxd, batch, conv_w[l], _row(conv_b[l]), pad_heads(dt_bias[l]), pad_heads(a_log[l]),
            _row(jnp.repeat(d_skip[l], B_HEAD_DIM)), _row(ssm_norm[l]),
            (whole(ffn2_w_gate), whole(ffn2_w_up), whole(ffn2_w_down), whole(w_out)))

        xf = _outproj(xf, y_a, y_b, wo, _row(mix_norm_post[l]))

        (xf,) = _ffn(xf, _row(ffn2_norm_pre[l]), _row(ffn2_norm_post[l]), w2_gate, w2_up, w2_down)
    return xf.reshape(batch, seq, d)
```

```python
import functools

import jax
import jax.numpy as jnp
from jax import lax
from jax.experimental import pallas as pl
from jax.experimental.pallas import tpu as pltpu

F32 = jnp.float32
BF16 = jnp.bfloat16

EPS = 1e-6
HALF_STEP = 0.5
SQRT_HALF = 0.7071067811865476

A_HEAD_DIM = 128
A_CHUNK = 128
B_HEAD_DIM = 64
B_HEADS = 16
B_GROUPS = 2
B_STATE = 128
B_CONV = 4
B_CHUNK = 256
B_WIDTH = B_HEADS * B_HEAD_DIM
B_GROUP_WIDTH = B_WIDTH // B_GROUPS
B_CONV_DIM = B_WIDTH + 2 * B_GROUPS * B_STATE

LANES = 128
SUBLANES = 8
BF16_SUBLANES = 16
VMEM_LIMIT_BYTES = 60 * 1024 * 1024

FFN_ROW_TILE = 1024
FFN_FF_TILE = 512
FFN_FF_SUB = 256
FFN_FINISH_ROWS = 256
FFN_HEAD_FF_TILE = 256
PROJ_ROW_TILE = 1024
SPLIT_PIECES = 2
PROJ_PIECE = 256


def _rmsnorm(x, g):
    return x * lax.rsqrt(jnp.mean(x * x, axis=-1, keepdims=True) + EPS) * g


def _gelu(x):
    return 0.5 * x * (1.0 + lax.erf(x * SQRT_HALF))


def _silu(x):
    return x * jax.nn.sigmoid(x)


def _dot(a, b):
    return jnp.dot(a, b, preferred_element_type=F32)


def _dot_nt(a, b_t):
    return lax.dot_general(a, b_t, (((1,), (1,)), ((), ())), preferred_element_type=F32)


def _split3(v):
    hi = v.astype(BF16)
    r1 = v - hi.astype(F32)
    mid = r1.astype(BF16)
    lo = (r1 - mid.astype(F32)).astype(BF16)
    return hi, mid, lo


def _params(semantics):
    return pltpu.CompilerParams(dimension_semantics=semantics, vmem_limit_bytes=VMEM_LIMIT_BYTES)


def _slab_rows(rows, max_slabs, align):
    for slab in range(align, rows + 1, align):
        if rows % slab == 0 and rows // slab <= max_slabs:
            return slab
    raise ValueError(f"no slab size for {rows} rows in {max_slabs} steps")


def _side_cast_specs(side_casts, flat_step, total_steps):
    inputs, in_specs, out_shapes, out_specs, plans = [], [], [], [], []
    for src, layer, row_start, n_rows, out_rows, transpose in side_casts:
        n_cols = src.shape[2]
        slab_rows = _slab_rows(out_rows, total_steps, LANES if transpose else BF16_SUBLANES)
        assert row_start % slab_rows == 0
        n_real, n_out, first = pl.cdiv(n_rows, slab_rows), out_rows // slab_rows, row_start // slab_rows
        src_slab = lambda *idx, n=n_real, first=first, layer=layer: (
            layer, first + jnp.minimum(flat_step(*idx), n - 1), 0)
        dst_slab = lambda *idx, n=n_out: jnp.minimum(flat_step(*idx), n - 1)
        inputs.append(src)
        in_specs.append(pl.BlockSpec((None, slab_rows, n_cols), src_slab))
        if transpose:
            out_shapes.append(jax.ShapeDtypeStruct((n_cols, out_rows), BF16))
            out_specs.append(pl.BlockSpec((n_cols, slab_rows), lambda *idx, f=dst_slab: (0, f(*idx))))
        else:
            out_shapes.append(jax.ShapeDtypeStruct((out_rows, n_cols), BF16))
            out_specs.append(pl.BlockSpec((slab_rows, n_cols), lambda *idx, f=dst_slab: (f(*idx), 0)))
        plans.append((n_rows, n_out, transpose))
    return inputs, in_specs, out_shapes, out_specs, tuple(plans)


def _run_side_casts(step, src_refs, dst_refs, plans):
    for src_ref, dst_ref, (n_rows, n_slabs, transpose) in zip(src_refs, dst_refs, plans):
        slab = src_ref[...]
        if n_rows != n_slabs * src_ref.shape[0]:
            row = jnp.minimum(step, n_slabs - 1) * src_ref.shape[0] + lax.broadcasted_iota(jnp.int32, slab.shape, 0)
            slab = jnp.where(row < n_rows, slab, 0.0)
        dst_ref[...] = (slab.T if transpose else slab).astype(BF16)


def _swiglu_partial(xn, wg, wu, wd, ff_sub, after_first=None, merge_narrow=False):
    width = wg.shape[1]
    acts = []
    for start in range(0, width, ff_sub):
        cols = slice(start, min(start + ff_sub, width))
        sub = cols.stop - start
        if merge_narrow and 2 * sub <= ff_sub:
            both = _dot(xn, jnp.concatenate([wg[:, cols], wu[:, cols]], axis=1))
            gate, up = both[:, :sub], both[:, sub:]
        else:
            gate, up = _dot(xn, wg[:, cols]), _dot(xn, wu[:, cols])
        acts.append((_silu(gate) * up).astype(BF16))
        if start == 0 and after_first is not None:
            after_first()
    return _dot(jnp.concatenate(acts, axis=1), wd)


def _ffn_body(*refs, n_steps, tail_width, ff_sub, head_piece, casts):
    refs = list(refs)
    x_ref, gpre_ref, gpost_ref, wg_ref, wu_ref, wd_ref = refs[:6]
    del refs[:6]
    head_ref = refs.pop(0) if head_piece else None
    cast_srcs = refs[:len(casts)]
    o_ref = refs[len(casts)]
    cast_dsts = refs[len(casts) + 1:-1]
    xn_ref = refs[-1]
    i = pl.program_id(0)
    j = pl.program_id(1)
    tf = wg_ref.shape[1]
    computed = (i > 0) if head_piece else True

    def partial_out(width, rows=slice(None)):
        def side():
            if rows.start in (None, 0):
                _run_side_casts(i * n_steps + j, cast_srcs, cast_dsts, casts)
        return _swiglu_partial(xn_ref[rows, :], wg_ref[:, :width], wu_ref[:, :width], wd_ref[:width, :], ff_sub, side,
                               merge_narrow=rows.start is not None)

    if head_piece:
        @pl.when(jnp.logical_and(i == 0, j < o_ref.shape[0] // head_piece))
        def _():
            o_ref[pl.ds(pl.multiple_of(j * head_piece, head_piece), head_piece), :] = head_ref[...]

    @pl.when(jnp.logical_and(computed, j == 0))
    def _():
        xn_ref[...] = _rmsnorm(x_ref[...], gpre_ref[...]).astype(BF16)
        o_ref[...] = partial_out(tf)

    @pl.when(jnp.logical_and(computed, jnp.logical_and(j > 0, j < n_steps - 1)))
    def _():
        o_ref[...] += partial_out(tf)

    @pl.when(jnp.logical_and(computed, j == n_steps - 1))
    def _():
        for start in range(0, x_ref.shape[0], FFN_FINISH_ROWS):
            rows = slice(start, start + FFN_FINISH_ROWS)
            h = o_ref[rows, :] + partial_out(tail_width, rows)
            o_ref[rows, :] = x_ref[rows, :] + HALF_STEP * _rmsnorm(h, gpost_ref[...])


def _ffn(x, g_pre, g_post, wg, wu, wd, head=None, side_casts=()):
    t, d = x.shape
    f = wg.shape[1]
    tm, tf, ff_sub = FFN_ROW_TILE, FFN_FF_TILE, FFN_FF_SUB
    n_steps = pl.cdiv(f, tf)
    tail_width = f - (n_steps - 1) * tf
    assert t % tm == 0 and n_steps >= 2 and tail_width % LANES == 0
    row = pl.BlockSpec((tm, d), lambda i, j: (i, 0))
    vec = pl.BlockSpec((1, d), lambda i, j: (0, 0))
    ff_tile = (lambda i, j: jnp.where(i == 0, 0, j)) if head is not None else (lambda i, j: j)
    head_in, head_specs, head_piece = [], [], None
    if head is not None:
        assert head.shape == (tm, d)
        head_piece = _slab_rows(tm, n_steps, SUBLANES)
        n_head = tm // head_piece
        head_in = [head]
        head_specs = [pl.BlockSpec((head_piece, d), lambda i, j: (jnp.where(i == 0, jnp.minimum(j, n_head - 1), n_head - 1), 0))]
    assert head is None or not side_casts
    cast_in, cast_in_specs, cast_shapes, cast_out_specs, cast_plans = _side_cast_specs(
        side_casts, lambda i, j: i * n_steps + j, (t // tm) * n_steps)
    return pl.pallas_call(
        functools.partial(_ffn_body, n_steps=n_steps, tail_width=tail_width, ff_sub=ff_sub,
                          head_piece=head_piece, casts=cast_plans),
        grid=(t // tm, n_steps),
        in_specs=[row, vec, vec,
                  pl.BlockSpec((d, tf), lambda i, j: (0, ff_tile(i, j))),
                  pl.BlockSpec((d, tf), lambda i, j: (0, ff_tile(i, j))),
                  pl.BlockSpec((tf, d), lambda i, j: (ff_tile(i, j), 0))] + head_specs + cast_in_specs,
        out_specs=[row] + cast_out_specs,
        out_shape=[jax.ShapeDtypeStruct((t, d), F32)] + cast_shapes,
        scratch_shapes=[pltpu.VMEM((tm, d), BF16)],
        compiler_params=_params(("arbitrary", "arbitrary") if side_casts else ("parallel", "arbitrary")),
        name="ffn",
    )(x, g_pre, g_post, wg, wu, wd, *head_in, *cast_in)


def _ffn_head_body(*refs, n_steps, tail_width, ff_sub, casts):
    x_ref, gpre_ref, gpost_ref, wg32_ref, wu32_ref, wd32_ref = refs[:6]
    cast_srcs = refs[6:6 + len(casts)]
    o_ref, wg16_ref, wu16_ref, wd16_ref = refs[6 + len(casts):10 + len(casts)]
    cast_dsts = refs[10 + len(casts):-1]
    xn_ref = refs[-1]
    j = pl.program_id(0)
    tf = wg32_ref.shape[1]

    def partial_out(width):
        wg = wg32_ref[:, :width].astype(BF16)
        wu = wu32_ref[:, :width].astype(BF16)
        wd = wd32_ref[:width, :].astype(BF16)
        wg16_ref[:, :width] = wg
        wu16_ref[:, :width] = wu
        wd16_ref[:width, :] = wd
        return _swiglu_partial(xn_ref[...], wg, wu, wd, ff_sub,
                               lambda: _run_side_casts(j, cast_srcs, cast_dsts, casts))

    @pl.when(j == 0)
    def _():
        xn_ref[...] = _rmsnorm(x_ref[...], gpre_ref[...]).astype(BF16)
        o_ref[...] = partial_out(tf)

    @pl.when(jnp.logical_and(j > 0, j < n_steps - 1))
    def _():
        o_ref[...] += partial_out(tf)

    @pl.when(j == n_steps - 1)
    def _():
        h = o_ref[...] + partial_out(tail_width)
        o_ref[...] = x_ref[...] + HALF_STEP * _rmsnorm(h, gpost_ref[...])


def _ffn_head(x, g_pre, g_post, w_gate, w_up, w_down, layer, side_casts=()):
    d = x.shape[1]
    f = w_gate.shape[2]
    tm, tf, ff_sub = FFN_ROW_TILE, FFN_HEAD_FF_TILE, FFN_FF_SUB
    n_steps = pl.cdiv(f, tf)
    tail_width = f - (n_steps - 1) * tf
    assert n_steps >= 2 and tail_width % LANES == 0
    const = lambda j: (0, 0)
    vec = pl.BlockSpec((1, d), const)
    cast_in, cast_in_specs, cast_shapes, cast_out_specs, cast_plans = _side_cast_specs(
        side_casts, lambda j: j, n_steps)
    return pl.pallas_call(
        functools.partial(_ffn_head_body, n_steps=n_steps, tail_width=tail_width, ff_sub=ff_sub, casts=cast_plans),
        grid=(n_steps,),
        in_specs=[pl.BlockSpec((tm, d), const, pipeline_mode=pl.Buffered(1)), vec, vec,
                  pl.BlockSpec((None, d, tf), lambda j: (layer, 0, j)),
                  pl.BlockSpec((None, d, tf), lambda j: (layer, 0, j)),
                  pl.BlockSpec((None, tf, d), lambda j: (layer, j, 0))] + cast_in_specs,
        out_specs=[pl.BlockSpec((tm, d), const),
                   pl.BlockSpec((d, tf), lambda j: (0, j)),
                   pl.BlockSpec((d, tf), lambda j: (0, j)),
                   pl.BlockSpec((tf, d), lambda j: (j, 0))] + cast_out_specs,
        out_shape=[jax.ShapeDtypeStruct((tm, d), F32),
                   jax.ShapeDtypeStruct((d, f), BF16), jax.ShapeDtypeStruct((d, f), BF16),
                   jax.ShapeDtypeStruct((f, d), BF16)] + cast_shapes,
        scratch_shapes=[pltpu.VMEM((tm, d), BF16)],
        compiler_params=_params(("arbitrary",)),
        name="ffn_head",
    )(x, g_pre, g_post, w_gate, w_up, w_down, *cast_in)


def _sgu_body(x_ref, gpre_ref, wu_ref, wv_ref, gs_ref, ws_ref, bs_ref, o_ref, xn_ref, u_ref, v_ref):
    tm = x_ref.shape[0]
    heads = ws_ref.shape[0]
    xn = _rmsnorm(x_ref[...], gpre_ref[...]).astype(BF16)
    xn_ref[...] = xn
    u_ref[...] = _gelu(_dot_nt(xn, wu_ref[...]))
    v_ref[...] = _rmsnorm(_gelu(_dot_nt(xn, wv_ref[...])), gs_ref[...]).astype(BF16)
    rows = lax.broadcasted_iota(jnp.int32, (A_CHUNK, A_CHUNK), 0)
    cols = lax.broadcasted_iota(jnp.int32, (A_CHUNK, A_CHUNK), 1)
    causal = rows >= cols
    for h in range(heads):
        w = jnp.where(causal, ws_ref[h], 0.0).astype(BF16)
        bias = bs_ref[h]
        lanes = pl.ds(h * A_HEAD_DIM, A_HEAD_DIM)
        for c in range(tm // A_CHUNK):
            rws = pl.ds(c * A_CHUNK, A_CHUNK)
            mixed = _dot(w, v_ref[rws, lanes]) + bias
            o_ref[rws, lanes] = (u_ref[rws, lanes] * mixed).astype(BF16)


def _inproj_sgu(x, g_pre, w_uv, g_sgu, w_spatial, b_full):
    t, d = x.shape
    aw = w_uv.shape[0] // 2
    heads = w_spatial.shape[0]
    tm = PROJ_ROW_TILE
    assert t % tm == 0 and tm % A_CHUNK == 0 and aw == heads * A_HEAD_DIM
    const2 = lambda i: (0, 0)
    const3 = lambda i: (0, 0, 0)
    return pl.pallas_call(
        _sgu_body,
        grid=(t // tm,),
        in_specs=[
            pl.BlockSpec((tm, d), lambda i: (i, 0)),
            pl.BlockSpec((1, d), const2),
            pl.BlockSpec((aw, d), const2, pipeline_mode=pl.Buffered(1)),
            pl.BlockSpec((aw, d), lambda i: (1, 0), pipeline_mode=pl.Buffered(1)),
            pl.BlockSpec((1, aw), const2),
            pl.BlockSpec((heads, A_CHUNK, A_CHUNK), const3),
            pl.BlockSpec((heads, A_CHUNK, A_HEAD_DIM), const3),
        ],
        out_specs=[pl.BlockSpec((tm, aw), lambda i: (i, 0)), pl.BlockSpec((tm, d), lambda i: (i, 0))],
        out_shape=[jax.ShapeDtypeStruct((t, aw), BF16), jax.ShapeDtypeStruct((t, d), BF16)],
        scratch_shapes=[pltpu.VMEM((tm, aw), F32), pltpu.VMEM((tm, aw), BF16)],
        compiler_params=_params(("parallel",)),
        name="inproj_sgu",
    )(x, g_pre, w_uv, w_uv, g_sgu, w_spatial, b_full)


Z_OFF = 0
XBC_OFF = B_WIDTH
DT_OFF = B_WIDTH + B_CONV_DIM
ZXD_WIDTH = DT_OFF + LANES
CONV_HALO = SUBLANES


def _split_cols(v, pieces):
    parts = []
    rest = v
    for _ in range(pieces):
        part = rest.astype(BF16)
        parts.append(part)
        rest = rest - part.astype(F32)
    return jnp.concatenate(parts, axis=1)


def _ssd_mix(zxd_ref, convw_ref, convb_ref, dtb_ref, alog_ref, dskip_ref, norm_ref, o_ref,
             ext_ref, state_ref, y_ref, side_work=()):
    cl = B_CHUNK
    side_work = list(side_work)

    def side(n=1):
        for _ in range(n):
            if side_work:
                side_work.pop(0)()

    ext_ref[CONV_HALO:CONV_HALO + cl, :] = zxd_ref[:, XBC_OFF:XBC_OFF + B_CONV_DIM]
    ext = ext_ref[...]
    conv = convw_ref[0:1, :] * ext
    for k in range(1, B_CONV):
        conv = pltpu.roll(conv, 1, axis=0) + convw_ref[k:k + 1, :] * ext
    conv = conv[CONV_HALO:, :] + convb_ref[...]
    ext_ref[0:CONV_HALO, :] = ext_ref[cl:cl + CONV_HALO, :]
    side()
    xbc = _silu(conv)
    xs = xbc[:, :B_WIDTH]

    dt = jax.nn.softplus(zxd_ref[:, DT_OFF:DT_OFF + LANES] + dtb_ref[...])
    da = dt * (-jnp.exp(alog_ref[...]))
    rows = lax.broadcasted_iota(jnp.int32, (cl, cl), 0)
    cols = lax.broadcasted_iota(jnp.int32, (cl, cl), 1)
    causal = rows >= cols
    tril = jnp.where(causal, 1.0, 0.0).astype(BF16)
    acs = sum(_dot(tril, part) for part in _split3(da))
    acs_t = acs.T
    side()

    e_rows = lax.broadcasted_iota(jnp.int32, (SPLIT_PIECES * LANES, B_WIDTH), 0)
    e_cols = lax.broadcasted_iota(jnp.int32, (SPLIT_PIECES * LANES, B_WIDTH), 1)
    expand = jnp.where(e_cols // B_HEAD_DIM == e_rows % LANES, 1.0, 0.0).astype(BF16)
    acs_e = _dot(_split_cols(acs, SPLIT_PIECES), expand)
    dt_e = _dot(_split_cols(dt, SPLIT_PIECES), expand)

    x = xs * dt_e
    even_head = (lax.broadcasted_iota(jnp.int32, (cl, B_WIDTH), 1) // B_HEAD_DIM) % 2 == 0
    x_even = jnp.where(even_head, x, 0.0).astype(BF16)
    x_odd = jnp.where(even_head, 0.0, x).astype(BF16)
    acs_last = acs_e[cl - 1:cl, :]
    decay_from_start = jnp.exp(acs_e)
    x_to_end = (x * jnp.exp(acs_last - acs_e)).astype(BF16)
    chunk_decay = jnp.exp(acs_last)
    side()

    heads_per_group = B_HEADS // B_GROUPS
    for g in range(B_GROUPS):
        gl = slice(g * B_GROUP_WIDTH, (g + 1) * B_GROUP_WIDTH)
        b_off = B_WIDTH + g * B_STATE
        c_off = B_WIDTH + B_GROUPS * B_STATE + g * B_STATE
        bc_t = xbc[:, b_off:b_off + B_STATE].T.astype(BF16)
        cc = xbc[:, c_off:c_off + B_STATE].astype(BF16)
        cb = _dot(cc, bc_t)
        state = state_ref[g]
        y_off = _dot(cc, state.astype(BF16)) * decay_from_start[:, gl]
        state_ref[g] = state * chunk_decay[:, gl] + _dot(bc_t, x_to_end[:, gl])
        for pair in range(heads_per_group // 2):
            head_a = g * heads_per_group + 2 * pair
            pl_off = head_a * B_HEAD_DIM
            y_pair = y_off[:, pair * LANES:(pair + 1) * LANES]
            for head, x_half in ((head_a, x_even), (head_a + 1, x_odd)):
                diff = acs[:, head:head + 1] - acs_t[head:head + 1, :]
                m = (cb * jnp.exp(jnp.where(causal, diff, -jnp.inf))).astype(BF16)
                y_pair = y_pair + _dot(m, x_half[:, pl_off:pl_off + LANES])
            y_ref[:, pl_off:pl_off + LANES] = y_pair
            side()

    side(len(side_work))
    y = (y_ref[...] + dskip_ref[...] * xs) * _silu(zxd_ref[:, Z_OFF:Z_OFF + B_WIDTH])
    for g in range(B_GROUPS):
        gl = slice(g * B_GROUP_WIDTH, (g + 1) * B_GROUP_WIDTH)
        yg = y[:, gl]
        yg = yg * lax.rsqrt(jnp.mean(yg * yg, axis=-1, keepdims=True) + EPS)
        o_ref[:, gl] = (yg * norm_ref[:, gl]).astype(BF16)


def _ssd_body(*refs, nc, n_chunks, casts):
    xn_ref, w_ref, convw_ref, convb_ref, dtb_ref, alog_ref, dskip_ref, norm_ref = refs[:8]
    cast_srcs = refs[8:8 + len(casts)]
    o_ref = refs[8 + len(casts)]
    cast_dsts = refs[9 + len(casts):-5]
    zxd0_ref, zxd1_ref, ext_ref, state_ref, y_ref = refs[-5:]
    s = pl.program_id(0)
    bufs = (zxd0_ref, zxd1_ref)

    def project_pieces(dst_ref):
        def piece(start):
            cols = slice(start, min(start + PROJ_PIECE, ZXD_WIDTH))

            def run():
                dst_ref[:, cols] = _dot(xn_ref[...], w_ref[:, cols])
            return run

        def casts_then(run):
            def both():
                _run_side_casts(s, cast_srcs, cast_dsts, casts)
                run()
            return both
        pieces = [piece(start) for start in range(0, ZXD_WIDTH, PROJ_PIECE)]
        return pieces[:-1] + [casts_then(pieces[-1])]

    def mix(src_ref, side_work=()):
        _ssd_mix(src_ref, convw_ref, convb_ref, dtb_ref, alog_ref, dskip_ref, norm_ref, o_ref,
                 ext_ref, state_ref, y_ref, side_work)

    @pl.when(lax.rem(s - 1, nc) == 0)
    def _():
        ext_ref[0:CONV_HALO, :] = jnp.zeros((CONV_HALO, B_CONV_DIM), F32)
        state_ref[...] = jnp.zeros_like(state_ref)

    @pl.when(s == 0)
    def _():
        for run in project_pieces(bufs[0]):
            run()

    for parity in range(2):
        @pl.when(jnp.logical_and(jnp.logical_and(s > 0, s < n_chunks), lax.rem(s, 2) == parity))
        def _():
            mix(bufs[1 - parity], project_pieces(bufs[parity]))

    @pl.when(s == n_chunks)
    def _():
        _run_side_casts(s, cast_srcs, cast_dsts, casts)
        mix(bufs[(n_chunks - 1) % 2])


def _inproj_ssd(xn, w_zxd, batch, conv_w, conv_b, dt_bias, a_log, d_skip, ssm_norm, side_casts=()):
    t, d = xn.shape
    seq = t // batch
    assert seq % B_CHUNK == 0 and w_zxd.shape[1] == ZXD_WIDTH
    nc = seq // B_CHUNK
    n_chunks = batch * nc
    const = lambda s: (0, 0)
    cast_in, cast_in_specs, cast_shapes, cast_out_specs, cast_plans = _side_cast_specs(
        side_casts, lambda s: s, n_chunks + 1)
    return pl.pallas_call(
        functools.partial(_ssd_body, nc=nc, n_chunks=n_chunks, casts=cast_plans),
        grid=(n_chunks + 1,),
        in_specs=[
            pl.BlockSpec((B_CHUNK, d), lambda s: (jnp.minimum(s, n_chunks - 1), 0)),
            pl.BlockSpec((d, ZXD_WIDTH), const),
            pl.BlockSpec((B_CONV, B_CONV_DIM), const),
            pl.BlockSpec((1, B_CONV_DIM), const),
            pl.BlockSpec((1, LANES), const),
            pl.BlockSpec((1, LANES), const),
            pl.BlockSpec((1, B_WIDTH), const),
            pl.BlockSpec((1, B_WIDTH), const),
        ] + cast_in_specs,
        out_specs=[pl.BlockSpec((B_CHUNK, B_WIDTH), lambda s: (jnp.maximum(s - 1, 0), 0))] + cast_out_specs,
        out_shape=[jax.ShapeDtypeStruct((t, B_WIDTH), BF16)] + cast_shapes,
        scratch_shapes=[
            pltpu.VMEM((B_CHUNK, ZXD_WIDTH), F32),
            pltpu.VMEM((B_CHUNK, ZXD_WIDTH), F32),
            pltpu.VMEM((CONV_HALO + B_CHUNK, B_CONV_DIM), F32),
            pltpu.VMEM((B_GROUPS, B_STATE, B_GROUP_WIDTH), F32),
            pltpu.VMEM((B_CHUNK, B_WIDTH), F32),
        ],
        compiler_params=_params(("arbitrary",)),
        name="inproj_ssd",
    )(xn, w_zxd, conv_w, conv_b, dt_bias, a_log, d_skip, ssm_norm, *cast_in)


def _outproj_body(x_ref, ya_ref, yb_ref, w_ref, g_ref, o_ref):
    h = _dot(jnp.concatenate([ya_ref[...], yb_ref[...]], axis=1), w_ref[...])
    o_ref[...] = x_ref[...] + _rmsnorm(h, g_ref[...])


def _outproj(x, y_a, y_b, w_out, g_post):
    t, d = x.shape
    ka, kb = y_a.shape[1], y_b.shape[1]
    tm = PROJ_ROW_TILE
    assert t % tm == 0 and ka == kb and w_out.shape[0] == ka + kb
    const = lambda i: (0, 0)
    return pl.pallas_call(
        _outproj_body,
        grid=(t // tm,),
        in_specs=[
            pl.BlockSpec((tm, d), lambda i: (i, 0)),
            pl.BlockSpec((tm, ka), lambda i: (i, 0)),
            pl.BlockSpec((tm, kb), lambda i: (i, 0)),
            pl.BlockSpec((ka + kb, d), const, pipeline_mode=pl.Buffered(1)),
            pl.BlockSpec((1, d), const),
        ],
        out_specs=pl.BlockSpec((tm, d), lambda i: (i, 0)),
        out_shape=jax.ShapeDtypeStruct((t, d), F32),
        compiler_params=_params(("parallel",)),
        name="outproj",
    )(x, y_a, y_b, w_out, g_post)


def _row(v):
    return v.reshape(1, -1).astype(F32)


def kernel(x, ffn1_norm_pre, ffn1_norm_post, ffn1_w_gate, ffn1_w_up, ffn1_w_down, mix_norm_pre, mix_norm_post, w_in, sgu_norm, w_spatial, b_spatial, conv_w, conv_b, dt_bias, a_log, d_skip, ssm_norm, w_out, ffn2_norm_pre, ffn2_norm_post, ffn2_w_gate, ffn2_w_up, ffn2_w_down):
    batch, seq, d = x.shape
    depth = ffn1_norm_pre.shape[0]
    a_width = sgu_norm.shape[1]
    xf = x.reshape(batch * seq, d)
    for l in range(depth):
        w_in_t = jnp.swapaxes(w_in, 1, 2)
        n_in = w_in.shape[2]
        whole = lambda w: (w, l, 0, w.shape[1], w.shape[1], False)
        head, w1_gate, w1_up, w1_down, w_uv, w_zxd = _ffn_head(
            xf, _row(ffn1_norm_pre[l]), _row(ffn1_norm_post[l]), ffn1_w_gate, ffn1_w_up, ffn1_w_down, l,
            ((w_in_t, l, 0, 2 * a_width, 2 * a_width, False),
             (w_in_t, l, 2 * a_width, n_in - 2 * a_width, ZXD_WIDTH, True)))
        (xf,) = _ffn(xf, _row(ffn1_norm_pre[l]), _row(ffn1_norm_post[l]), w1_gate, w1_up, w1_down, head=head)

        g_mix = _row(mix_norm_pre[l])
        b_full = jnp.broadcast_to(b_spatial[l][:, :, None], b_spatial[l].shape + (A_HEAD_DIM,))
        y_a, xn = _inproj_sgu(xf, g_mix, w_uv, _row(sgu_norm[l]), w_spatial[l], b_full)

        pad_heads = lambda v: jnp.pad(v.astype(F32), (0, LANES - B_HEADS)).reshape(1, LANES)
        y_b, wo = _inproj_ssd(
            xn, w_zxd, batch, conv_w[l], _row(conv_b[l]), pad_heads(dt_bias[l]), pad_heads(a_log[l]),
            _row(jnp.repeat(d_skip[l], B_HEAD_DIM)), _row(ssm_norm[l]), (whole(w_out),))

        xf = _outproj(xf, y_a, y_b, wo, _row(mix_norm_post[l]))

        head, w2_gate, w2_up, w2_down = _ffn_head(
            xf, _row(ffn2_norm_pre[l]), _row(ffn2_norm_post[l]), ffn2_w_gate, ffn2_w_up, ffn2_w_down, l)
        (xf,) = _ffn(xf, _row(ffn2_norm_pre[l]), _row(ffn2_norm_post[l]), w2_gate, w2_up, w2_down, head=head)
    return xf.reshape(batch, seq, d)
```

```python
import functools

import jax
import jax.numpy as jnp
from jax import lax
from jax.experimental import pallas as pl
from jax.experimental.pallas import tpu as pltpu

F32 = jnp.float32
BF16 = jnp.bfloat16

EPS = 1e-6
HALF_STEP = 0.5
SQRT_HALF = 0.7071067811865476

A_HEAD_DIM = 128
A_CHUNK = 128
B_HEAD_DIM = 64
B_HEADS = 16
B_GROUPS = 2
B_STATE = 128
B_CONV = 4
B_CHUNK = 256
B_WIDTH = B_HEADS * B_HEAD_DIM
B_GROUP_WIDTH = B_WIDTH // B_GROUPS
B_CONV_DIM = B_WIDTH + 2 * B_GROUPS * B_STATE

LANES = 128
SUBLANES = 8
BF16_SUBLANES = 16
VMEM_LIMIT_BYTES = 60 * 1024 * 1024

FFN_ROW_TILE = 1024
FFN_FF_TILE = 512
FFN_FF_SUB = 256
FFN_FINISH_ROWS = 256
FFN_HEAD_FF_TILE = 256
PROJ_ROW_TILE = 1024
SPLIT_PIECES = 2
PROJ_PIECE = 256


def _rmsnorm(x, g):
    return x * lax.rsqrt(jnp.mean(x * x, axis=-1, keepdims=True) + EPS) * g


def _gelu(x):
    return 0.5 * x * (1.0 + lax.erf(x * SQRT_HALF))


def _silu(x):
    return x * jax.nn.sigmoid(x)


def _dot(a, b):
    return jnp.dot(a, b, preferred_element_type=F32)


def _dot_nt(a, b_t):
    return lax.dot_general(a, b_t, (((1,), (1,)), ((), ())), preferred_element_type=F32)


def _split3(v):
    hi = v.astype(BF16)
    r1 = v - hi.astype(F32)
    mid = r1.astype(BF16)
    lo = (r1 - mid.astype(F32)).astype(BF16)
    return hi, mid, lo


def _params(semantics):
    return pltpu.CompilerParams(dimension_semantics=semantics, vmem_limit_bytes=VMEM_LIMIT_BYTES)


def _slab_rows(rows, max_slabs, align):
    for slab in range(align, rows + 1, align):
        if rows % slab == 0 and rows // slab <= max_slabs:
            return slab
    raise ValueError(f"no slab size for {rows} rows in {max_slabs} steps")


def _side_cast_specs(side_casts, flat_step, total_steps):
    inputs, in_specs, out_shapes, out_specs, plans = [], [], [], [], []
    for src, layer, row_start, n_rows, out_rows, transpose in side_casts:
        n_cols = src.shape[2]
        slab_rows = _slab_rows(out_rows, total_steps, LANES if transpose else BF16_SUBLANES)
        assert row_start % slab_rows == 0
        n_real, n_out, first = pl.cdiv(n_rows, slab_rows), out_rows // slab_rows, row_start // slab_rows
        src_slab = lambda *idx, n=n_real, first=first, layer=layer: (
            layer, first + jnp.minimum(flat_step(*idx), n - 1), 0)
        dst_slab = lambda *idx, n=n_out: jnp.minimum(flat_step(*idx), n - 1)
        inputs.append(src)
        in_specs.append(pl.BlockSpec((None, slab_rows, n_cols), src_slab))
        if transpose:
            out_shapes.append(jax.ShapeDtypeStruct((n_cols, out_rows), BF16))
            out_specs.append(pl.BlockSpec((n_cols, slab_rows), lambda *idx, f=dst_slab: (0, f(*idx))))
        else:
            out_shapes.append(jax.ShapeDtypeStruct((out_rows, n_cols), BF16))
            out_specs.append(pl.BlockSpec((slab_rows, n_cols), lambda *idx, f=dst_slab: (f(*idx), 0)))
        plans.append((n_rows, n_out, transpose))
    return inputs, in_specs, out_shapes, out_specs, tuple(plans)


def _run_side_casts(step, src_refs, dst_refs, plans):
    for src_ref, dst_ref, (n_rows, n_slabs, transpose) in zip(src_refs, dst_refs, plans):
        slab = src_ref[...]
        if n_rows != n_slabs * src_ref.shape[0]:
            row = jnp.minimum(step, n_slabs - 1) * src_ref.shape[0] + lax.broadcasted_iota(jnp.int32, slab.shape, 0)
            slab = jnp.where(row < n_rows, slab, 0.0)
        dst_ref[...] = (slab.T if transpose else slab).astype(BF16)


def _swiglu_partial(xn, wg, wu, wd, ff_sub, after_first=None, merge_narrow=False):
    width = wg.shape[1]
    acts = []
    for start in range(0, width, ff_sub):
        cols = slice(start, min(start + ff_sub, width))
        sub = cols.stop - start
        if merge_narrow and 2 * sub <= ff_sub:
            both = _dot(xn, jnp.concatenate([wg[:, cols], wu[:, cols]], axis=1))
            gate, up = both[:, :sub], both[:, sub:]
        else:
            gate, up = _dot(xn, wg[:, cols]), _dot(xn, wu[:, cols])
        acts.append((_silu(gate) * up).astype(BF16))
        if start == 0 and after_first is not None:
            after_first()
    return _dot(jnp.concatenate(acts, axis=1), wd)


def _ffn_body(*refs, n_steps, tail_width, ff_sub, head_piece, casts):
    refs = list(refs)
    x_ref, gpre_ref, gpost_ref, wg_ref, wu_ref, wd_ref = refs[:6]
    del refs[:6]
    head_ref = refs.pop(0) if head_piece else None
    cast_srcs = refs[:len(casts)]
    o_ref = refs[len(casts)]
    cast_dsts = refs[len(casts) + 1:-1]
    xn_ref = refs[-1]
    i = pl.program_id(0)
    j = pl.program_id(1)
    tf = wg_ref.shape[1]
    computed = (i > 0) if head_piece else True

    def partial_out(width, rows=slice(None)):
        def side():
            if rows.start in (None, 0):
                _run_side_casts(i * n_steps + j, cast_srcs, cast_dsts, casts)
        return _swiglu_partial(xn_ref[rows, :], wg_ref[:, :width], wu_ref[:, :width], wd_ref[:width, :], ff_sub, side,
                               merge_narrow=rows.start is not None)

    if head_piece:
        @pl.when(jnp.logical_and(i == 0, j < o_ref.shape[0] // head_piece))
        def _():
            o_ref[pl.ds(pl.multiple_of(j * head_piece, head_piece), head_piece), :] = head_ref[...]

    @pl.when(jnp.logical_and(computed, j == 0))
    def _():
        xn_ref[...] = _rmsnorm(x_ref[...], gpre_ref[...]).astype(BF16)
        o_ref[...] = partial_out(tf)

    @pl.when(jnp.logical_and(computed, jnp.logical_and(j > 0, j < n_steps - 1)))
    def _():
        o_ref[...] += partial_out(tf)

    @pl.when(jnp.logical_and(computed, j == n_steps - 1))
    def _():
        for start in range(0, x_ref.shape[0], FFN_FINISH_ROWS):
            rows = slice(start, start + FFN_FINISH_ROWS)
            h = o_ref[rows, :] + partial_out(tail_width, rows)
            o_ref[rows, :] = x_ref[rows, :] + HALF_STEP * _rmsnorm(h, gpost_ref[...])


def _ffn(x, g_pre, g_post, wg, wu, wd, head=None, side_casts=()):
    t, d = x.shape
    f = wg.shape[1]
    tm, tf, ff_sub = FFN_ROW_TILE, FFN_FF_TILE, FFN_FF_SUB
    n_steps = pl.cdiv(f, tf)
    tail_width = f - (n_steps - 1) * tf
    assert t % tm == 0 and n_steps >= 2 and tail_width % LANES == 0
    row = pl.BlockSpec((tm, d), lambda i, j: (i, 0))
    vec = pl.BlockSpec((1, d), lambda i, j: (0, 0))
    ff_tile = (lambda i, j: jnp.where(i == 0, 0, j)) if head is not None else (lambda i, j: j)
    head_in, head_specs, head_piece = [], [], None
    if head is not None:
        assert head.shape == (tm, d)
        head_piece = _slab_rows(tm, n_steps, SUBLANES)
        n_head = tm // head_piece
        head_in = [head]
        head_specs = [pl.BlockSpec((head_piece, d), lambda i, j: (jnp.where(i == 0, jnp.minimum(j, n_head - 1), n_head - 1), 0))]
    assert head is None or not side_casts
    cast_in, cast_in_specs, cast_shapes, cast_out_specs, cast_plans = _side_cast_specs(
        side_casts, lambda i, j: i * n_steps + j, (t // tm) * n_steps)
    return pl.pallas_call(
        functools.partial(_ffn_body, n_steps=n_steps, tail_width=tail_width, ff_sub=ff_sub,
                          head_piece=head_piece, casts=cast_plans),
        grid=(t // tm, n_steps),
        in_specs=[pl.BlockSpec((tm, d), lambda i, j: (jnp.maximum(i, 1), 0)) if head is not None else row, vec, vec,
                  pl.BlockSpec((d, tf), lambda i, j: (0, ff_tile(i, j))),
                  pl.BlockSpec((d, tf), lambda i, j: (0, ff_tile(i, j))),
                  pl.BlockSpec((tf, d), lambda i, j: (ff_tile(i, j), 0))] + head_specs + cast_in_specs,
        out_specs=[row] + cast_out_specs,
        out_shape=[jax.ShapeDtypeStruct((t, d), F32)] + cast_shapes,
        scratch_shapes=[pltpu.VMEM((tm, d), BF16)],
        compiler_params=_params(("arbitrary", "arbitrary") if side_casts else ("parallel", "arbitrary")),
        name="ffn",
    )(x, g_pre, g_post, wg, wu, wd, *head_in, *cast_in)


def _ffn_head_body(*refs, n_steps, tail_width, ff_sub, casts):
    x_ref, gpre_ref, gpost_ref, wg32_ref, wu32_ref, wd32_ref = refs[:6]
    cast_srcs = refs[6:6 + len(casts)]
    o_ref, wg16_ref, wu16_ref, wd16_ref = refs[6 + len(casts):10 + len(casts)]
    cast_dsts = refs[10 + len(casts):-1]
    xn_ref = refs[-1]
    j = pl.program_id(0)
    tf = wg32_ref.shape[1]

    def partial_out(width):
        wg = wg32_ref[:, :width].astype(BF16)
        wu = wu32_ref[:, :width].astype(BF16)
        wd = wd32_ref[:width, :].astype(BF16)
        wg16_ref[:, :width] = wg
        wu16_ref[:, :width] = wu
        wd16_ref[:width, :] = wd
        return _swiglu_partial(xn_ref[...], wg, wu, wd, ff_sub,
                               lambda: _run_side_casts(j, cast_srcs, cast_dsts, casts))

    @pl.when(j == 0)
    def _():
        xn_ref[...] = _rmsnorm(x_ref[...], gpre_ref[...]).astype(BF16)
        o_ref[...] = partial_out(tf)

    @pl.when(jnp.logical_and(j > 0, j < n_steps - 1))
    def _():
        o_ref[...] += partial_out(tf)

    @pl.when(j == n_steps - 1)
    def _():
        h = o_ref[...] + partial_out(tail_width)
        o_ref[...] = x_ref[...] + HALF_STEP * _rmsnorm(h, gpost_ref[...])


def _ffn_head(x, g_pre, g_post, w_gate, w_up, w_down, layer, side_casts=()):
    d = x.shape[1]
    f = w_gate.shape[2]
    tm, tf, ff_sub = FFN_ROW_TILE, FFN_HEAD_FF_TILE, FFN_FF_SUB
    n_steps = pl.cdiv(f, tf)
    tail_width = f - (n_steps - 1) * tf
    assert n_steps >= 2 and tail_width % LANES == 0
    const = lambda j: (0, 0)
    vec = pl.BlockSpec((1, d), const)
    cast_in, cast_in_specs, cast_shapes, cast_out_specs, cast_plans = _side_cast_specs(
        side_casts, lambda j: j, n_steps)
    return pl.pallas_call(
        functools.partial(_ffn_head_body, n_steps=n_steps, tail_width=tail_width, ff_sub=ff_sub, casts=cast_plans),
        grid=(n_steps,),
        in_specs=[pl.BlockSpec((tm, d), const, pipeline_mode=pl.Buffered(1)), vec, vec,
                  pl.BlockSpec((None, d, tf), lambda j: (layer, 0, j)),
                  pl.BlockSpec((None, d, tf), lambda j: (layer, 0, j)),
                  pl.BlockSpec((None, tf, d), lambda j: (layer, j, 0))] + cast_in_specs,
        out_specs=[pl.BlockSpec((tm, d), const),
                   pl.BlockSpec((d, tf), lambda j: (0, j)),
                   pl.BlockSpec((d, tf), lambda j: (0, j)),
                   pl.BlockSpec((tf, d), lambda j: (j, 0))] + cast_out_specs,
        out_shape=[jax.ShapeDtypeStruct((tm, d), F32),
                   jax.ShapeDtypeStruct((d, f), BF16), jax.ShapeDtypeStruct((d, f), BF16),
                   jax.ShapeDtypeStruct((f, d), BF16)] + cast_shapes,
        scratch_shapes=[pltpu.VMEM((tm, d), BF16)],
        compiler_params=_params(("arbitrary",)),
        name="ffn_head",
    )(x, g_pre, g_post, w_gate, w_up, w_down, *cast_in)


def _sgu_body(x_ref, gpre_ref, wu_ref, wv_ref, gs_ref, ws_ref, bs_ref, o_ref, xn_ref, u_ref, v_ref):
    tm = x_ref.shape[0]
    heads = ws_ref.shape[0]
    xn = _rmsnorm(x_ref[...], gpre_ref[...]).astype(BF16)
    xn_ref[...] = xn
    v_ref[...] = _rmsnorm(_gelu(_dot_nt(xn, wv_ref[...])), gs_ref[...]).astype(BF16)
    u_ref[...] = _gelu(_dot_nt(xn, wu_ref[...]))
    rows = lax.broadcasted_iota(jnp.int32, (A_CHUNK, A_CHUNK), 0)
    cols = lax.broadcasted_iota(jnp.int32, (A_CHUNK, A_CHUNK), 1)
    causal = rows >= cols
    for h in range(heads):
        w = jnp.where(causal, ws_ref[h], 0.0).astype(BF16)
        bias = bs_ref[h]
        lanes = pl.ds(h * A_HEAD_DIM, A_HEAD_DIM)
        for c in range(tm // A_CHUNK):
            rws = pl.ds(c * A_CHUNK, A_CHUNK)
            mixed = _dot(w, v_ref[rws, lanes]) + bias
            o_ref[rws, lanes] = (u_ref[rws, lanes] * mixed).astype(BF16)


def _inproj_sgu(x, g_pre, w_uv, g_sgu, w_spatial, b_full):
    t, d = x.shape
    aw = w_uv.shape[0] // 2
    heads = w_spatial.shape[0]
    tm = PROJ_ROW_TILE
    assert t % tm == 0 and tm % A_CHUNK == 0 and aw == heads * A_HEAD_DIM
    const2 = lambda i: (0, 0)
    const3 = lambda i: (0, 0, 0)
    return pl.pallas_call(
        _sgu_body,
        grid=(t // tm,),
        in_specs=[
            pl.BlockSpec((tm, d), lambda i: (i, 0)),
            pl.BlockSpec((1, d), const2),
            pl.BlockSpec((aw, d), const2, pipeline_mode=pl.Buffered(1)),
            pl.BlockSpec((aw, d), lambda i: (1, 0), pipeline_mode=pl.Buffered(1)),
            pl.BlockSpec((1, aw), const2),
            pl.BlockSpec((heads, A_CHUNK, A_CHUNK), const3),
            pl.BlockSpec((heads, A_CHUNK, A_HEAD_DIM), const3),
        ],
        out_specs=[pl.BlockSpec((tm, aw), lambda i: (i, 0)), pl.BlockSpec((tm, d), lambda i: (i, 0))],
        out_shape=[jax.ShapeDtypeStruct((t, aw), BF16), jax.ShapeDtypeStruct((t, d), BF16)],
        scratch_shapes=[pltpu.VMEM((tm, aw), F32), pltpu.VMEM((tm, aw), BF16)],
        compiler_params=_params(("parallel",)),
        name="inproj_sgu",
    )(x, g_pre, w_uv, w_uv, g_sgu, w_spatial, b_full)


Z_OFF = 0
XBC_OFF = B_WIDTH
DT_OFF = B_WIDTH + B_CONV_DIM
ZXD_WIDTH = DT_OFF + LANES
CONV_HALO = SUBLANES


def _split_cols(v, pieces):
    parts = []
    rest = v
    for _ in range(pieces):
        part = rest.astype(BF16)
        parts.append(part)
        rest = rest - part.astype(F32)
    return jnp.concatenate(parts, axis=1)


def _ssd_mix(zxd_ref, convw_ref, convb_ref, dtb_ref, alog_ref, dskip_ref, norm_ref, o_ref,
             ext_ref, state_ref, y_ref, side_work=()):
    cl = B_CHUNK
    side_work = list(side_work)

    def side(n=1):
        for _ in range(n):
            if side_work:
                side_work.pop(0)()

    ext_ref[CONV_HALO:CONV_HALO + cl, :] = zxd_ref[:, XBC_OFF:XBC_OFF + B_CONV_DIM]
    ext = ext_ref[...]
    conv = convw_ref[0:1, :] * ext
    for k in range(1, B_CONV):
        conv = pltpu.roll(conv, 1, axis=0) + convw_ref[k:k + 1, :] * ext
    conv = conv[CONV_HALO:, :] + convb_ref[...]
    ext_ref[0:CONV_HALO, :] = ext_ref[cl:cl + CONV_HALO, :]
    side()
    xbc = _silu(conv)
    xs = xbc[:, :B_WIDTH]

    dt = jax.nn.softplus(zxd_ref[:, DT_OFF:DT_OFF + LANES] + dtb_ref[...])
    da = dt * (-jnp.exp(alog_ref[...]))
    rows = lax.broadcasted_iota(jnp.int32, (cl, cl), 0)
    cols = lax.broadcasted_iota(jnp.int32, (cl, cl), 1)
    causal = rows >= cols
    tril = jnp.where(causal, 1.0, 0.0).astype(BF16)
    acs = sum(_dot(tril, part) for part in _split3(da))
    acs_t = acs.T
    side()

    e_rows = lax.broadcasted_iota(jnp.int32, (SPLIT_PIECES * LANES, B_WIDTH), 0)
    e_cols = lax.broadcasted_iota(jnp.int32, (SPLIT_PIECES * LANES, B_WIDTH), 1)
    expand = jnp.where(e_cols // B_HEAD_DIM == e_rows % LANES, 1.0, 0.0).astype(BF16)
    acs_e = _dot(_split_cols(acs, SPLIT_PIECES), expand)
    dt_e = _dot(_split_cols(dt, SPLIT_PIECES), expand)

    x = xs * dt_e
    even_head = (lax.broadcasted_iota(jnp.int32, (cl, B_WIDTH), 1) // B_HEAD_DIM) % 2 == 0
    x_even = jnp.where(even_head, x, 0.0).astype(BF16)
    x_odd = jnp.where(even_head, 0.0, x).astype(BF16)
    acs_last = acs_e[cl - 1:cl, :]
    decay_from_start = jnp.exp(acs_e)
    x_to_end = (x * jnp.exp(acs_last - acs_e)).astype(BF16)
    chunk_decay = jnp.exp(acs_last)
    side()

    heads_per_group = B_HEADS // B_GROUPS
    for g in range(B_GROUPS):
        gl = slice(g * B_GROUP_WIDTH, (g + 1) * B_GROUP_WIDTH)
        b_off = B_WIDTH + g * B_STATE
        c_off = B_WIDTH + B_GROUPS * B_STATE + g * B_STATE
        bc_t = xbc[:, b_off:b_off + B_STATE].T.astype(BF16)
        cc = xbc[:, c_off:c_off + B_STATE].astype(BF16)
        cb = _dot(cc, bc_t)
        state = state_ref[g]
        y_off = _dot(cc, state.astype(BF16)) * decay_from_start[:, gl]
        state_ref[g] = state * chunk_decay[:, gl] + _dot(bc_t, x_to_end[:, gl])
        for pair in range(heads_per_group // 2):
            head_a = g * heads_per_group + 2 * pair
            pl_off = head_a * B_HEAD_DIM
            y_pair = y_off[:, pair * LANES:(pair + 1) * LANES]
            for head, x_half in ((head_a, x_even), (head_a + 1, x_odd)):
                diff = acs[:, head:head + 1] - acs_t[head:head + 1, :]
                m = (cb * jnp.exp(jnp.where(causal, diff, -jnp.inf))).astype(BF16)
                y_pair = y_pair + _dot(m, x_half[:, pl_off:pl_off + LANES])
            y_ref[:, pl_off:pl_off + LANES] = y_pair
            side()

    side(len(side_work))
    y = (y_ref[...] + dskip_ref[...] * xs) * _silu(zxd_ref[:, Z_OFF:Z_OFF + B_WIDTH])
    for g in range(B_GROUPS):
        gl = slice(g * B_GROUP_WIDTH, (g + 1) * B_GROUP_WIDTH)
        yg = y[:, gl]
        yg = yg * lax.rsqrt(jnp.mean(yg * yg, axis=-1, keepdims=True) + EPS)
        o_ref[:, gl] = (yg * norm_ref[:, gl]).astype(BF16)


def _ssd_body(*refs, nc, n_chunks, casts):
    xn_ref, w_ref, convw_ref, convb_ref, dtb_ref, alog_ref, dskip_ref, norm_ref = refs[:8]
    cast_srcs = refs[8:8 + len(casts)]
    o_ref = refs[8 + len(casts)]
    cast_dsts = refs[9 + len(casts):-5]
    zxd0_ref, zxd1_ref, ext_ref, state_ref, y_ref = refs[-5:]
    s = pl.program_id(0)
    bufs = (zxd0_ref, zxd1_ref)

    def project_pieces(dst_ref):
        def piece(start):
            cols = slice(start, min(start + PROJ_PIECE, ZXD_WIDTH))

            def run():
                dst_ref[:, cols] = _dot(xn_ref[...], w_ref[:, cols])
            return run

        def casts_then(run):
            def both():
                _run_side_casts(s, cast_srcs, cast_dsts, casts)
                run()
            return both
        pieces = [piece(start) for start in range(0, ZXD_WIDTH, PROJ_PIECE)]
        return pieces[:-1] + [casts_then(pieces[-1])]

    def mix(src_ref, side_work=()):
        _ssd_mix(src_ref, convw_ref, convb_ref, dtb_ref, alog_ref, dskip_ref, norm_ref, o_ref,
                 ext_ref, state_ref, y_ref, side_work)

    @pl.when(lax.rem(s - 1, nc) == 0)
    def _():
        ext_ref[0:CONV_HALO, :] = jnp.zeros((CONV_HALO, B_CONV_DIM), F32)
        state_ref[...] = jnp.zeros_like(state_ref)

    @pl.when(s == 0)
    def _():
        for run in project_pieces(bufs[0]):
            run()

    for parity in range(2):
        @pl.when(jnp.logical_and(jnp.logical_and(s > 0, s < n_chunks), lax.rem(s, 2) == parity))
        def _():
            mix(bufs[1 - parity], project_pieces(bufs[parity]))

    @pl.when(s == n_chunks)
    def _():
        _run_side_casts(s, cast_srcs, cast_dsts, casts)
        mix(bufs[(n_chunks - 1) % 2])


def _inproj_ssd(xn, w_zxd, batch, conv_w, conv_b, dt_bias, a_log, d_skip, ssm_norm, side_casts=()):
    t, d = xn.shape
    seq = t // batch
    assert seq % B_CHUNK == 0 and w_zxd.shape[1] == ZXD_WIDTH
    nc = seq // B_CHUNK
    n_chunks = batch * nc
    const = lambda s: (0, 0)
    cast_in, cast_in_specs, cast_shapes, cast_out_specs, cast_plans = _side_cast_specs(
        side_casts, lambda s: s, n_chunks + 1)
    return pl.pallas_call(
        functools.partial(_ssd_body, nc=nc, n_chunks=n_chunks, casts=cast_plans),
        grid=(n_chunks + 1,),
        in_specs=[
            pl.BlockSpec((B_CHUNK, d), lambda s: (jnp.minimum(s, n_chunks - 1), 0)),
            pl.BlockSpec((d, ZXD_WIDTH), const),
            pl.BlockSpec((B_CONV, B_CONV_DIM), const),
            pl.BlockSpec((1, B_CONV_DIM), const),
            pl.BlockSpec((1, LANES), const),
            pl.BlockSpec((1, LANES), const),
            pl.BlockSpec((1, B_WIDTH), const),
            pl.BlockSpec((1, B_WIDTH), const),
        ] + cast_in_specs,
        out_specs=[pl.BlockSpec((B_CHUNK, B_WIDTH), lambda s: (jnp.maximum(s - 1, 0), 0))] + cast_out_specs,
        out_shape=[jax.ShapeDtypeStruct((t, B_WIDTH), BF16)] + cast_shapes,
        scratch_shapes=[
            pltpu.VMEM((B_CHUNK, ZXD_WIDTH), F32),
            pltpu.VMEM((B_CHUNK, ZXD_WIDTH), F32),
            pltpu.VMEM((CONV_HALO + B_CHUNK, B_CONV_DIM), F32),
            pltpu.VMEM((B_GROUPS, B_STATE, B_GROUP_WIDTH), F32),
            pltpu.VMEM((B_CHUNK, B_WIDTH), F32),
        ],
        compiler_params=_params(("arbitrary",)),
        name="inproj_ssd",
    )(xn, w_zxd, conv_w, conv_b, dt_bias, a_log, d_skip, ssm_norm, *cast_in)


def _outproj_body(x_ref, ya_ref, yb_ref, w_ref, g_ref, o_ref):
    for start in range(0, x_ref.shape[0], FFN_FINISH_ROWS):
        rows = pl.ds(start, FFN_FINISH_ROWS)
        h = _dot(jnp.concatenate([ya_ref[rows, :], yb_ref[rows, :]], axis=1), w_ref[...])
        o_ref[rows, :] = x_ref[rows, :] + _rmsnorm(h, g_ref[...])


def _outproj(x, y_a, y_b, w_out, g_post):
    t, d = x.shape
    ka, kb = y_a.shape[1], y_b.shape[1]
    tm = PROJ_ROW_TILE
    assert t % tm == 0 and ka == kb and w_out.shape[0] == ka + kb
    const = lambda i: (0, 0)
    return pl.pallas_call(
        _outproj_body,
        grid=(t // tm,),
        in_specs=[
            pl.BlockSpec((tm, d), lambda i: (i, 0)),
            pl.BlockSpec((tm, ka), lambda i: (i, 0)),
            pl.BlockSpec((tm, kb), lambda i: (i, 0)),
            pl.BlockSpec((ka + kb, d), const, pipeline_mode=pl.Buffered(1)),
            pl.BlockSpec((1, d), const),
        ],
        out_specs=pl.BlockSpec((tm, d), lambda i: (i, 0)),
        out_shape=jax.ShapeDtypeStruct((t, d), F32),
        compiler_params=_params(("parallel",)),
        name="outproj",
    )(x, y_a, y_b, w_out, g_post)


def _row(v):
    return v.reshape(1, -1).astype(F32)


def kernel(x, ffn1_norm_pre, ffn1_norm_post, ffn1_w_gate, ffn1_w_up, ffn1_w_down, mix_norm_pre, mix_norm_post, w_in, sgu_norm, w_spatial, b_spatial, conv_w, conv_b, dt_bias, a_log, d_skip, ssm_norm, w_out, ffn2_norm_pre, ffn2_norm_post, ffn2_w_gate, ffn2_w_up, ffn2_w_down):
    batch, seq, d = x.shape
    depth = ffn1_norm_pre.shape[0]
    a_width = sgu_norm.shape[1]
    xf = x.reshape(batch * seq, d)
    for l in range(depth):
        w_in_t = jnp.swapaxes(w_in, 1, 2)
        n_in = w_in.shape[2]
        whole = lambda w: (w, l, 0, w.shape[1], w.shape[1], False)
        head, w1_gate, w1_up, w1_down, w_uv, w_zxd = _ffn_head(
            xf, _row(ffn1_norm_pre[l]), _row(ffn1_norm_post[l]), ffn1_w_gate, ffn1_w_up, ffn1_w_down, l,
            ((w_in_t, l, 0, 2 * a_width, 2 * a_width, False),
             (w_in_t, l, 2 * a_width, n_in - 2 * a_width, ZXD_WIDTH, True)))
        (xf,) = _ffn(xf, _row(ffn1_norm_pre[l]), _row(ffn1_norm_post[l]), w1_gate, w1_up, w1_down, head=head)

        g_mix = _row(mix_norm_pre[l])
        b_full = jnp.broadcast_to(b_spatial[l][:, :, None], b_spatial[l].shape + (A_HEAD_DIM,))
        y_a, xn = _inproj_sgu(xf, g_mix, w_uv, _row(sgu_norm[l]), w_spatial[l], b_full)

        pad_heads = lambda v: jnp.pad(v.astype(F32), (0, LANES - B_HEADS)).reshape(1, LANES)
        y_b, w2_gate, w2_up, w2_down, wo = _inproj_ssd(
            xn, w_zxd, batch, conv_w[l], _row(conv_b[l]), pad_heads(dt_bias[l]), pad_heads(a_log[l]),
            _row(jnp.repeat(d_skip[l], B_HEAD_DIM)), _row(ssm_norm[l]),
            (whole(ffn2_w_gate), whole(ffn2_w_up), whole(ffn2_w_down), whole(w_out)))

        xf = _outproj(xf, y_a, y_b, wo, _row(mix_norm_post[l]))

        (xf,) = _ffn(xf, _row(ffn2_norm_pre[l]), _row(ffn2_norm_post[l]), w2_gate, w2_up, w2_down)
    return xf.reshape(batch, seq, d)
```
